```python
import math
import jax
import jax.numpy as jnp
from jax import lax
import numpy as np

D_MODEL = 1024
BATCH = 4
SEQ = 4096
DEPTH = 1

NORM_EPS = 1e-6
GMLP_WIDTH = 1024
GMLP_GROUPS = 8
GMLP_GROUP_DIM = GMLP_WIDTH // GMLP_GROUPS
GMLP_CHUNK = 128
ATT_HEADS = 8
HEAD_DIM = 128
ATT_WIDTH = ATT_HEADS * HEAD_DIM
MOBA_BLOCK = 256
MOBA_TOPK = 3
MOBA_QCHUNK = 32
REL_BUCKETS = 32
REL_MAX_DIST = 128
N_GROUPS = 4
EXPERTS_PER_GROUP = 8
N_EXPERTS = N_GROUPS * EXPERTS_PER_GROUP
TOPK_IN_GROUP = 2
D_EXPERT = 256
IN_SPLITS = (GMLP_WIDTH, GMLP_WIDTH, ATT_WIDTH, ATT_WIDTH, ATT_WIDTH, D_MODEL, D_MODEL)
IN_COLS = 2 * GMLP_WIDTH + 3 * ATT_WIDTH + 2 * D_MODEL

kernel_name = "hybrid_gmlp_moba_hmoe_block"


def rmsnorm(x, g):
    xf = x.astype(jnp.float32)
    y = xf * lax.rsqrt(jnp.mean(xf * xf, axis=-1, keepdims=True) + NORM_EPS)
    return (y * g.astype(jnp.float32)).astype(x.dtype)


def layernorm(x, g, b):
    xf = x.astype(jnp.float32)
    mu = jnp.mean(xf, axis=-1, keepdims=True)
    var = jnp.mean(jnp.square(xf - mu), axis=-1, keepdims=True)
    y = (xf - mu) * lax.rsqrt(var + NORM_EPS)
    return (y * g.astype(jnp.float32) + b.astype(jnp.float32)).astype(x.dtype)


def t5_bucket(n):
    n = jnp.maximum(n, 0)
    max_exact = REL_BUCKETS // 2
    nf = jnp.maximum(n, max_exact).astype(jnp.float32)
    large = max_exact + (jnp.log(nf / max_exact) / math.log(REL_MAX_DIST / max_exact)
                         * (REL_BUCKETS - max_exact)).astype(jnp.int32)
    large = jnp.minimum(large, REL_BUCKETS - 1)
    return jnp.where(n < max_exact, n, large)


def gmlp_spatial_gating(u, v, ln_g, ln_b, w_spatial, b_spatial):
    B, S, W = u.shape
    v = layernorm(v, ln_g, ln_b)
    n_chunks = S // GMLP_CHUNK
    vc = v.reshape(B, n_chunks, GMLP_CHUNK, GMLP_GROUPS, GMLP_GROUP_DIM)
    causal = jnp.tril(jnp.ones((GMLP_CHUNK, GMLP_CHUNK), dtype=bool))
    ws = jnp.where(causal[None], w_spatial, 0.0).astype(v.dtype)
    mixed = jnp.einsum('gts,bnsgc->bntgc', ws, vc)
    mixed = mixed + b_spatial.T.astype(v.dtype)[None, None, :, :, None]
    return u * mixed.reshape(B, S, W)


def moba_attention(q, k, v, rel_bias):
    B, H, S, Dh = q.shape
    nb = -(-S // MOBA_BLOCK)
    s_pad = nb * MOBA_BLOCK
    pad = ((0, 0), (0, 0), (0, s_pad - S), (0, 0))
    kb = jnp.pad(k, pad).reshape(B, H, nb, MOBA_BLOCK, Dh)
    vb = jnp.pad(v, pad).reshape(B, H, nb, MOBA_BLOCK, Dh)
    k_mean = jnp.mean(kb.astype(jnp.float32), axis=3)
    pos = jnp.arange(S, dtype=jnp.int32)
    q_blk = pos // MOBA_BLOCK
    gate = jnp.einsum('bhsd,bhnd->bhsn', q.astype(jnp.float32), k_mean)
    fully_past = jnp.arange(nb, dtype=jnp.int32)[None, :] < q_blk[:, None]
    gate = jnp.where(fully_past[None, None], gate, -jnp.inf)
    k_sel = min(MOBA_TOPK, nb)
    _, sel_idx = lax.top_k(gate, k_sel)
    sel_valid = jnp.arange(k_sel, dtype=jnp.int32)[None, :] < q_blk[:, None]
    sel_idx = jnp.where(sel_valid[None, None], sel_idx, 0)

    scale = HEAD_DIM ** -0.5
    offs = jnp.arange(MOBA_BLOCK, dtype=jnp.int32)
    b_ix = jnp.arange(B)[:, None, None, None]
    h_ix = jnp.arange(H)[None, :, None, None]
    h_ix5 = jnp.arange(H)[None, :, None, None, None]
    n_qc = S // MOBA_QCHUNK

    def one_chunk(c):
        q0 = c * MOBA_QCHUNK
        qc = lax.dynamic_slice_in_dim(q, q0, MOBA_QCHUNK, axis=2)
        idx = lax.dynamic_slice_in_dim(sel_idx, q0, MOBA_QCHUNK, axis=2)
        valid = lax.dynamic_slice_in_dim(sel_valid, q0, MOBA_QCHUNK, axis=0)
        qpos = q0 + jnp.arange(MOBA_QCHUNK, dtype=jnp.int32)
        k_past = kb[b_ix, h_ix, idx]
        v_past = vb[b_ix, h_ix, idx]
        s_past = jnp.einsum('bhqd,bhqkjd->bhqkj', qc, k_past).astype(jnp.float32) * scale
        kpos_past = idx[..., None] * MOBA_BLOCK + offs
        bucket_past = t5_bucket(qpos[None, None, :, None, None] - kpos_past)
        s_past = s_past + rel_bias[bucket_past, h_ix5].astype(jnp.float32)
        s_past = jnp.where(valid[None, None, :, :, None], s_past, -jnp.inf)
        blk = q0 // MOBA_BLOCK
        k_own = lax.dynamic_index_in_dim(kb, blk, axis=2, keepdims=False)
        v_own = lax.dynamic_index_in_dim(vb, blk, axis=2, keepdims=False)
        rel = qpos[:, None] - (blk * MOBA_BLOCK + offs)[None, :]
        bias_own = jnp.moveaxis(rel_bias[t5_bucket(rel)], -1, 0).astype(jnp.float32)
        s_own = jnp.einsum('bhqd,bhjd->bhqj', qc, k_own).astype(jnp.float32) * scale + bias_own[None]
        s_own = jnp.where((rel >= 0)[None, None], s_own, -jnp.inf)
        logits = jnp.concatenate([s_past.reshape(B, H, MOBA_QCHUNK, k_sel * MOBA_BLOCK), s_own], axis=-1)
        p = jax.nn.softmax(logits, axis=-1).astype(q.dtype)
        p_past = p[..., :k_sel * MOBA_BLOCK].reshape(B, H, MOBA_QCHUNK, k_sel, MOBA_BLOCK)
        p_own = p[..., k_sel * MOBA_BLOCK:]
        out = (jnp.einsum('bhqkj,bhqkjd->bhqd', p_past, v_past)
               + jnp.einsum('bhqj,bhjd->bhqd', p_own, v_own))
        return out.astype(q.dtype)

    outs = lax.map(one_chunk, jnp.arange(n_qc, dtype=jnp.int32))
    return jnp.transpose(outs, (1, 2, 0, 3, 4)).reshape(B, H, S, Dh)


def hierarchical_moe(xn, w_gr, b_gr, w_er, b_er, w1, w3, w2):
    B, S, D = xn.shape
    xt = xn.reshape(B * S, D)
    g_prob = jax.nn.softmax((xt @ w_gr + b_gr).astype(jnp.float32), axis=-1)
    g_w, g_idx = lax.top_k(g_prob, 1)
    e_logits = (jnp.einsum('td,gde->tge', xt, w_er) + b_er).astype(jnp.float32)
    e_logits = jnp.take_along_axis(e_logits, g_idx[:, :, None], axis=1)[:, 0]
    e_val, e_idx = lax.top_k(e_logits, TOPK_IN_GROUP)
    e_w = jax.nn.softmax(e_val, axis=-1) * g_w
    expert_id = g_idx * EXPERTS_PER_GROUP + e_idx
    combine = jnp.sum(jax.nn.one_hot(expert_id, N_EXPERTS, dtype=jnp.float32) * e_w[..., None],
                      axis=1).astype(xt.dtype)
    y = jnp.zeros_like(xt)
    for g in range(N_GROUPS):
        sl = slice(g * EXPERTS_PER_GROUP, (g + 1) * EXPERTS_PER_GROUP)
        a = jnp.einsum('td,edf->tef', xt, w1[sl])
        b = jnp.einsum('td,edf->tef', xt, w3[sl])
        hid = jax.nn.silu(a) * b * combine[:, sl, None]
        y = y + jnp.einsum('tef,efd->td', hid, w2[sl])
    return y.reshape(B, S, D)


def setup_inputs(seed: int = 0) -> dict:
    key = jax.random.key(seed)
    ks = jax.random.split(key, 20)
    L = DEPTH

    def nrm(k, shape, scale):
        return jax.random.normal(k, shape, jnp.float32) * scale

    return {
        "x": nrm(ks[0], (BATCH, SEQ, D_MODEL), 1.0),
        "norm_mix_g": 1.0 + nrm(ks[1], (L, D_MODEL), 0.02),
        "w_in": nrm(ks[2], (L, D_MODEL, IN_COLS), D_MODEL ** -0.5),
        "b_gates": nrm(ks[3], (L, 2 * D_MODEL), 0.1),
        "gmlp_ln_g": 1.0 + nrm(ks[4], (L, GMLP_WIDTH), 0.02),
        "gmlp_ln_b": nrm(ks[5], (L, GMLP_WIDTH), 0.02),
        "w_spatial": nrm(ks[6], (L, GMLP_GROUPS, GMLP_CHUNK, GMLP_CHUNK), GMLP_CHUNK ** -0.5),
        "b_spatial": 1.0 + nrm(ks[7], (L, GMLP_GROUPS, GMLP_CHUNK), 0.1),
        "rel_bias": nrm(ks[8], (REL_BUCKETS, ATT_HEADS), 0.5),
        "w_out": nrm(ks[9], (L, D_MODEL, D_MODEL), D_MODEL ** -0.5),
        "norm_ffn_g": 1.0 + nrm(ks[10], (L, D_MODEL), 0.02),
        "w_group_router": nrm(ks[11], (L, D_MODEL, N_GROUPS), D_MODEL ** -0.5),
        "b_group_router": nrm(ks[12], (L, N_GROUPS), 0.01),
        "w_expert_router": nrm(ks[13], (L, N_GROUPS, D_MODEL, EXPERTS_PER_GROUP), D_MODEL ** -0.5),
        "b_expert_router": nrm(ks[14], (L, N_GROUPS, EXPERTS_PER_GROUP), 0.01),
        "w1": nrm(ks[15], (L, N_EXPERTS, D_MODEL, D_EXPERT), D_MODEL ** -0.5),
        "w3": nrm(ks[16], (L, N_EXPERTS, D_MODEL, D_EXPERT), D_MODEL ** -0.5),
        "w2": nrm(ks[17], (L, N_EXPERTS, D_EXPERT, D_MODEL), D_EXPERT ** -0.5),
        "norm_final_g": 1.0 + nrm(ks[18], (D_MODEL,), 0.02),
    }


def reference(x, norm_mix_g, w_in, b_gates, gmlp_ln_g, gmlp_ln_b, w_spatial, b_spatial, rel_bias,
              w_out, norm_ffn_g, w_group_router, b_group_router, w_expert_router, b_expert_router,
              w1, w3, w2, norm_final_g):
    B, S, _ = x.shape
    split_at = [int(c) for c in np.cumsum(IN_SPLITS)[:-1]]
    h = x
    for l in range(DEPTH):
        xn = rmsnorm(h, norm_mix_g[l])
        proj = xn @ w_in[l]
        u, v_g, q, k, v_a, gate_a, gate_b = jnp.split(proj, split_at, axis=-1)
        y_a = gmlp_spatial_gating(jax.nn.gelu(u, approximate=False), jax.nn.gelu(v_g, approximate=False),
                                  gmlp_ln_g[l], gmlp_ln_b[l], w_spatial[l], b_spatial[l])
        heads = lambda t: jnp.transpose(t.reshape(B, S, ATT_HEADS, HEAD_DIM), (0, 2, 1, 3))
        y_b = moba_attention(heads(q), heads(k), heads(v_a), rel_bias)
        y_b = jnp.transpose(y_b, (0, 2, 1, 3)).reshape(B, S, ATT_WIDTH)
        gates = jax.nn.sigmoid(jnp.concatenate([gate_a, gate_b], axis=-1) + b_gates[l])
        g_a, g_b = gates[..., :D_MODEL], gates[..., D_MODEL:]
        h = h + (g_a * y_a + g_b * y_b) @ w_out[l]
        xn = rmsnorm(h, norm_ffn_g[l])
        h = h + hierarchical_moe(xn, w_group_router[l], b_group_router[l], w_expert_router[l],
                                 b_expert_router[l], w1[l], w3[l], w2[l])
    return rmsnorm(h, norm_final_g)
```

```python
import functools
import math

import numpy as np
import jax
import jax.numpy as jnp
from jax import lax
from jax.experimental import pallas as pl
from jax.experimental.pallas import tpu as pltpu

F32 = jnp.float32
BF16 = jnp.bfloat16

D_MODEL = 1024
NORM_EPS = 1e-6
GMLP_GROUPS = 8
GMLP_CHUNK = 128
ATT_HEADS = 8
HEAD_DIM = 128
MOBA_BLOCK = 256
MOBA_TOPK = 3
REL_BUCKETS = 32
REL_MAX_DIST = 128
N_GROUPS = 4
EXPERTS_PER_GROUP = 8
N_EXPERTS = N_GROUPS * EXPERTS_PER_GROUP
D_EXPERT = 256
N_SEGMENTS = 7

LANES = 128
VMEM_LIMIT_BYTES = 56 * 1024 * 1024

SQRT_HALF = math.sqrt(0.5)
LOG2E = math.log2(math.e)
SCORE_SCALE2 = (HEAD_DIM ** -0.5) * LOG2E
MASK_NEG = -(2.0 ** 100)
ROUTER_LANES = 128
GROUP_LANE0 = N_EXPERTS

PROJ_ROWS = 512
MERGE_ROWS = 512
MOE_ROWS = 1024


def _rmsnorm(x, g):
    return x * lax.rsqrt(jnp.mean(x * x, axis=-1, keepdims=True) + NORM_EPS) * g


def _gelu(a):
    return 0.5 * a * (1.0 + lax.erf(a * SQRT_HALF))


def _sigmoid(a):
    return 1.0 / (1.0 + jnp.exp(-a))


def _dot(a, b):
    return jnp.dot(a, b, preferred_element_type=F32)


def _proj_kernel(x_ref, ng_ref, w_ref, bg_ref, lng_ref, lnb_ref, ws_ref, bs_ref,
                 ya_ref, q_ref, k_ref, v_ref, gb_ref, kmean_ref,
                 xn_scr, vln_scr, mix_scr):
    rows = x_ref.shape[0]
    d = D_MODEL
    xn_scr[...] = _rmsnorm(x_ref[...], ng_ref[...]).astype(BF16)

    def seg(i):
        return _dot(xn_scr[...], w_ref[:, i * d:(i + 1) * d])

    v = _gelu(seg(1))
    mu = jnp.mean(v, axis=-1, keepdims=True)
    vc = v - mu
    var = jnp.mean(vc * vc, axis=-1, keepdims=True)
    vln_scr[...] = (vc * lax.rsqrt(var + NORM_EPS) * lng_ref[...] + lnb_ref[...]).astype(BF16)

    t_idx = lax.broadcasted_iota(jnp.int32, (GMLP_CHUNK, GMLP_CHUNK), 0)
    s_idx = lax.broadcasted_iota(jnp.int32, (GMLP_CHUNK, GMLP_CHUNK), 1)
    causal = t_idx >= s_idx
    gd = d // GMLP_GROUPS
    for g in range(GMLP_GROUPS):
        ws = jnp.where(causal, ws_ref[g], 0.0).astype(BF16)
        bias = bs_ref[g]
        for c in range(rows // GMLP_CHUNK):
            r0 = c * GMLP_CHUNK
            mixed = _dot(ws, vln_scr[r0:r0 + GMLP_CHUNK, g * gd:(g + 1) * gd]) + bias
            mix_scr[r0:r0 + GMLP_CHUNK, g * gd:(g + 1) * gd] = mixed

    mix_scr[...] = _gelu(seg(0)) * mix_scr[...]
    ya_ref[...] = (_sigmoid(seg(5) + bg_ref[:, :d]) * mix_scr[...]).astype(BF16)
    gb_ref[...] = _sigmoid(seg(6) + bg_ref[:, d:]).astype(BF16)

    q_ref[...] = seg(2).astype(BF16)
    k = seg(3)
    k_ref[...] = k.astype(BF16)
    for blk in range(rows // MOBA_BLOCK):
        kb = k[blk * MOBA_BLOCK:(blk + 1) * MOBA_BLOCK, :]
        kmean_ref[0, blk:blk + 1, :] = jnp.mean(kb, axis=0, keepdims=True)
    v_ref[...] = seg(4).astype(BF16)


def _proj_call(x2, norm_g, w_in, b_gates, ln_g, ln_b, w_spatial, b_spatial):
    t, d = x2.shape
    rows = PROJ_ROWS
    assert t % rows == 0 and rows % MOBA_BLOCK == 0 and rows % GMLP_CHUNK == 0
    n_tiles = t // rows
    blocks_per_tile = rows // MOBA_BLOCK
    row_spec = pl.BlockSpec((rows, d), lambda i: (i, 0))
    const2 = lambda i: (0, 0)
    const3 = lambda i: (0, 0, 0)
    act = jax.ShapeDtypeStruct((t, d), BF16)
    return pl.pallas_call(
        _proj_kernel,
        grid=(n_tiles,),
        in_specs=[
            row_spec,
            pl.BlockSpec((1, d), const2),
            pl.BlockSpec((d, N_SEGMENTS * d), const2, pipeline_mode=pl.Buffered(1)),
            pl.BlockSpec((1, 2 * d), const2),
            pl.BlockSpec((1, d), const2),
            pl.BlockSpec((1, d), const2),
            pl.BlockSpec((GMLP_GROUPS, GMLP_CHUNK, GMLP_CHUNK), const3),
            pl.BlockSpec((GMLP_GROUPS, GMLP_CHUNK, 1), const3),
        ],
        out_specs=[row_spec, row_spec, row_spec, row_spec, row_spec,
                   pl.BlockSpec((1, blocks_per_tile, d), lambda i: (i, 0, 0))],
        out_shape=[act, act, act, act, act,
                   jax.ShapeDtypeStruct((n_tiles, blocks_per_tile, d), F32)],
        scratch_shapes=[pltpu.VMEM((rows, d), BF16), pltpu.VMEM((rows, d), BF16),
                        pltpu.VMEM((rows, d), F32)],
        compiler_params=pltpu.CompilerParams(
            dimension_semantics=("arbitrary",), vmem_limit_bytes=VMEM_LIMIT_BYTES),
        name="proj_gmlp",
    )(x2, norm_g, w_in, b_gates, ln_g, ln_b, w_spatial, b_spatial)


def _t5_bucket_np(n):
    n = np.maximum(n, 0)
    max_exact = REL_BUCKETS // 2
    nf = np.maximum(n, max_exact).astype(np.float32)
    large = max_exact + (np.log(nf / max_exact) / math.log(REL_MAX_DIST / max_exact)
                         * (REL_BUCKETS - max_exact)).astype(np.int32)
    large = np.minimum(large, REL_BUCKETS - 1)
    return np.where(n < max_exact, n, large).astype(np.int32)


def _bucket_tiles():
    kpos = np.arange(MOBA_BLOCK, dtype=np.int32)[:, None]
    qpos = np.arange(MOBA_BLOCK, dtype=np.int32)[None, :]
    rel = qpos - kpos
    return np.stack([_t5_bucket_np(rel), _t5_bucket_np(rel + MOBA_BLOCK)]), rel


def _bias_kernel(relb_ref, bucket_ref, out_ref):
    h = pl.program_id(0)
    far = relb_ref[REL_BUCKETS - 1, h]
    k_idx = lax.broadcasted_iota(jnp.int32, (MOBA_BLOCK, MOBA_BLOCK), 0)
    q_idx = lax.broadcasted_iota(jnp.int32, (MOBA_BLOCK, MOBA_BLOCK), 1)
    for tile in range(2):
        bucket = bucket_ref[tile]
        bias = jnp.zeros((MOBA_BLOCK, MOBA_BLOCK), F32)
        for b in range(REL_BUCKETS):
            bias = jnp.where(bucket == b, relb_ref[b, h], bias)
        bias2 = (bias - far) * LOG2E
        if tile == 0:
            bias2 = jnp.where(q_idx >= k_idx, bias2, MASK_NEG)
        out_ref[0, tile] = bias2


def _bias_call(rel_bias):
    buckets, _ = _bucket_tiles()
    return pl.pallas_call(
        _bias_kernel,
        grid=(ATT_HEADS,),
        in_specs=[pl.BlockSpec(memory_space=pltpu.SMEM),
                  pl.BlockSpec((2, MOBA_BLOCK, MOBA_BLOCK), lambda h: (0, 0, 0))],
        out_specs=pl.BlockSpec((1, 2, MOBA_BLOCK, MOBA_BLOCK), lambda h: (h, 0, 0, 0)),
        out_shape=jax.ShapeDtypeStruct((ATT_HEADS, 2, MOBA_BLOCK, MOBA_BLOCK), F32),
        compiler_params=pltpu.CompilerParams(dimension_semantics=("arbitrary",)),
        name="t5_bias_tiles",
    )(rel_bias, jnp.asarray(buckets))


def _attn_kernel(qT_ref, k_ref, vT_ref, kmh_ref, kml_ref, oh_ref, bias_ref, o_ref,
                 m_scr, l_scr, acc_scr):
    i = pl.program_id(2)
    nb = kmh_ref.shape[0]
    blk = MOBA_BLOCK
    qT = qT_ref[...]

    gate = _dot(kmh_ref[...], qT) + _dot(kml_ref[...], qT)
    n_idx = lax.broadcasted_iota(jnp.int32, (nb, blk), 0)
    past = n_idx < i
    gate = jnp.where(past, gate, -jnp.inf)
    rank = jnp.zeros((nb, blk), F32)
    for m in range(nb):
        row = gate[m:m + 1, :]
        beats = (row > gate) | ((row == gate) & (m < n_idx))
        rank = rank + jnp.where(beats, 1.0, 0.0)
    keep = (past & (rank < MOBA_TOPK)) | (n_idx == i)
    sel = jnp.where(keep, 0.0, MASK_NEG)
    sel = jnp.concatenate([sel, jnp.zeros((HEAD_DIM - nb, blk), F32)], axis=0)
    qp = jnp.concatenate([qT, sel.astype(BF16)], axis=0)

    def scores(j):
        r0 = pl.multiple_of(j * blk, blk)
        kp = jnp.concatenate([k_ref[pl.ds(r0, blk), :], oh_ref[j]], axis=1)
        return _dot(kp, qp) * SCORE_SCALE2

    s = scores(i) + bias_ref[0]
    m0 = jnp.max(s, axis=0, keepdims=True)
    p = jnp.exp2(s - m0)
    m_scr[...] = m0
    l_scr[...] = jnp.sum(p, axis=0, keepdims=True)
    acc_scr[...] = _dot(vT_ref[i], p.astype(BF16))

    def update(j, s):
        m_old = m_scr[...]
        m_new = jnp.maximum(m_old, jnp.max(s, axis=0, keepdims=True))
        alpha = jnp.exp2(m_old - m_new)
        p = jnp.exp2(s - m_new)
        l_scr[...] = alpha * l_scr[...] + jnp.sum(p, axis=0, keepdims=True)
        acc_scr[...] = alpha * acc_scr[...] + _dot(vT_ref[j], p.astype(BF16))
        m_scr[...] = m_new

    @pl.when(i >= 1)
    def _():
        update(i - 1, scores(i - 1) + bias_ref[1])

    def far_body(j, carry):
        update(j, scores(j))
        return carry

    lax.fori_loop(0, i - 1, far_body, 0)

    o_ref[...] = (acc_scr[...] * (1.0 / l_scr[...])).astype(o_ref.dtype)


def _attn_call(qT, k2, vT, km_hi, km_lo, bias_tiles):
    b, h, nb, hd, blk = vT.shape
    s = nb * blk
    onehot = np.zeros((nb, blk, LANES), np.float32)
    for j in range(nb):
        onehot[j, :, j] = 1.0
    return pl.pallas_call(
        _attn_kernel,
        grid=(b, h, nb),
        in_specs=[
            pl.BlockSpec((None, None, hd, blk), lambda bi, hi, i: (bi, hi, 0, i)),
            pl.BlockSpec((s, hd), lambda bi, hi, i: (bi, hi)),
            pl.BlockSpec((None, None, nb, hd, blk), lambda bi, hi, i: (bi, hi, 0, 0, 0)),
            pl.BlockSpec((None, None, nb, hd), lambda bi, hi, i: (bi, hi, 0, 0)),
            pl.BlockSpec((None, None, nb, hd), lambda bi, hi, i: (bi, hi, 0, 0)),
            pl.BlockSpec((nb, blk, LANES), lambda bi, hi, i: (0, 0, 0)),
            pl.BlockSpec((None, 2, blk, blk), lambda bi, hi, i: (hi, 0, 0, 0)),
        ],
        out_specs=pl.BlockSpec((None, None, hd, blk), lambda bi, hi, i: (bi, hi, 0, i)),
        out_shape=jax.ShapeDtypeStruct((b, h, hd, s), BF16),
        scratch_shapes=[pltpu.VMEM((1, blk), F32), pltpu.VMEM((1, blk), F32),
                        pltpu.VMEM((hd, blk), F32)],
        compiler_params=pltpu.CompilerParams(
            dimension_semantics=("arbitrary", "arbitrary", "arbitrary"),
            vmem_limit_bytes=VMEM_LIMIT_BYTES),
        name="moba_attention",
    )(qT, k2, vT, km_hi, km_lo, jnp.asarray(onehot, BF16), bias_tiles)


def _merge_kernel(x_ref, ya_ref, gb_ref, yb_ref, wo_ref, ng_ref, wrh_ref, wrl_ref, br_ref,
                  h_ref, xn_ref, comb_ref):
    f = lambda r: r[...].astype(F32)
    mix = (f(ya_ref) + f(gb_ref) * f(yb_ref)).astype(BF16)
    h = x_ref[...] + _dot(mix, wo_ref[...])
    h_ref[...] = h
    xn = _rmsnorm(h, ng_ref[...])
    x_hi = xn.astype(BF16)
    xn_ref[...] = x_hi
    x_lo = (xn - x_hi.astype(F32)).astype(BF16)
    logits = (_dot(x_hi, wrh_ref[...]) + _dot(x_lo, wrh_ref[...]) + _dot(x_hi, wrl_ref[...])
              + br_ref[...])
    lane = lax.broadcasted_iota(jnp.int32, logits.shape, 1).astype(F32)
    big = float(ROUTER_LANES)
    neg_inf = -jnp.inf

    gl = jnp.where((lane >= GROUP_LANE0) & (lane < GROUP_LANE0 + N_GROUPS), logits, neg_inf)
    gmax = jnp.max(gl, axis=1, keepdims=True)
    g_w = 1.0 / jnp.sum(jnp.exp(gl - gmax), axis=1, keepdims=True)
    g_idx = jnp.min(jnp.where(gl == gmax, lane, big), axis=1, keepdims=True) - GROUP_LANE0

    e0 = g_idx * EXPERTS_PER_GROUP
    el = jnp.where((lane >= e0) & (lane < e0 + EXPERTS_PER_GROUP), logits, neg_inf)
    m1 = jnp.max(el, axis=1, keepdims=True)
    i1 = jnp.min(jnp.where(el == m1, lane, big), axis=1, keepdims=True)
    el2 = jnp.where(lane == i1, neg_inf, el)
    m2 = jnp.max(el2, axis=1, keepdims=True)
    i2 = jnp.min(jnp.where(el2 == m2, lane, big), axis=1, keepdims=True)
    e2 = jnp.exp(m2 - m1)
    den = 1.0 + e2
    w1 = (1.0 / den) * g_w
    w2 = (e2 / den) * g_w
    comb_ref[...] = jnp.where(lane == i1, w1, jnp.where(lane == i2, w2, 0.0))


def _merge_call(x2, ya, gb, yb, w_out, norm_g, wr_hi, wr_lo, b_router):
    t, d = x2.shape
    rows = MERGE_ROWS
    assert t % rows == 0
    row_spec = pl.BlockSpec((rows, d), lambda i: (i, 0))
    const2 = lambda i: (0, 0)
    return pl.pallas_call(
        _merge_kernel,
        grid=(t // rows,),
        in_specs=[row_spec, row_spec, row_spec, row_spec,
                  pl.BlockSpec((d, d), const2),
                  pl.BlockSpec((1, d), const2),
                  pl.BlockSpec((d, ROUTER_LANES), const2),
                  pl.BlockSpec((d, ROUTER_LANES), const2),
                  pl.BlockSpec((1, ROUTER_LANES), const2)],
        out_specs=[row_spec, row_spec, pl.BlockSpec((rows, ROUTER_LANES), lambda i: (i, 0))],
        out_shape=[jax.ShapeDtypeStruct((t, d), F32), jax.ShapeDtypeStruct((t, d), BF16),
                   jax.ShapeDtypeStruct((t, ROUTER_LANES), F32)],
        compiler_params=pltpu.CompilerParams(
            dimension_semantics=("arbitrary",), vmem_limit_bytes=VMEM_LIMIT_BYTES),
        name="merge_outproj_router",
    )(x2, ya, gb, yb, w_out, norm_g, wr_hi, wr_lo, b_router)


def _moe_kernel(xn_ref, w1_ref, w3_ref, w2_ref, comb_ref, h_ref, ng_ref, out_ref, acc_scr):
    e = pl.program_id(1)

    @pl.when(e == 0)
    def _():
        acc_scr[...] = jnp.zeros_like(acc_scr)

    x = xn_ref[...]
    a = _dot(x, w1_ref[...])
    b = _dot(x, w3_ref[...])
    comb = comb_ref[...]
    lane = lax.broadcasted_iota(jnp.int32, comb.shape, 1)
    c = jnp.sum(jnp.where(lane == e, comb, 0.0), axis=1, keepdims=True)
    hid = (a * _sigmoid(a)) * b * c
    acc_scr[...] += _dot(hid.astype(BF16), w2_ref[...])

    @pl.when(e == pl.num_programs(1) - 1)
    def _():
        out_ref[...] = _rmsnorm(h_ref[...] + acc_scr[...], ng_ref[...])


def _moe_call(xn, w1, w3, w2, comb, h, norm_g):
    t, d = h.shape
    rows = MOE_ROWS
    assert t % rows == 0
    row_spec = pl.BlockSpec((rows, d), lambda i, e: (i, 0))
    return pl.pallas_call(
        _moe_kernel,
        grid=(t // rows, N_EXPERTS),
        in_specs=[row_spec,
                  pl.BlockSpec((None, d, D_EXPERT), lambda i, e: (e, 0, 0)),
                  pl.BlockSpec((None, d, D_EXPERT), lambda i, e: (e, 0, 0)),
                  pl.BlockSpec((None, D_EXPERT, d), lambda i, e: (e, 0, 0)),
                  pl.BlockSpec((rows, ROUTER_LANES), lambda i, e: (i, 0)),
                  row_spec,
                  pl.BlockSpec((1, d), lambda i, e: (0, 0))],
        out_specs=row_spec,
        out_shape=jax.ShapeDtypeStruct((t, d), F32),
        scratch_shapes=[pltpu.VMEM((rows, d), F32)],
        compiler_params=pltpu.CompilerParams(
            dimension_semantics=("arbitrary", "arbitrary"), vmem_limit_bytes=VMEM_LIMIT_BYTES),
        name="moe_experts",
    )(xn, w1, w3, w2, comb, h, norm_g)


def _layer(h, norm_mix_g, w_in, b_gates, gmlp_ln_g, gmlp_ln_b, w_spatial, b_spatial, bias_tiles,
           w_out, norm_ffn_g, w_group_router, b_group_router, w_expert_router, b_expert_router,
           w1, w3, w2, norm_out_g):
    b, s, d = h.shape
    t = b * s
    nb = s // MOBA_BLOCK
    x2 = h.reshape(t, d)
    row = lambda v: v.reshape(1, -1)

    ya, q, k, v, gb, kmean = _proj_call(
        x2, row(norm_mix_g), w_in.astype(BF16), row(b_gates), row(gmlp_ln_g), row(gmlp_ln_b),
        w_spatial, b_spatial[:, :, None])

    qT = jnp.transpose(q.reshape(b, s, ATT_HEADS, HEAD_DIM), (0, 2, 3, 1))
    vT = jnp.transpose(v.reshape(b, nb, MOBA_BLOCK, ATT_HEADS, HEAD_DIM), (0, 3, 1, 4, 2))
    km = jnp.transpose(kmean.reshape(b, nb, ATT_HEADS, HEAD_DIM), (0, 2, 1, 3))
    km_hi = km.astype(BF16)
    km_lo = (km - km_hi.astype(F32)).astype(BF16)
    ybT = _attn_call(qT, k, vT, km_hi, km_lo, bias_tiles)
    yb = jnp.transpose(ybT, (0, 3, 1, 2)).reshape(t, d)

    w_router = jnp.concatenate(
        [jnp.transpose(w_expert_router, (1, 0, 2)).reshape(d, N_EXPERTS), w_group_router,
         jnp.zeros((d, ROUTER_LANES - N_EXPERTS - N_GROUPS), F32)], axis=1)
    b_router = jnp.concatenate(
        [b_expert_router.reshape(-1), b_group_router,
         jnp.zeros((ROUTER_LANES - N_EXPERTS - N_GROUPS,), F32)]).reshape(1, ROUTER_LANES)
    wr_hi = w_router.astype(BF16)
    wr_lo = (w_router - wr_hi.astype(F32)).astype(BF16)
    h2, xn, comb = _merge_call(x2, ya, gb, yb, w_out.astype(BF16), row(norm_ffn_g),
                               wr_hi, wr_lo, b_router)

    out = _moe_call(xn, w1.astype(BF16), w3.astype(BF16), w2.astype(BF16), comb, h2,
                    row(norm_out_g))
    return out.reshape(b, s, d)


def kernel(x, norm_mix_g, w_in, b_gates, gmlp_ln_g, gmlp_ln_b, w_spatial, b_spatial, rel_bias, w_out, norm_ffn_g, w_group_router, b_group_router, w_expert_router, b_expert_router, w1, w3, w2, norm_final_g):
    depth = w_in.shape[0]
    assert depth == 1, "the final rmsnorm is fused into the last layer's expert kernel"
    bias_tiles = _bias_call(rel_bias)
    return _layer(x, norm_mix_g[0], w_in[0], b_gates[0], gmlp_ln_g[0], gmlp_ln_b[0], w_spatial[0],
                  b_spatial[0], bias_tiles, w_out[0], norm_ffn_g[0], w_group_router[0],
                  b_group_router[0], w_expert_router[0], b_expert_router[0], w1[0], w3[0], w2[0],
                  norm_final_g)
```

```python
import functools
import math

import numpy as np
import jax
import jax.numpy as jnp
from jax import lax
from jax.experimental import pallas as pl
from jax.experimental.pallas import tpu as pltpu

F32 = jnp.float32
BF16 = jnp.bfloat16

D_MODEL = 1024
NORM_EPS = 1e-6
GMLP_GROUPS = 8
GMLP_CHUNK = 128
ATT_HEADS = 8
HEAD_DIM = 128
MOBA_BLOCK = 256
MOBA_TOPK = 3
REL_BUCKETS = 32
REL_MAX_DIST = 128
N_GROUPS = 4
EXPERTS_PER_GROUP = 8
N_EXPERTS = N_GROUPS * EXPERTS_PER_GROUP
D_EXPERT = 256
N_SEGMENTS = 7

LANES = 128
TOKEN_SUBLANES = 8
ROUTE_ROWS = 8
VMEM_LIMIT_BYTES = 56 * 1024 * 1024

SQRT_HALF = math.sqrt(0.5)
LOG2E = math.log2(math.e)
SCORE_SCALE2 = (HEAD_DIM ** -0.5) * LOG2E
MASK_NEG = -(2.0 ** 100)
V_ROWS = HEAD_DIM + 16
ROUTER_LANES = 128
ROUTER_UNITS = 48
GROUP_LANE0 = N_EXPERTS

PROJ_ROWS = 512
MERGE_ROWS = 512
EXPERT_ROWS = 256
COMBINE_ROWS = 256
DMA_UNROLL = 8


def _rmsnorm(x, g):
    return x * lax.rsqrt(jnp.mean(x * x, axis=-1, keepdims=True) + NORM_EPS) * g


def _gelu(a):
    return 0.5 * a * (1.0 + lax.erf(a * SQRT_HALF))


def _sigmoid(a):
    return 1.0 / (1.0 + jnp.exp(-a))


def _dot(a, b):
    return jnp.dot(a, b, preferred_element_type=F32)


def _proj_kernel(x_ref, ng_ref, w_ref, bg_ref, lng_ref, lnb_ref, ws_ref, bs_ref,
                 ya_ref, qT_ref, k_ref, vT_ref, gb_ref, kmean_ref,
                 xn_scr, vln_scr, mix_scr):
    rows = x_ref.shape[0]
    d = D_MODEL
    xn_scr[...] = _rmsnorm(x_ref[...], ng_ref[...]).astype(BF16)

    def seg(i):
        return _dot(xn_scr[...], w_ref[:, i * d:(i + 1) * d])

    hd = HEAD_DIM

    v = _gelu(seg(1))
    mu = jnp.mean(v, axis=-1, keepdims=True)
    vc = v - mu
    var = jnp.mean(vc * vc, axis=-1, keepdims=True)
    vln_scr[...] = (vc * lax.rsqrt(var + NORM_EPS) * lng_ref[...] + lnb_ref[...]).astype(BF16)

    mix_scr[...] = _gelu(seg(0)) * _sigmoid(seg(5) + bg_ref[:, :d])

    q = seg(2) * SCORE_SCALE2
    for h in range(ATT_HEADS):
        qT_ref[h] = q[:, h * hd:(h + 1) * hd].T.astype(BF16)

    t_idx = lax.broadcasted_iota(jnp.int32, (GMLP_CHUNK, GMLP_CHUNK), 0)
    s_idx = lax.broadcasted_iota(jnp.int32, (GMLP_CHUNK, GMLP_CHUNK), 1)
    causal = t_idx >= s_idx
    gd = d // GMLP_GROUPS
    n_chunks = rows // GMLP_CHUNK
    for g in range(GMLP_GROUPS):
        ws = jnp.where(causal, ws_ref[g], 0.0).astype(BF16)
        bias = bs_ref[g]
        vg = jnp.concatenate(
            [vln_scr[c * GMLP_CHUNK:(c + 1) * GMLP_CHUNK, g * gd:(g + 1) * gd]
             for c in range(n_chunks)], axis=1)
        mixed = _dot(ws, vg)
        for c in range(n_chunks):
            blk_rows = slice(c * GMLP_CHUNK, (c + 1) * GMLP_CHUNK)
            blk_cols = slice(g * gd, (g + 1) * gd)
            ya_ref[blk_rows, blk_cols] = (
                mix_scr[blk_rows, blk_cols] * (mixed[:, c * gd:(c + 1) * gd] + bias)).astype(BF16)

    v = seg(4)
    for blk in range(rows // MOBA_BLOCK):
        r0 = blk * MOBA_BLOCK
        for h in range(ATT_HEADS):
            vT_ref[h, blk, :hd, :] = v[r0:r0 + MOBA_BLOCK, h * hd:(h + 1) * hd].T.astype(BF16)
            vT_ref[h, blk, hd:, :] = jnp.ones((V_ROWS - hd, MOBA_BLOCK), BF16)

    gb_ref[...] = _sigmoid(seg(6) + bg_ref[:, d:]).astype(BF16)

    k = seg(3)
    k_ref[...] = k.astype(BF16)
    for blk in range(rows // MOBA_BLOCK):
        r0 = blk * MOBA_BLOCK
        kmean_ref[0, blk:blk + 1, :] = jnp.mean(k[r0:r0 + MOBA_BLOCK, :], axis=0, keepdims=True)


def _proj_call(x2, batch, norm_g, w_in, b_gates, ln_g, ln_b, w_spatial, b_spatial):
    t, d = x2.shape
    rows = PROJ_ROWS
    seq = t // batch
    assert seq % rows == 0 and rows % MOBA_BLOCK == 0 and rows % GMLP_CHUNK == 0
    n_tiles = t // rows
    tiles_per_seq = seq // rows
    blocks_per_tile = rows // MOBA_BLOCK
    nb = seq // MOBA_BLOCK
    row_spec = pl.BlockSpec((rows, d), lambda i: (i, 0))
    const2 = lambda i: (0, 0)
    const3 = lambda i: (0, 0, 0)
    act = jax.ShapeDtypeStruct((t, d), BF16)
    qT_spec = pl.BlockSpec((None, ATT_HEADS, HEAD_DIM, rows),
                           lambda i: (i // tiles_per_seq, 0, 0, i % tiles_per_seq))
    vT_spec = pl.BlockSpec((None, ATT_HEADS, blocks_per_tile, V_ROWS, MOBA_BLOCK),
                           lambda i: (i // tiles_per_seq, 0, i % tiles_per_seq, 0, 0))
    return pl.pallas_call(
        _proj_kernel,
        grid=(n_tiles,),
        in_specs=[
            row_spec,
            pl.BlockSpec((1, d), const2),
            pl.BlockSpec((d, N_SEGMENTS * d), const2, pipeline_mode=pl.Buffered(1)),
            pl.BlockSpec((1, 2 * d), const2),
            pl.BlockSpec((1, d), const2),
            pl.BlockSpec((1, d), const2),
            pl.BlockSpec((GMLP_GROUPS, GMLP_CHUNK, GMLP_CHUNK), const3),
            pl.BlockSpec((GMLP_GROUPS, GMLP_CHUNK, 1), const3),
        ],
        out_specs=[row_spec, qT_spec, row_spec, vT_spec, row_spec,
                   pl.BlockSpec((1, blocks_per_tile, d), lambda i: (i, 0, 0))],
        out_shape=[act,
                   jax.ShapeDtypeStruct((batch, ATT_HEADS, HEAD_DIM, seq), BF16),
                   act,
                   jax.ShapeDtypeStruct((batch, ATT_HEADS, nb, V_ROWS, MOBA_BLOCK), BF16),
                   act,
                   jax.ShapeDtypeStruct((n_tiles, blocks_per_tile, d), F32)],
        scratch_shapes=[pltpu.VMEM((rows, d), BF16), pltpu.VMEM((rows, d), BF16),
                        pltpu.VMEM((rows, d), F32)],
        compiler_params=pltpu.CompilerParams(
            dimension_semantics=("arbitrary",), vmem_limit_bytes=VMEM_LIMIT_BYTES),
        name="proj_gmlp",
    )(x2, norm_g, w_in, b_gates, ln_g, ln_b, w_spatial, b_spatial)


def _t5_bucket_np(n):
    n = np.maximum(n, 0)
    max_exact = REL_BUCKETS // 2
    nf = np.maximum(n, max_exact).astype(np.float32)
    large = max_exact + (np.log(nf / max_exact) / math.log(REL_MAX_DIST / max_exact)
                         * (REL_BUCKETS - max_exact)).astype(np.int32)
    large = np.minimum(large, REL_BUCKETS - 1)
    return np.where(n < max_exact, n, large).astype(np.int32)


def _bucket_tiles():
    kpos = np.arange(MOBA_BLOCK, dtype=np.int32)[:, None]
    qpos = np.arange(MOBA_BLOCK, dtype=np.int32)[None, :]
    rel = qpos - kpos
    return np.stack([_t5_bucket_np(rel), _t5_bucket_np(rel + MOBA_BLOCK)]), rel


def _bias_kernel(relb_ref, bucket_ref, out_ref):
    h = pl.program_id(0)
    far = relb_ref[REL_BUCKETS - 1, h]
    k_idx = lax.broadcasted_iota(jnp.int32, (MOBA_BLOCK, MOBA_BLOCK), 0)
    q_idx = lax.broadcasted_iota(jnp.int32, (MOBA_BLOCK, MOBA_BLOCK), 1)
    for tile in range(2):
        bucket = bucket_ref[tile]
        bias = jnp.zeros((MOBA_BLOCK, MOBA_BLOCK), F32)
        for b in range(REL_BUCKETS):
            bias = jnp.where(bucket == b, relb_ref[b, h], bias)
        bias2 = (bias - far) * LOG2E
        if tile == BIAS_OWN:
            bias2 = jnp.where(q_idx >= k_idx, bias2, MASK_NEG)
        out_ref[0, tile] = bias2


def _bias_call(rel_bias):
    buckets, _ = _bucket_tiles()
    return pl.pallas_call(
        _bias_kernel,
        grid=(ATT_HEADS,),
        in_specs=[pl.BlockSpec(memory_space=pltpu.SMEM),
                  pl.BlockSpec((2, MOBA_BLOCK, MOBA_BLOCK), lambda h: (0, 0, 0))],
        out_specs=pl.BlockSpec((1, N_BIAS_TILES, MOBA_BLOCK, MOBA_BLOCK), lambda h: (h, 0, 0, 0)),
        out_shape=jax.ShapeDtypeStruct((ATT_HEADS, N_BIAS_TILES, MOBA_BLOCK, MOBA_BLOCK), F32),
        compiler_params=pltpu.CompilerParams(dimension_semantics=("arbitrary",)),
        name="t5_bias_tiles",
    )(rel_bias, jnp.asarray(buckets))


BIAS_OWN, BIAS_PREV = range(2)
N_BIAS_TILES = 2


def _attn_kernel(qT_ref, k_ref, vT_ref, kmh_ref, kml_ref, oh_ref, bias_ref, o_ref,
                 qp_scr, m_scr, acc_scr):
    i = pl.program_id(1)
    nh, nb = kmh_ref.shape[0], kmh_ref.shape[1]
    blk = MOBA_BLOCK
    hd = HEAD_DIM

    n_idx = lax.broadcasted_iota(jnp.int32, (nb, blk), 0)
    past = n_idx < i
    for h in range(nh):
        qT = qT_ref[h]
        gate = _dot(kmh_ref[h], qT) + _dot(kml_ref[h], qT)
        gate = jnp.where(past, gate, -jnp.inf)
        rank = jnp.zeros((nb, blk), F32)
        for m in range(nb):
            row = gate[m:m + 1, :]
            beats = (row > gate) | ((row == gate) & (m < n_idx))
            rank = rank + jnp.where(beats, 1.0, 0.0)
        keep = (past & (rank < MOBA_TOPK)) | (n_idx == i)
        sel = jnp.where(keep, 0.0, MASK_NEG)
        sel = jnp.concatenate([sel, jnp.zeros((hd - nb, blk), F32)], axis=0)
        qp_scr[h] = jnp.concatenate([qT, sel.astype(BF16)], axis=0)

    def scores(h, j):
        r0 = pl.multiple_of(j * blk, blk)
        kp = jnp.concatenate([k_ref[pl.ds(r0, blk), h * hd:(h + 1) * hd], oh_ref[j]], axis=1)
        return _dot(kp, qp_scr[h])

    def fold(blocks, bias_tiles, first):
        ss = []
        for h in range(nh):
            parts = []
            for j, tile in zip(blocks, bias_tiles):
                s = scores(h, j)
                parts.append(s if tile is None else s + bias_ref[h, tile])
            ss.append(parts)
        ps, alphas = [], []
        for h in range(nh):
            m_new = functools.reduce(
                jnp.maximum, [jnp.max(s, axis=0, keepdims=True) for s in ss[h]])
            if not first:
                m_old = m_scr[h]
                m_new = jnp.maximum(m_old, m_new)
                alphas.append(jnp.exp2(m_old - m_new))
            m_scr[h] = m_new
            ps.append([jnp.exp2(s - m_new).astype(BF16) for s in ss[h]])
        for h in range(nh):
            pv = functools.reduce(
                lambda a, b: a + b, [_dot(vT_ref[h, j], x) for j, x in zip(blocks, ps[h])])
            acc_scr[h] = pv if first else alphas[h] * acc_scr[h] + pv

    @pl.when(i == 0)
    def _():
        fold([i], [BIAS_OWN], first=True)

    @pl.when(i >= 1)
    def _():
        fold([i, i - 1], [BIAS_OWN, BIAS_PREV], first=True)

    n_far = i - 1

    def far_pair(p, carry):
        fold([2 * p, 2 * p + 1], [None, None], first=False)
        return carry

    lax.fori_loop(0, n_far // 2, far_pair, 0)

    @pl.when((n_far >= 1) & (n_far % 2 == 1))
    def _():
        fold([n_far - 1], [None], first=False)

    for h in range(nh):
        y = acc_scr[h, :hd, :] * (1.0 / acc_scr[h, hd:hd + 1, :])
        o_ref[:, h * hd:(h + 1) * hd] = y.T.astype(o_ref.dtype)


def _attn_call(qT, k2, vT, km_hi, km_lo, bias_tiles):
    b, h, nb, v_rows, blk = vT.shape
    hd = HEAD_DIM
    s = nb * blk
    onehot = np.zeros((nb, blk, LANES), np.float32)
    for j in range(nb):
        onehot[j, :, j] = 1.0
    once = pl.Buffered(1)
    return pl.pallas_call(
        _attn_kernel,
        grid=(b, nb),
        in_specs=[
            pl.BlockSpec((None, h, hd, blk), lambda bi, i: (bi, 0, 0, i)),
            pl.BlockSpec((s, h * hd), lambda bi, i: (bi, 0)),
            pl.BlockSpec((None, h, nb, v_rows, blk), lambda bi, i: (bi, 0, 0, 0, 0)),
            pl.BlockSpec((None, h, nb, hd), lambda bi, i: (bi, 0, 0, 0)),
            pl.BlockSpec((None, h, nb, hd), lambda bi, i: (bi, 0, 0, 0)),
            pl.BlockSpec((nb, blk, LANES), lambda bi, i: (0, 0, 0), pipeline_mode=once),
            pl.BlockSpec((h, N_BIAS_TILES, blk, blk), lambda bi, i: (0, 0, 0, 0),
                         pipeline_mode=once),
        ],
        out_specs=pl.BlockSpec((blk, h * hd), lambda bi, i: (bi * nb + i, 0)),
        out_shape=jax.ShapeDtypeStruct((b * s, h * hd), BF16),
        scratch_shapes=[pltpu.VMEM((h, 2 * hd, blk), BF16), pltpu.VMEM((h, 1, blk), F32),
                        pltpu.VMEM((h, v_rows, blk), F32)],
        compiler_params=pltpu.CompilerParams(
            dimension_semantics=("arbitrary", "arbitrary"),
            vmem_limit_bytes=VMEM_LIMIT_BYTES),
        name="moba_attention",
    )(qT, k2, vT, km_hi, km_lo, jnp.asarray(onehot, BF16), bias_tiles)


ROUTE_E1, ROUTE_E2, ROUTE_W1, ROUTE_W2, ROUTE_R1, ROUTE_R2 = range(6)


def _dot_nt(a, b):
    return lax.dot_general(a, b, (((1,), (1,)), ((), ())), preferred_element_type=F32)


def _store_token_major(ref, x):
    rows = x.shape[0]
    for s in range(TOKEN_SUBLANES):
        ref[pl.ds(s, rows, stride=TOKEN_SUBLANES), :] = x[:, s * LANES:(s + 1) * LANES]


def _load_token_major(ref, rows):
    return jnp.concatenate(
        [ref[pl.ds(s, rows, stride=TOKEN_SUBLANES), :] for s in range(TOKEN_SUBLANES)], axis=1)


def _merge_kernel(x_ref, ya_ref, gb_ref, yb_ref, wo_ref, ng_ref, wrh_ref, wrl_ref, br_ref,
                  h_ref, xn_ref, route_ref, route_t_ref, counts_ref, run_scr):
    @pl.when(pl.program_id(0) == 0)
    def _():
        run_scr[...] = jnp.zeros_like(run_scr)

    f = lambda r: r[...].astype(F32)
    mix = (f(ya_ref) + f(gb_ref) * f(yb_ref)).astype(BF16)
    h = x_ref[...] + _dot(mix, wo_ref[...])
    h_ref[...] = h
    xn = _rmsnorm(h, ng_ref[...])
    _store_token_major(xn_ref, xn)
    rows = xn.shape[0]

    x_hi = xn.astype(BF16)
    x_lo = (xn - x_hi.astype(F32)).astype(BF16)
    logits = (_dot_nt(wrh_ref[...], x_hi) + _dot_nt(wrh_ref[...], x_lo)
              + _dot_nt(wrl_ref[...], x_hi) + br_ref[...])
    unit = lax.broadcasted_iota(jnp.int32, logits.shape, 0).astype(F32)
    big = float(ROUTER_UNITS)
    neg_inf = -jnp.inf

    gl = jnp.where((unit >= GROUP_LANE0) & (unit < GROUP_LANE0 + N_GROUPS), logits, neg_inf)
    gmax = jnp.max(gl, axis=0, keepdims=True)
    g_w = 1.0 / jnp.sum(jnp.exp(gl - gmax), axis=0, keepdims=True)
    g_idx = jnp.min(jnp.where(gl == gmax, unit, big), axis=0, keepdims=True) - GROUP_LANE0

    e0 = g_idx * EXPERTS_PER_GROUP
    el = jnp.where((unit >= e0) & (unit < e0 + EXPERTS_PER_GROUP), logits, neg_inf)
    m1 = jnp.max(el, axis=0, keepdims=True)
    i1 = jnp.min(jnp.where(el == m1, unit, big), axis=0, keepdims=True)
    el2 = jnp.where(unit == i1, neg_inf, el)
    m2 = jnp.max(el2, axis=0, keepdims=True)
    i2 = jnp.min(jnp.where(el2 == m2, unit, big), axis=0, keepdims=True)
    e2 = jnp.exp(m2 - m1)
    den = 1.0 + e2
    w1 = (1.0 / den) * g_w
    w2 = (e2 / den) * g_w

    hit1 = unit == i1
    hit2 = unit == i2
    onehot = jnp.where(hit1, 1.0, jnp.where(hit2, 1.0, 0.0))
    c_idx = lax.broadcasted_iota(jnp.int32, (rows, rows), 0)
    r_idx = lax.broadcasted_iota(jnp.int32, (rows, rows), 1)
    earlier = jnp.where(c_idx < r_idx, 1.0, 0.0).astype(BF16)
    prefix = run_scr[...] + _dot(onehot.astype(BF16), earlier)
    rank1 = jnp.sum(jnp.where(hit1, prefix, 0.0), axis=0, keepdims=True)
    rank2 = jnp.sum(jnp.where(hit2, prefix, 0.0), axis=0, keepdims=True)
    run_scr[...] = run_scr[...] + jnp.sum(onehot, axis=1, keepdims=True)
    counts_ref[...] = run_scr[...]

    route_t = jnp.concatenate(
        [i1, i2, w1, w2, rank1, rank2, jnp.zeros((ROUTER_LANES - 6, rows), F32)], axis=0)
    route_t_ref[0] = route_t[:ROUTE_ROWS]
    route_ref[...] = route_t.T


def _merge_call(x2, ya, gb, yb, w_out, norm_g, wr_hi, wr_lo, b_router):
    t, d = x2.shape
    rows = MERGE_ROWS
    assert t % rows == 0 and d == TOKEN_SUBLANES * LANES
    n_tiles = t // rows
    row_spec = pl.BlockSpec((rows, d), lambda i: (i, 0))
    const2 = lambda i: (0, 0)
    return pl.pallas_call(
        _merge_kernel,
        grid=(n_tiles,),
        in_specs=[row_spec, row_spec, row_spec, row_spec,
                  pl.BlockSpec((d, d), const2),
                  pl.BlockSpec((1, d), const2),
                  pl.BlockSpec((ROUTER_UNITS, d), const2),
                  pl.BlockSpec((ROUTER_UNITS, d), const2),
                  pl.BlockSpec((ROUTER_UNITS, 1), const2)],
        out_specs=[row_spec,
                   pl.BlockSpec((rows * TOKEN_SUBLANES, LANES), lambda i: (i, 0)),
                   pl.BlockSpec((rows, ROUTER_LANES), lambda i: (i, 0)),
                   pl.BlockSpec((1, ROUTE_ROWS, rows), lambda i: (i, 0, 0)),
                   pl.BlockSpec((ROUTER_UNITS, 1), const2)],
        out_shape=[jax.ShapeDtypeStruct((t, d), F32),
                   jax.ShapeDtypeStruct((t * TOKEN_SUBLANES, LANES), F32),
                   jax.ShapeDtypeStruct((t, ROUTER_LANES), F32),
                   jax.ShapeDtypeStruct((n_tiles, ROUTE_ROWS, rows), F32),
                   jax.ShapeDtypeStruct((ROUTER_UNITS, 1), F32)],
        scratch_shapes=[pltpu.VMEM((ROUTER_UNITS, 1), F32)],
        compiler_params=pltpu.CompilerParams(
            dimension_semantics=("arbitrary",), vmem_limit_bytes=VMEM_LIMIT_BYTES),
        name="merge_outproj_router",
    )(x2, ya, gb, yb, w_out, norm_g, wr_hi, wr_lo, b_router)


def _token_rows(ref, token):
    return ref.at[pl.ds(pl.multiple_of(token * TOKEN_SUBLANES, TOKEN_SUBLANES), TOKEN_SUBLANES)]


def _dispatch_kernel(last_ref, nreal_ref, pos1_ref, pos2_ref, xn_ref, xs_hbm, zero_scr, sem):
    rows = pos1_ref.shape[2]
    tile = EXPERT_ROWS
    n_tiles = xs_hbm.shape[0] // (tile * TOKEN_SUBLANES)

    def zero_tile(j):
        start = pl.multiple_of(j * (tile * TOKEN_SUBLANES), tile * TOKEN_SUBLANES)
        return pltpu.make_async_copy(
            zero_scr, xs_hbm.at[pl.ds(start, tile * TOKEN_SUBLANES)], sem)

    @pl.when(pl.program_id(0) == 0)
    def _():
        zero_scr[...] = jnp.zeros_like(zero_scr)
        for e in range(N_EXPERTS):
            @pl.when(last_ref[e] >= 0)
            def _():
                zero_tile(last_ref[e]).start()

        def tail_start(j, carry):
            zero_tile(j).start()
            return carry

        lax.fori_loop(nreal_ref[0], n_tiles, tail_start, 0)

        for e in range(N_EXPERTS):
            @pl.when(last_ref[e] >= 0)
            def _():
                zero_tile(0).wait()

        def tail_wait(j, carry):
            zero_tile(0).wait()
            return carry

        lax.fori_loop(nreal_ref[0], n_tiles, tail_wait, 0)

    def issue(g, carry):
        for u in range(DMA_UNROLL):
            r = g * DMA_UNROLL + u
            src = _token_rows(xn_ref, r)
            pltpu.make_async_copy(src, _token_rows(xs_hbm, pos1_ref[0, 0, r]), sem).start(0)
            pltpu.make_async_copy(src, _token_rows(xs_hbm, pos2_ref[0, 0, r]), sem).start(1)
        return carry

    lax.fori_loop(0, rows // DMA_UNROLL, issue, 0)

    for _ in range(2):
        pltpu.make_async_copy(xn_ref, xs_hbm.at[pl.ds(0, rows * TOKEN_SUBLANES)], sem).wait()


def _dispatch_call(last_tile, n_real, pos1, pos2, xn, n_sorted_rows):
    n_steps, _, rows = pos1.shape
    smem_row = pl.BlockSpec((1, 1, rows), lambda i, lt, nr: (i, 0, 0), memory_space=pltpu.SMEM)
    return pl.pallas_call(
        _dispatch_kernel,
        grid_spec=pltpu.PrefetchScalarGridSpec(
            num_scalar_prefetch=2,
            grid=(n_steps,),
            in_specs=[smem_row, smem_row,
                      pl.BlockSpec((rows * TOKEN_SUBLANES, LANES), lambda i, lt, nr: (i, 0))],
            out_specs=pl.BlockSpec(memory_space=pl.ANY),
            scratch_shapes=[pltpu.VMEM((EXPERT_ROWS * TOKEN_SUBLANES, LANES), F32),
                            pltpu.SemaphoreType.DMA(())],
        ),
        out_shape=jax.ShapeDtypeStruct((n_sorted_rows * TOKEN_SUBLANES, LANES), F32),
        compiler_params=pltpu.CompilerParams(dimension_semantics=("arbitrary",)),
        name="moe_dispatch",
    )(last_tile, n_real, pos1, pos2, xn)


EXPERT_IN_SLOTS = 3
EXPERT_OUT_SLOTS = 2


def _expert_kernel(first_ref, end_ref, nreal_ref, xs_hbm, w1_ref, w3_ref, w2_ref, ys_hbm,
                   w1_scr, w3_scr, w2_scr, xbuf, ybuf, in_sems, out_sems):
    e = pl.program_id(0)
    tile_rows = EXPERT_ROWS * TOKEN_SUBLANES
    n_real = nreal_ref[0]
    n_tiles = xs_hbm.shape[0] // tile_rows

    def tile_of(ref, t):
        return ref.at[pl.ds(pl.multiple_of(t * tile_rows, tile_rows), tile_rows)]

    def in_copy(t):
        slot = t % EXPERT_IN_SLOTS
        return pltpu.make_async_copy(tile_of(xs_hbm, t), xbuf.at[slot], in_sems.at[slot])

    def out_copy(t):
        slot = t % EXPERT_OUT_SLOTS
        return pltpu.make_async_copy(ybuf.at[slot], tile_of(ys_hbm, t), out_sems.at[slot])

    @pl.when(e == 0)
    def _():
        for t in range(EXPERT_IN_SLOTS - 1):
            @pl.when(t < n_real)
            def _():
                in_copy(t).start()

    w1_scr[...] = w1_ref[...].astype(BF16)
    w3_scr[...] = w3_ref[...].astype(BF16)
    w2_scr[...] = w2_ref[...].astype(BF16)

    def tile(t, carry):
        ahead = t + EXPERT_IN_SLOTS - 1

        @pl.when(ahead < n_real)
        def _():
            in_copy(ahead).start()

        in_copy(t).wait()

        @pl.when(t >= EXPERT_OUT_SLOTS)
        def _():
            out_copy(t - EXPERT_OUT_SLOTS).wait()

        x = _load_token_major(xbuf.at[t % EXPERT_IN_SLOTS], EXPERT_ROWS).astype(BF16)
        a = _dot(x, w1_scr[...])
        b = _dot(x, w3_scr[...])
        hid = (a * _sigmoid(a)) * b
        _store_token_major(ybuf.at[t % EXPERT_OUT_SLOTS], _dot(hid.astype(BF16), w2_scr[...]))
        out_copy(t).start()
        return carry

    lax.fori_loop(first_ref[e], end_ref[e], tile, 0)

    @pl.when(e == pl.num_programs(0) - 1)
    def _():
        for back in range(EXPERT_OUT_SLOTS, 0, -1):
            @pl.when(n_real - back >= 0)
            def _():
                out_copy(n_real - back).wait()
        ybuf[0] = jnp.zeros(ybuf.shape[1:], F32)

        def tail_start(t, carry):
            pltpu.make_async_copy(ybuf.at[0], tile_of(ys_hbm, t), out_sems.at[0]).start()
            return carry

        def tail_wait(t, carry):
            pltpu.make_async_copy(ybuf.at[0], tile_of(ys_hbm, t), out_sems.at[0]).wait()
            return carry

        lax.fori_loop(n_real, n_tiles, tail_start, 0)
        lax.fori_loop(n_real, n_tiles, tail_wait, 0)


def _expert_call(first_tile, end_tile, n_real, xs, w1, w3, w2):
    n_experts, d, d_expert = w1.shape
    tile_rows = EXPERT_ROWS * TOKEN_SUBLANES
    per_expert = lambda e, f, n, nr: (e, 0, 0)
    return pl.pallas_call(
        _expert_kernel,
        grid_spec=pltpu.PrefetchScalarGridSpec(
            num_scalar_prefetch=3,
            grid=(n_experts,),
            in_specs=[pl.BlockSpec(memory_space=pl.ANY),
                      pl.BlockSpec((None, d, d_expert), per_expert),
                      pl.BlockSpec((None, d, d_expert), per_expert),
                      pl.BlockSpec((None, d_expert, d), per_expert)],
            out_specs=pl.BlockSpec(memory_space=pl.ANY),
            scratch_shapes=[pltpu.VMEM((d, d_expert), BF16), pltpu.VMEM((d, d_expert), BF16),
                            pltpu.VMEM((d_expert, d), BF16),
                            pltpu.VMEM((EXPERT_IN_SLOTS, tile_rows, LANES), F32),
                            pltpu.VMEM((EXPERT_OUT_SLOTS, tile_rows, LANES), F32),
                            pltpu.SemaphoreType.DMA((EXPERT_IN_SLOTS,)),
                            pltpu.SemaphoreType.DMA((EXPERT_OUT_SLOTS,))],
        ),
        out_shape=jax.ShapeDtypeStruct(xs.shape, F32),
        compiler_params=pltpu.CompilerParams(
            dimension_semantics=("arbitrary",), vmem_limit_bytes=VMEM_LIMIT_BYTES),
        name="moe_experts",
    )(first_tile, end_tile, n_real, xs, w1, w3, w2)


def _combine_kernel(p1_first, p2_first, p1_next, p2_next, ys_hbm, h_ref, route_ref, ng_ref,
                    out_ref, buf, sems):
    i = pl.program_id(0)
    n = pl.num_programs(0)
    rows = h_ref.shape[0]

    def issue(p1_ref, p2_ref, slot):
        def body(g, carry):
            for u in range(DMA_UNROLL):
                r = g * DMA_UNROLL + u
                for which, p_ref in ((0, p1_ref), (1, p2_ref)):
                    pltpu.make_async_copy(_token_rows(ys_hbm, p_ref[0, 0, r]),
                                          _token_rows(buf.at[2 * slot + which], r),
                                          sems.at[slot]).start(which)
            return carry
        lax.fori_loop(0, rows // DMA_UNROLL, body, 0)

    @pl.when(i == 0)
    def _():
        issue(p1_first, p2_first, 0)

    @pl.when(i + 1 < n)
    def _():
        issue(p1_next, p2_next, (i + 1) % 2)

    slot = i % 2
    for which in range(2):
        pltpu.make_async_copy(ys_hbm.at[pl.ds(0, rows * TOKEN_SUBLANES)],
                              buf.at[2 * slot + which], sems.at[slot]).wait()

    route = route_ref[...]
    w1 = route[:, ROUTE_W1:ROUTE_W1 + 1]
    w2 = route[:, ROUTE_W2:ROUTE_W2 + 1]
    y = (h_ref[...] + w1 * _load_token_major(buf.at[2 * slot], rows)
         + w2 * _load_token_major(buf.at[2 * slot + 1], rows))
    out_ref[...] = _rmsnorm(y, ng_ref[...])


def _combine_call(pos1, pos2, ys, h, route, norm_g):
    t, d = h.shape
    n_steps, _, rows = pos1.shape
    row_spec = pl.BlockSpec((rows, d), lambda i: (i, 0))
    first = pl.BlockSpec((1, 1, rows), lambda i: (0, 0, 0), memory_space=pltpu.SMEM)
    nxt = pl.BlockSpec((1, 1, rows), lambda i: (jnp.minimum(i + 1, n_steps - 1), 0, 0),
                       memory_space=pltpu.SMEM)
    return pl.pallas_call(
        _combine_kernel,
        grid=(n_steps,),
        in_specs=[first, first, nxt, nxt,
                  pl.BlockSpec(memory_space=pl.ANY),
                  row_spec,
                  pl.BlockSpec((rows, ROUTER_LANES), lambda i: (i, 0)),
                  pl.BlockSpec((1, d), lambda i: (0, 0))],
        out_specs=row_spec,
        out_shape=jax.ShapeDtypeStruct((t, d), F32),
        scratch_shapes=[pltpu.VMEM((4, rows * TOKEN_SUBLANES, LANES), F32),
                        pltpu.SemaphoreType.DMA((2,))],
        compiler_params=pltpu.CompilerParams(
            dimension_semantics=("arbitrary",), vmem_limit_bytes=VMEM_LIMIT_BYTES),
        name="moe_combine",
    )(pos1, pos2, pos1, pos2, ys, h, route, norm_g)


def _sparse_moe(xn, route, route_t, counts, h, w1, w3, w2, norm_g):
    t = h.shape[0]
    tile = EXPERT_ROWS
    n_tiles = (2 * t) // tile + N_EXPERTS
    expert = jnp.arange(N_EXPERTS, dtype=jnp.int32)
    counts = counts[:N_EXPERTS, 0].astype(jnp.int32)
    group_tiles = (counts + tile - 1) // tile
    end_tile = jnp.sum(jnp.where(expert[None, :] <= expert[:, None], group_tiles[None, :], 0), axis=1)
    first_tile = end_tile - group_tiles
    n_real = end_tile[-1:]
    last_tile = jnp.where(group_tiles > 0, end_tile - 1, -1)

    def positions(e_row, r_row):
        e = route_t[:, e_row, :].astype(jnp.int32)
        start = jnp.zeros_like(e)
        for k in range(N_EXPERTS):
            start = jnp.where(e == k, first_tile[k] * tile, start)
        pos = start + route_t[:, r_row, :].astype(jnp.int32)
        return pos.reshape(t // COMBINE_ROWS, 1, COMBINE_ROWS)

    pos1 = positions(ROUTE_E1, ROUTE_R1)
    pos2 = positions(ROUTE_E2, ROUTE_R2)
    xs = _dispatch_call(last_tile, n_real, pos1, pos2, xn, n_tiles * tile)
    ys = _expert_call(first_tile, end_tile, n_real, xs, w1, w3, w2)
    return _combine_call(pos1, pos2, ys, h, route, norm_g)


def _layer(h, norm_mix_g, w_in, b_gates, gmlp_ln_g, gmlp_ln_b, w_spatial, b_spatial, bias_tiles,
           w_out, norm_ffn_g, w_group_router, b_group_router, w_expert_router, b_expert_router,
           w1, w3, w2, norm_out_g):
    b, s, d = h.shape
    t = b * s
    nb = s // MOBA_BLOCK
    x2 = h.reshape(t, d)
    row = lambda v: v.reshape(1, -1)

    ya, qT, k, vT, gb, kmean = _proj_call(
        x2, b, row(norm_mix_g), w_in.astype(BF16), row(b_gates), row(gmlp_ln_g), row(gmlp_ln_b),
        w_spatial, b_spatial[:, :, None])

    km = jnp.transpose(kmean.reshape(b, nb, ATT_HEADS, HEAD_DIM), (0, 2, 1, 3))
    km_hi = km.astype(BF16)
    km_lo = (km - km_hi.astype(F32)).astype(BF16)
    yb = _attn_call(qT, k, vT, km_hi, km_lo, bias_tiles)

    w_router = jnp.concatenate(
        [jnp.transpose(w_expert_router, (0, 2, 1)).reshape(N_EXPERTS, d), w_group_router.T,
         jnp.zeros((ROUTER_UNITS - N_EXPERTS - N_GROUPS, d), F32)], axis=0)
    b_router = jnp.concatenate(
        [b_expert_router.reshape(-1), b_group_router,
         jnp.zeros((ROUTER_UNITS - N_EXPERTS - N_GROUPS,), F32)]).reshape(ROUTER_UNITS, 1)
    wr_hi = w_router.astype(BF16)
    wr_lo = (w_router - wr_hi.astype(F32)).astype(BF16)
    h2, xn, route, route_t, counts = _merge_call(
        x2, ya, gb, yb, w_out.astype(BF16), row(norm_ffn_g), wr_hi, wr_lo, b_router)

    out = _sparse_moe(xn, route, route_t, counts, h2, w1, w3, w2, row(norm_out_g))
    return out.reshape(b, s, d)


def kernel(x, norm_mix_g, w_in, b_gates, gmlp_ln_g, gmlp_ln_b, w_spatial, b_spatial, rel_bias, w_out, norm_ffn_g, w_group_router, b_group_router, w_expert_router, b_expert_router, w1, w3, w2, norm_final_g):
    depth = w_in.shape[0]
    assert depth == 1, "the final rmsnorm is fused into the last layer's combine kernel"
    bias_tiles = _bias_call(rel_bias)
    return _layer(x, norm_mix_g[0], w_in[0], b_gates[0], gmlp_ln_g[0], gmlp_ln_b[0], w_spatial[0],
                  b_spatial[0], bias_tiles, w_out[0], norm_ffn_g[0], w_group_router[0],
                  b_group_router[0], w_expert_router[0], b_expert_router[0], w1[0], w3[0], w2[0],
                  norm_final_g)
```

```python
import functools
import math

import numpy as np
import jax
import jax.numpy as jnp
from jax import lax
from jax.experimental import pallas as pl
from jax.experimental.pallas import tpu as pltpu

F32 = jnp.float32
BF16 = jnp.bfloat16

D_MODEL = 1024
NORM_EPS = 1e-6
GMLP_GROUPS = 8
GMLP_CHUNK = 128
ATT_HEADS = 8
HEAD_DIM = 128
MOBA_BLOCK = 256
MOBA_TOPK = 3
REL_BUCKETS = 32
REL_MAX_DIST = 128
N_GROUPS = 4
EXPERTS_PER_GROUP = 8
N_EXPERTS = N_GROUPS * EXPERTS_PER_GROUP
D_EXPERT = 256
N_SEGMENTS = 7

LANES = 128
TOKEN_SUBLANES = 8
ROUTE_ROWS = 8
VMEM_LIMIT_BYTES = 56 * 1024 * 1024

SQRT_HALF = math.sqrt(0.5)
LOG2E = math.log2(math.e)
SCORE_SCALE2 = (HEAD_DIM ** -0.5) * LOG2E
MASK_NEG = -(2.0 ** 100)
V_ROWS = HEAD_DIM + 16
ROUTER_LANES = 128
ROUTER_UNITS = 48
GROUP_LANE0 = N_EXPERTS

PROJ_ROWS = 512
MERGE_ROWS = 512
EXPERT_ROWS = 256
DISPATCH_ROWS = 1024
COMBINE_ROWS = 256
DMA_UNROLL = 8


def _rmsnorm(x, g):
    return x * lax.rsqrt(jnp.mean(x * x, axis=-1, keepdims=True) + NORM_EPS) * g


def _gelu(a):
    return 0.5 * a * (1.0 + lax.erf(a * SQRT_HALF))


def _sigmoid(a):
    return 1.0 / (1.0 + jnp.exp(-a))


def _dot(a, b):
    return jnp.dot(a, b, preferred_element_type=F32)


def _proj_kernel(x_ref, ng_ref, w_ref, bg_ref, lng_ref, lnb_ref, ws_ref, bs_ref,
                 ya_ref, qT_ref, k_ref, vT_ref, gb_ref, kmean_ref,
                 xn_scr, vln_scr, mix_scr):
    rows = x_ref.shape[0]
    d = D_MODEL
    xn_scr[...] = _rmsnorm(x_ref[...], ng_ref[...]).astype(BF16)

    def seg(i):
        return _dot(xn_scr[...], w_ref[:, i * d:(i + 1) * d])

    hd = HEAD_DIM

    v = _gelu(seg(1))
    mu = jnp.mean(v, axis=-1, keepdims=True)
    vc = v - mu
    var = jnp.mean(vc * vc, axis=-1, keepdims=True)
    vln_scr[...] = (vc * lax.rsqrt(var + NORM_EPS) * lng_ref[...] + lnb_ref[...]).astype(BF16)

    mix_scr[...] = _gelu(seg(0)) * _sigmoid(seg(5) + bg_ref[:, :d])

    q = seg(2) * SCORE_SCALE2
    for h in range(ATT_HEADS):
        qT_ref[h] = q[:, h * hd:(h + 1) * hd].T.astype(BF16)

    t_idx = lax.broadcasted_iota(jnp.int32, (GMLP_CHUNK, GMLP_CHUNK), 0)
    s_idx = lax.broadcasted_iota(jnp.int32, (GMLP_CHUNK, GMLP_CHUNK), 1)
    causal = t_idx >= s_idx
    gd = d // GMLP_GROUPS
    n_chunks = rows // GMLP_CHUNK
    for g in range(GMLP_GROUPS):
        ws = jnp.where(causal, ws_ref[g], 0.0).astype(BF16)
        bias = bs_ref[g]
        vg = jnp.concatenate(
            [vln_scr[c * GMLP_CHUNK:(c + 1) * GMLP_CHUNK, g * gd:(g + 1) * gd]
             for c in range(n_chunks)], axis=1)
        mixed = _dot(ws, vg)
        for c in range(n_chunks):
            blk_rows = slice(c * GMLP_CHUNK, (c + 1) * GMLP_CHUNK)
            blk_cols = slice(g * gd, (g + 1) * gd)
            ya_ref[blk_rows, blk_cols] = (
                mix_scr[blk_rows, blk_cols] * (mixed[:, c * gd:(c + 1) * gd] + bias)).astype(BF16)

    v = seg(4)
    for blk in range(rows // MOBA_BLOCK):
        r0 = blk * MOBA_BLOCK
        for h in range(ATT_HEADS):
            vT_ref[h, blk, :hd, :] = v[r0:r0 + MOBA_BLOCK, h * hd:(h + 1) * hd].T.astype(BF16)
            vT_ref[h, blk, hd:, :] = jnp.ones((V_ROWS - hd, MOBA_BLOCK), BF16)

    gb_ref[...] = _sigmoid(seg(6) + bg_ref[:, d:]).astype(BF16)

    k = seg(3)
    k_ref[...] = k.astype(BF16)
    for blk in range(rows // MOBA_BLOCK):
        r0 = blk * MOBA_BLOCK
        kmean_ref[0, blk:blk + 1, :] = jnp.mean(k[r0:r0 + MOBA_BLOCK, :], axis=0, keepdims=True)


def _proj_call(x2, batch, norm_g, w_in, b_gates, ln_g, ln_b, w_spatial, b_spatial):
    t, d = x2.shape
    rows = PROJ_ROWS
    seq = t // batch
    assert seq % rows == 0 and rows % MOBA_BLOCK == 0 and rows % GMLP_CHUNK == 0
    n_tiles = t // rows
    tiles_per_seq = seq // rows
    blocks_per_tile = rows // MOBA_BLOCK
    nb = seq // MOBA_BLOCK
    row_spec = pl.BlockSpec((rows, d), lambda i: (i, 0))
    const2 = lambda i: (0, 0)
    const3 = lambda i: (0, 0, 0)
    act = jax.ShapeDtypeStruct((t, d), BF16)
    qT_spec = pl.BlockSpec((None, ATT_HEADS, HEAD_DIM, rows),
                           lambda i: (i // tiles_per_seq, 0, 0, i % tiles_per_seq))
    vT_spec = pl.BlockSpec((None, ATT_HEADS, blocks_per_tile, V_ROWS, MOBA_BLOCK),
                           lambda i: (i // tiles_per_seq, 0, i % tiles_per_seq, 0, 0))
    return pl.pallas_call(
        _proj_kernel,
        grid=(n_tiles,),
        in_specs=[
            row_spec,
            pl.BlockSpec((1, d), const2),
            pl.BlockSpec((d, N_SEGMENTS * d), const2, pipeline_mode=pl.Buffered(1)),
            pl.BlockSpec((1, 2 * d), const2),
            pl.BlockSpec((1, d), const2),
            pl.BlockSpec((1, d), const2),
            pl.BlockSpec((GMLP_GROUPS, GMLP_CHUNK, GMLP_CHUNK), const3),
            pl.BlockSpec((GMLP_GROUPS, GMLP_CHUNK, 1), const3),
        ],
        out_specs=[row_spec, qT_spec, row_spec, vT_spec, row_spec,
                   pl.BlockSpec((1, blocks_per_tile, d), lambda i: (i, 0, 0))],
        out_shape=[act,
                   jax.ShapeDtypeStruct((batch, ATT_HEADS, HEAD_DIM, seq), BF16),
                   act,
                   jax.ShapeDtypeStruct((batch, ATT_HEADS, nb, V_ROWS, MOBA_BLOCK), BF16),
                   act,
                   jax.ShapeDtypeStruct((n_tiles, blocks_per_tile, d), F32)],
        scratch_shapes=[pltpu.VMEM((rows, d), BF16), pltpu.VMEM((rows, d), BF16),
                        pltpu.VMEM((rows, d), F32)],
        compiler_params=pltpu.CompilerParams(
            dimension_semantics=("arbitrary",), vmem_limit_bytes=VMEM_LIMIT_BYTES),
        name="proj_gmlp",
    )(x2, norm_g, w_in, b_gates, ln_g, ln_b, w_spatial, b_spatial)


def _t5_bucket_np(n):
    n = np.maximum(n, 0)
    max_exact = REL_BUCKETS // 2
    nf = np.maximum(n, max_exact).astype(np.float32)
    large = max_exact + (np.log(nf / max_exact) / math.log(REL_MAX_DIST / max_exact)
                         * (REL_BUCKETS - max_exact)).astype(np.int32)
    large = np.minimum(large, REL_BUCKETS - 1)
    return np.where(n < max_exact, n, large).astype(np.int32)


def _bucket_tiles():
    kpos = np.arange(MOBA_BLOCK, dtype=np.int32)[:, None]
    qpos = np.arange(MOBA_BLOCK, dtype=np.int32)[None, :]
    rel = qpos - kpos
    return np.stack([_t5_bucket_np(rel), _t5_bucket_np(rel + MOBA_BLOCK)]), rel


def _bias_kernel(relb_ref, bucket_ref, out_ref):
    h = pl.program_id(0)
    far = relb_ref[REL_BUCKETS - 1, h]
    k_idx = lax.broadcasted_iota(jnp.int32, (MOBA_BLOCK, MOBA_BLOCK), 0)
    q_idx = lax.broadcasted_iota(jnp.int32, (MOBA_BLOCK, MOBA_BLOCK), 1)
    for tile in range(2):
        bucket = bucket_ref[tile]
        bias = jnp.zeros((MOBA_BLOCK, MOBA_BLOCK), F32)
        for b in range(REL_BUCKETS):
            bias = jnp.where(bucket == b, relb_ref[b, h], bias)
        bias2 = (bias - far) * LOG2E
        if tile == BIAS_OWN:
            bias2 = jnp.where(q_idx >= k_idx, bias2, MASK_NEG)
        out_ref[0, tile] = bias2


def _bias_call(rel_bias):
    buckets, _ = _bucket_tiles()
    return pl.pallas_call(
        _bias_kernel,
        grid=(ATT_HEADS,),
        in_specs=[pl.BlockSpec(memory_space=pltpu.SMEM),
                  pl.BlockSpec((2, MOBA_BLOCK, MOBA_BLOCK), lambda h: (0, 0, 0))],
        out_specs=pl.BlockSpec((1, N_BIAS_TILES, MOBA_BLOCK, MOBA_BLOCK), lambda h: (h, 0, 0, 0)),
        out_shape=jax.ShapeDtypeStruct((ATT_HEADS, N_BIAS_TILES, MOBA_BLOCK, MOBA_BLOCK), F32),
        compiler_params=pltpu.CompilerParams(dimension_semantics=("arbitrary",)),
        name="t5_bias_tiles",
    )(rel_bias, jnp.asarray(buckets))


BIAS_OWN, BIAS_PREV = range(2)
N_BIAS_TILES = 2


def _attn_kernel(qT_ref, k_ref, vT_ref, kmh_ref, kml_ref, oh_ref, bias_ref, o_ref,
                 qp_scr, m_scr, acc_scr):
    i = pl.program_id(1)
    nh, nb = kmh_ref.shape[0], kmh_ref.shape[1]
    blk = MOBA_BLOCK
    hd = HEAD_DIM

    n_idx = lax.broadcasted_iota(jnp.int32, (nb, blk), 0)
    past = n_idx < i
    for h in range(nh):
        qT = qT_ref[h]
        gate = _dot(kmh_ref[h], qT) + _dot(kml_ref[h], qT)
        gate = jnp.where(past, gate, -jnp.inf)
        rank = jnp.zeros((nb, blk), F32)
        for m in range(nb):
            row = gate[m:m + 1, :]
            beats = (row > gate) | ((row == gate) & (m < n_idx))
            rank = rank + jnp.where(beats, 1.0, 0.0)
        keep = (past & (rank < MOBA_TOPK)) | (n_idx == i)
        sel = jnp.where(keep, 0.0, MASK_NEG)
        sel = jnp.concatenate([sel, jnp.zeros((hd - nb, blk), F32)], axis=0)
        qp_scr[h] = jnp.concatenate([qT, sel.astype(BF16)], axis=0)

    def scores(h, j):
        r0 = pl.multiple_of(j * blk, blk)
        kp = jnp.concatenate([k_ref[pl.ds(r0, blk), h * hd:(h + 1) * hd], oh_ref[j]], axis=1)
        return _dot(kp, qp_scr[h])

    def fold(blocks, bias_tiles, first):
        ss = []
        for h in range(nh):
            parts = []
            for j, tile in zip(blocks, bias_tiles):
                s = scores(h, j)
                parts.append(s if tile is None else s + bias_ref[h, tile])
            ss.append(parts)
        ps, alphas = [], []
        for h in range(nh):
            m_new = functools.reduce(
                jnp.maximum, [jnp.max(s, axis=0, keepdims=True) for s in ss[h]])
            if not first:
                m_old = m_scr[h]
                m_new = jnp.maximum(m_old, m_new)
                alphas.append(jnp.exp2(m_old - m_new))
            m_scr[h] = m_new
            ps.append([jnp.exp2(s - m_new).astype(BF16) for s in ss[h]])
        for h in range(nh):
            pv = functools.reduce(
                lambda a, b: a + b, [_dot(vT_ref[h, j], x) for j, x in zip(blocks, ps[h])])
            acc_scr[h] = pv if first else alphas[h] * acc_scr[h] + pv

    @pl.when(i == 0)
    def _():
        fold([i], [BIAS_OWN], first=True)

    @pl.when(i >= 1)
    def _():
        fold([i, i - 1], [BIAS_OWN, BIAS_PREV], first=True)

    n_far = i - 1

    def far_pair(p, carry):
        fold([2 * p, 2 * p + 1], [None, None], first=False)
        return carry

    lax.fori_loop(0, n_far // 2, far_pair, 0)

    @pl.when((n_far >= 1) & (n_far % 2 == 1))
    def _():
        fold([n_far - 1], [None], first=False)

    for h in range(nh):
        y = acc_scr[h, :hd, :] * (1.0 / acc_scr[h, hd:hd + 1, :])
        o_ref[:, h * hd:(h + 1) * hd] = y.T.astype(o_ref.dtype)


def _attn_call(qT, k2, vT, km_hi, km_lo, bias_tiles):
    b, h, nb, v_rows, blk = vT.shape
    hd = HEAD_DIM
    s = nb * blk
    onehot = np.zeros((nb, blk, LANES), np.float32)
    for j in range(nb):
        onehot[j, :, j] = 1.0
    once = pl.Buffered(1)
    return pl.pallas_call(
        _attn_kernel,
        grid=(b, nb),
        in_specs=[
            pl.BlockSpec((None, h, hd, blk), lambda bi, i: (bi, 0, 0, i)),
            pl.BlockSpec((s, h * hd), lambda bi, i: (bi, 0)),
            pl.BlockSpec((None, h, nb, v_rows, blk), lambda bi, i: (bi, 0, 0, 0, 0)),
            pl.BlockSpec((None, h, nb, hd), lambda bi, i: (bi, 0, 0, 0)),
            pl.BlockSpec((None, h, nb, hd), lambda bi, i: (bi, 0, 0, 0)),
            pl.BlockSpec((nb, blk, LANES), lambda bi, i: (0, 0, 0), pipeline_mode=once),
            pl.BlockSpec((h, N_BIAS_TILES, blk, blk), lambda bi, i: (0, 0, 0, 0),
                         pipeline_mode=once),
        ],
        out_specs=pl.BlockSpec((blk, h * hd), lambda bi, i: (bi * nb + i, 0)),
        out_shape=jax.ShapeDtypeStruct((b * s, h * hd), BF16),
        scratch_shapes=[pltpu.VMEM((h, 2 * hd, blk), BF16), pltpu.VMEM((h, 1, blk), F32),
                        pltpu.VMEM((h, v_rows, blk), F32)],
        compiler_params=pltpu.CompilerParams(
            dimension_semantics=("arbitrary", "arbitrary"),
            vmem_limit_bytes=VMEM_LIMIT_BYTES),
        name="moba_attention",
    )(qT, k2, vT, km_hi, km_lo, jnp.asarray(onehot, BF16), bias_tiles)


ROUTE_E1, ROUTE_E2, ROUTE_W1, ROUTE_W2, ROUTE_R1, ROUTE_R2 = range(6)


def _dot_nt(a, b):
    return lax.dot_general(a, b, (((1,), (1,)), ((), ())), preferred_element_type=F32)


def _store_token_major(ref, x):
    rows = x.shape[0]
    for s in range(TOKEN_SUBLANES):
        ref[pl.ds(s, rows, stride=TOKEN_SUBLANES), :] = x[:, s * LANES:(s + 1) * LANES]


def _load_token_major(ref, rows):
    return jnp.concatenate(
        [ref[pl.ds(s, rows, stride=TOKEN_SUBLANES), :] for s in range(TOKEN_SUBLANES)], axis=1)


def _merge_kernel(x_ref, ya_ref, gb_ref, yb_ref, wo_ref, ng_ref, wrh_ref, wrl_ref, br_ref,
                  h_ref, xn_ref, route_ref, route_t_ref, counts_ref, run_scr):
    @pl.when(pl.program_id(0) == 0)
    def _():
        run_scr[...] = jnp.zeros_like(run_scr)

    f = lambda r: r[...].astype(F32)
    mix = (f(ya_ref) + f(gb_ref) * f(yb_ref)).astype(BF16)
    h = x_ref[...] + _dot(mix, wo_ref[...])
    h_ref[...] = h
    xn = _rmsnorm(h, ng_ref[...])
    _store_token_major(xn_ref, xn)
    rows = xn.shape[0]

    x_hi = xn.astype(BF16)
    x_lo = (xn - x_hi.astype(F32)).astype(BF16)
    logits = (_dot_nt(wrh_ref[...], x_hi) + _dot_nt(wrh_ref[...], x_lo)
              + _dot_nt(wrl_ref[...], x_hi) + br_ref[...])
    unit = lax.broadcasted_iota(jnp.int32, logits.shape, 0).astype(F32)
    big = float(ROUTER_UNITS)
    neg_inf = -jnp.inf

    gl = jnp.where((unit >= GROUP_LANE0) & (unit < GROUP_LANE0 + N_GROUPS), logits, neg_inf)
    gmax = jnp.max(gl, axis=0, keepdims=True)
    g_w = 1.0 / jnp.sum(jnp.exp(gl - gmax), axis=0, keepdims=True)
    g_idx = jnp.min(jnp.where(gl == gmax, unit, big), axis=0, keepdims=True) - GROUP_LANE0

    e0 = g_idx * EXPERTS_PER_GROUP
    el = jnp.where((unit >= e0) & (unit < e0 + EXPERTS_PER_GROUP), logits, neg_inf)
    m1 = jnp.max(el, axis=0, keepdims=True)
    i1 = jnp.min(jnp.where(el == m1, unit, big), axis=0, keepdims=True)
    el2 = jnp.where(unit == i1, neg_inf, el)
    m2 = jnp.max(el2, axis=0, keepdims=True)
    i2 = jnp.min(jnp.where(el2 == m2, unit, big), axis=0, keepdims=True)
    e2 = jnp.exp(m2 - m1)
    den = 1.0 + e2
    w1 = (1.0 / den) * g_w
    w2 = (e2 / den) * g_w

    hit1 = unit == i1
    hit2 = unit == i2
    onehot = jnp.where(hit1, 1.0, jnp.where(hit2, 1.0, 0.0))
    c_idx = lax.broadcasted_iota(jnp.int32, (rows, rows), 0)
    r_idx = lax.broadcasted_iota(jnp.int32, (rows, rows), 1)
    earlier = jnp.where(c_idx < r_idx, 1.0, 0.0).astype(BF16)
    prefix = run_scr[...] + _dot(onehot.astype(BF16), earlier)
    rank1 = jnp.sum(jnp.where(hit1, prefix, 0.0), axis=0, keepdims=True)
    rank2 = jnp.sum(jnp.where(hit2, prefix, 0.0), axis=0, keepdims=True)
    run_scr[...] = run_scr[...] + jnp.sum(onehot, axis=1, keepdims=True)
    counts_ref[...] = run_scr[...]

    route_t = jnp.concatenate(
        [i1, i2, w1, w2, rank1, rank2, jnp.zeros((ROUTER_LANES - 6, rows), F32)], axis=0)
    route_t_ref[0] = route_t[:ROUTE_ROWS]
    route_ref[...] = route_t.T


def _merge_call(x2, ya, gb, yb, w_out, norm_g, wr_hi, wr_lo, b_router):
    t, d = x2.shape
    rows = MERGE_ROWS
    assert t % rows == 0 and d == TOKEN_SUBLANES * LANES
    n_tiles = t // rows
    row_spec = pl.BlockSpec((rows, d), lambda i: (i, 0))
    const2 = lambda i: (0, 0)
    return pl.pallas_call(
        _merge_kernel,
        grid=(n_tiles,),
        in_specs=[row_spec, row_spec, row_spec, row_spec,
                  pl.BlockSpec((d, d), const2),
                  pl.BlockSpec((1, d), const2),
                  pl.BlockSpec((ROUTER_UNITS, d), const2),
                  pl.BlockSpec((ROUTER_UNITS, d), const2),
                  pl.BlockSpec((ROUTER_UNITS, 1), const2)],
        out_specs=[row_spec,
                   pl.BlockSpec((rows * TOKEN_SUBLANES, LANES), lambda i: (i, 0)),
                   pl.BlockSpec((rows, ROUTER_LANES), lambda i: (i, 0)),
                   pl.BlockSpec((1, ROUTE_ROWS, rows), lambda i: (i, 0, 0)),
                   pl.BlockSpec((ROUTER_UNITS, 1), const2)],
        out_shape=[jax.ShapeDtypeStruct((t, d), F32),
                   jax.ShapeDtypeStruct((t * TOKEN_SUBLANES, LANES), F32),
                   jax.ShapeDtypeStruct((t, ROUTER_LANES), F32),
                   jax.ShapeDtypeStruct((n_tiles, ROUTE_ROWS, rows), F32),
                   jax.ShapeDtypeStruct((ROUTER_UNITS, 1), F32)],
        scratch_shapes=[pltpu.VMEM((ROUTER_UNITS, 1), F32)],
        compiler_params=pltpu.CompilerParams(
            dimension_semantics=("arbitrary",), vmem_limit_bytes=VMEM_LIMIT_BYTES),
        name="merge_outproj_router",
    )(x2, ya, gb, yb, w_out, norm_g, wr_hi, wr_lo, b_router)


def _token_rows(ref, token):
    return ref.at[pl.ds(pl.multiple_of(token * TOKEN_SUBLANES, TOKEN_SUBLANES), TOKEN_SUBLANES)]


def _dispatch_kernel(last_ref, nreal_ref, pos1_ref, pos2_ref, xn_ref, xs_hbm, zero_scr, sem):
    rows = pos1_ref.shape[2]
    tile = EXPERT_ROWS
    n_tiles = xs_hbm.shape[0] // (tile * TOKEN_SUBLANES)

    def zero_tile(j):
        start = pl.multiple_of(j * (tile * TOKEN_SUBLANES), tile * TOKEN_SUBLANES)
        return pltpu.make_async_copy(
            zero_scr, xs_hbm.at[pl.ds(start, tile * TOKEN_SUBLANES)], sem)

    @pl.when(pl.program_id(0) == 0)
    def _():
        zero_scr[...] = jnp.zeros_like(zero_scr)
        for e in range(N_EXPERTS):
            @pl.when(last_ref[e] >= 0)
            def _():
                zero_tile(last_ref[e]).start()

        def tail_start(j, carry):
            zero_tile(j).start()
            return carry

        lax.fori_loop(nreal_ref[0], n_tiles, tail_start, 0)

        for e in range(N_EXPERTS):
            @pl.when(last_ref[e] >= 0)
            def _():
                zero_tile(0).wait()

        def tail_wait(j, carry):
            zero_tile(0).wait()
            return carry

        lax.fori_loop(nreal_ref[0], n_tiles, tail_wait, 0)

    def issue(g, carry):
        for u in range(DMA_UNROLL):
            r = g * DMA_UNROLL + u
            src = _token_rows(xn_ref, r)
            pltpu.make_async_copy(src, _token_rows(xs_hbm, pos1_ref[0, 0, r]), sem).start(0)
            pltpu.make_async_copy(src, _token_rows(xs_hbm, pos2_ref[0, 0, r]), sem).start(1)
        return carry

    lax.fori_loop(0, rows // DMA_UNROLL, issue, 0)

    for _ in range(2):
        pltpu.make_async_copy(xn_ref, xs_hbm.at[pl.ds(0, rows * TOKEN_SUBLANES)], sem).wait()


def _dispatch_call(last_tile, n_real, pos1, pos2, xn, n_sorted_rows):
    n_steps, _, rows = pos1.shape
    smem_row = pl.BlockSpec((1, 1, rows), lambda i, lt, nr: (i, 0, 0), memory_space=pltpu.SMEM)
    return pl.pallas_call(
        _dispatch_kernel,
        grid_spec=pltpu.PrefetchScalarGridSpec(
            num_scalar_prefetch=2,
            grid=(n_steps,),
            in_specs=[smem_row, smem_row,
                      pl.BlockSpec((rows * TOKEN_SUBLANES, LANES), lambda i, lt, nr: (i, 0))],
            out_specs=pl.BlockSpec(memory_space=pl.ANY),
            scratch_shapes=[pltpu.VMEM((EXPERT_ROWS * TOKEN_SUBLANES, LANES), F32),
                            pltpu.SemaphoreType.DMA(())],
        ),
        out_shape=jax.ShapeDtypeStruct((n_sorted_rows * TOKEN_SUBLANES, LANES), F32),
        compiler_params=pltpu.CompilerParams(dimension_semantics=("arbitrary",)),
        name="moe_dispatch",
    )(last_tile, n_real, pos1, pos2, xn)


EXPERT_IN_SLOTS = 3
EXPERT_OUT_SLOTS = 2


def _expert_kernel(first_ref, end_ref, nreal_ref, xs_hbm, w1_ref, w3_ref, w2_ref, ys_hbm,
                   w1_scr, w3_scr, w2_scr, xbuf, ybuf, in_sems, out_sems):
    e = pl.program_id(0)
    tile_rows = EXPERT_ROWS * TOKEN_SUBLANES
    n_real = nreal_ref[0]
    n_tiles = xs_hbm.shape[0] // tile_rows

    def tile_of(ref, t):
        return ref.at[pl.ds(pl.multiple_of(t * tile_rows, tile_rows), tile_rows)]

    def in_copy(t):
        slot = t % EXPERT_IN_SLOTS
        return pltpu.make_async_copy(tile_of(xs_hbm, t), xbuf.at[slot], in_sems.at[slot])

    def out_copy(t):
        slot = t % EXPERT_OUT_SLOTS
        return pltpu.make_async_copy(ybuf.at[slot], tile_of(ys_hbm, t), out_sems.at[slot])

    @pl.when(e == 0)
    def _():
        for t in range(EXPERT_IN_SLOTS - 1):
            @pl.when(t < n_real)
            def _():
                in_copy(t).start()

    w1_scr[...] = w1_ref[...].astype(BF16)
    w3_scr[...] = w3_ref[...].astype(BF16)
    w2_scr[...] = w2_ref[...].astype(BF16)

    def tile(t, carry):
        ahead = t + EXPERT_IN_SLOTS - 1

        @pl.when(ahead < n_real)
        def _():
            in_copy(ahead).start()

        in_copy(t).wait()

        @pl.when(t >= EXPERT_OUT_SLOTS)
        def _():
            out_copy(t - EXPERT_OUT_SLOTS).wait()

        x = _load_token_major(xbuf.at[t % EXPERT_IN_SLOTS], EXPERT_ROWS).astype(BF16)
        a = _dot(x, w1_scr[...])
        b = _dot(x, w3_scr[...])
        hid = (a * _sigmoid(a)) * b
        _store_token_major(ybuf.at[t % EXPERT_OUT_SLOTS], _dot(hid.astype(BF16), w2_scr[...]))
        out_copy(t).start()
        return carry

    lax.fori_loop(first_ref[e], end_ref[e], tile, 0)

    @pl.when(e == pl.num_programs(0) - 1)
    def _():
        for back in range(EXPERT_OUT_SLOTS, 0, -1):
            @pl.when(n_real - back >= 0)
            def _():
                out_copy(n_real - back).wait()
        ybuf[0] = jnp.zeros(ybuf.shape[1:], F32)

        def tail_start(t, carry):
            pltpu.make_async_copy(ybuf.at[0], tile_of(ys_hbm, t), out_sems.at[0]).start()
            return carry

        def tail_wait(t, carry):
            pltpu.make_async_copy(ybuf.at[0], tile_of(ys_hbm, t), out_sems.at[0]).wait()
            return carry

        lax.fori_loop(n_real, n_tiles, tail_start, 0)
        lax.fori_loop(n_real, n_tiles, tail_wait, 0)


def _expert_call(first_tile, end_tile, n_real, xs, w1, w3, w2):
    n_experts, d, d_expert = w1.shape
    tile_rows = EXPERT_ROWS * TOKEN_SUBLANES
    per_expert = lambda e, f, n, nr: (e, 0, 0)
    return pl.pallas_call(
        _expert_kernel,
        grid_spec=pltpu.PrefetchScalarGridSpec(
            num_scalar_prefetch=3,
            grid=(n_experts,),
            in_specs=[pl.BlockSpec(memory_space=pl.ANY),
                      pl.BlockSpec((None, d, d_expert), per_expert),
                      pl.BlockSpec((None, d, d_expert), per_expert),
                      pl.BlockSpec((None, d_expert, d), per_expert)],
            out_specs=pl.BlockSpec(memory_space=pl.ANY),
            scratch_shapes=[pltpu.VMEM((d, d_expert), BF16), pltpu.VMEM((d, d_expert), BF16),
                            pltpu.VMEM((d_expert, d), BF16),
                            pltpu.VMEM((EXPERT_IN_SLOTS, tile_rows, LANES), F32),
                            pltpu.VMEM((EXPERT_OUT_SLOTS, tile_rows, LANES), F32),
                            pltpu.SemaphoreType.DMA((EXPERT_IN_SLOTS,)),
                            pltpu.SemaphoreType.DMA((EXPERT_OUT_SLOTS,))],
        ),
        out_shape=jax.ShapeDtypeStruct(xs.shape, F32),
        compiler_params=pltpu.CompilerParams(
            dimension_semantics=("arbitrary",), vmem_limit_bytes=VMEM_LIMIT_BYTES),
        name="moe_experts",
    )(first_tile, end_tile, n_real, xs, w1, w3, w2)


def _combine_kernel(p1_first, p2_first, p1_next, p2_next, ys_hbm, h_ref, route_ref, ng_ref,
                    out_ref, buf, sems):
    i = pl.program_id(0)
    n = pl.num_programs(0)
    rows = h_ref.shape[0]

    def issue(p1_ref, p2_ref, slot):
        def body(g, carry):
            for u in range(DMA_UNROLL):
                r = g * DMA_UNROLL + u
                for which, p_ref in ((0, p1_ref), (1, p2_ref)):
                    pltpu.make_async_copy(_token_rows(ys_hbm, p_ref[0, 0, r]),
                                          _token_rows(buf.at[2 * slot + which], r),
                                          sems.at[slot]).start(which)
            return carry
        lax.fori_loop(0, rows // DMA_UNROLL, body, 0)

    @pl.when(i == 0)
    def _():
        issue(p1_first, p2_first, 0)

    @pl.when(i + 1 < n)
    def _():
        issue(p1_next, p2_next, (i + 1) % 2)

    slot = i % 2
    for which in range(2):
        pltpu.make_async_copy(ys_hbm.at[pl.ds(0, rows * TOKEN_SUBLANES)],
                              buf.at[2 * slot + which], sems.at[slot]).wait()

    route = route_ref[...]
    w1 = route[:, ROUTE_W1:ROUTE_W1 + 1]
    w2 = route[:, ROUTE_W2:ROUTE_W2 + 1]
    y = (h_ref[...] + w1 * _load_token_major(buf.at[2 * slot], rows)
         + w2 * _load_token_major(buf.at[2 * slot + 1], rows))
    out_ref[...] = _rmsnorm(y, ng_ref[...])


def _combine_call(pos1, pos2, ys, h, route, norm_g):
    t, d = h.shape
    n_steps, _, rows = pos1.shape
    row_spec = pl.BlockSpec((rows, d), lambda i: (i, 0))
    first = pl.BlockSpec((1, 1, rows), lambda i: (0, 0, 0), memory_space=pltpu.SMEM)
    nxt = pl.BlockSpec((1, 1, rows), lambda i: (jnp.minimum(i + 1, n_steps - 1), 0, 0),
                       memory_space=pltpu.SMEM)
    return pl.pallas_call(
        _combine_kernel,
        grid=(n_steps,),
        in_specs=[first, first, nxt, nxt,
                  pl.BlockSpec(memory_space=pl.ANY),
                  row_spec,
                  pl.BlockSpec((rows, ROUTER_LANES), lambda i: (i, 0)),
                  pl.BlockSpec((1, d), lambda i: (0, 0))],
        out_specs=row_spec,
        out_shape=jax.ShapeDtypeStruct((t, d), F32),
        scratch_shapes=[pltpu.VMEM((4, rows * TOKEN_SUBLANES, LANES), F32),
                        pltpu.SemaphoreType.DMA((2,))],
        compiler_params=pltpu.CompilerParams(
            dimension_semantics=("arbitrary",), vmem_limit_bytes=VMEM_LIMIT_BYTES),
        name="moe_combine",
    )(pos1, pos2, pos1, pos2, ys, h, route, norm_g)


def _sparse_moe(xn, route, route_t, counts, h, w1, w3, w2, norm_g):
    t = h.shape[0]
    tile = EXPERT_ROWS
    n_tiles = (2 * t) // tile + N_EXPERTS
    expert = jnp.arange(N_EXPERTS, dtype=jnp.int32)
    counts = counts[:N_EXPERTS, 0].astype(jnp.int32)
    group_tiles = (counts + tile - 1) // tile
    end_tile = jnp.sum(jnp.where(expert[None, :] <= expert[:, None], group_tiles[None, :], 0), axis=1)
    first_tile = end_tile - group_tiles
    n_real = end_tile[-1:]
    last_tile = jnp.where(group_tiles > 0, end_tile - 1, -1)

    def positions(e_row, r_row):
        e = route_t[:, e_row, :].astype(jnp.int32)
        start = jnp.zeros_like(e)
        for k in range(N_EXPERTS):
            start = jnp.where(e == k, first_tile[k] * tile, start)
        return start + route_t[:, r_row, :].astype(jnp.int32)

    pos1 = positions(ROUTE_E1, ROUTE_R1)
    pos2 = positions(ROUTE_E2, ROUTE_R2)
    per_step = lambda pos, rows: pos.reshape(t // rows, 1, rows)
    xs = _dispatch_call(last_tile, n_real, per_step(pos1, DISPATCH_ROWS),
                        per_step(pos2, DISPATCH_ROWS), xn, n_tiles * tile)
    ys = _expert_call(first_tile, end_tile, n_real, xs, w1, w3, w2)
    return _combine_call(per_step(pos1, COMBINE_ROWS), per_step(pos2, COMBINE_ROWS), ys, h,
                         route, norm_g)


def _layer(h, norm_mix_g, w_in, b_gates, gmlp_ln_g, gmlp_ln_b, w_spatial, b_spatial, bias_tiles,
           w_out, norm_ffn_g, w_group_router, b_group_router, w_expert_router, b_expert_router,
           w1, w3, w2, norm_out_g):
    b, s, d = h.shape
    t = b * s
    nb = s // MOBA_BLOCK
    x2 = h.reshape(t, d)
    row = lambda v: v.reshape(1, -1)

    ya, qT, k, vT, gb, kmean = _proj_call(
        x2, b, row(norm_mix_g), w_in.astype(BF16), row(b_gates), row(gmlp_ln_g), row(gmlp_ln_b),
        w_spatial, b_spatial[:, :, None])

    km = jnp.transpose(kmean.reshape(b, nb, ATT_HEADS, HEAD_DIM), (0, 2, 1, 3))
    km_hi = km.astype(BF16)
    km_lo = (km - km_hi.astype(F32)).astype(BF16)
    yb = _attn_call(qT, k, vT, km_hi, km_lo, bias_tiles)

    w_router = jnp.concatenate(
        [jnp.transpose(w_expert_router, (0, 2, 1)).reshape(N_EXPERTS, d), w_group_router.T,
         jnp.zeros((ROUTER_UNITS - N_EXPERTS - N_GROUPS, d), F32)], axis=0)
    b_router = jnp.concatenate(
        [b_expert_router.reshape(-1), b_group_router,
         jnp.zeros((ROUTER_UNITS - N_EXPERTS - N_GROUPS,), F32)]).reshape(ROUTER_UNITS, 1)
    wr_hi = w_router.astype(BF16)
    wr_lo = (w_router - wr_hi.astype(F32)).astype(BF16)
    h2, xn, route, route_t, counts = _merge_call(
        x2, ya, gb, yb, w_out.astype(BF16), row(norm_ffn_g), wr_hi, wr_lo, b_router)

    out = _sparse_moe(xn, route, route_t, counts, h2, w1, w3, w2, row(norm_out_g))
    return out.reshape(b, s, d)


def kernel(x, norm_mix_g, w_in, b_gates, gmlp_ln_g, gmlp_ln_b, w_spatial, b_spatial, rel_bias, w_out, norm_ffn_g, w_group_router, b_group_router, w_expert_router, b_expert_router, w1, w3, w2, norm_final_g):
    depth = w_in.shape[0]
    assert depth == 1, "the final rmsnorm is fused into the last layer's combine kernel"
    bias_tiles = _bias_call(rel_bias)
    return _layer(x, norm_mix_g[0], w_in[0], b_gates[0], gmlp_ln_g[0], gmlp_ln_b[0], w_spatial[0],
                  b_spatial[0], bias_tiles, w_out[0], norm_ffn_g[0], w_group_router[0],
                  b_group_router[0], w_expert_router[0], b_expert_router[0], w1[0], w3[0], w2[0],
                  norm_final_g)
```

```python
import functools
import math

import numpy as np
import jax
import jax.numpy as jnp
from jax import lax
from jax.experimental import pallas as pl
from jax.experimental.pallas import tpu as pltpu

F32 = jnp.float32
BF16 = jnp.bfloat16

D_MODEL = 1024
NORM_EPS = 1e-6
GMLP_GROUPS = 8
GMLP_CHUNK = 128
ATT_HEADS = 8
HEAD_DIM = 128
MOBA_BLOCK = 256
MOBA_TOPK = 3
REL_BUCKETS = 32
REL_MAX_DIST = 128
N_GROUPS = 4
EXPERTS_PER_GROUP = 8
N_EXPERTS = N_GROUPS * EXPERTS_PER_GROUP
D_EXPERT = 256
N_SEGMENTS = 7

LANES = 128
TOKEN_SUBLANES = 8
ROUTE_ROWS = 8
VMEM_LIMIT_BYTES = 56 * 1024 * 1024

SQRT_HALF = math.sqrt(0.5)
LOG2E = math.log2(math.e)
SCORE_SCALE2 = (HEAD_DIM ** -0.5) * LOG2E
MASK_NEG = -(2.0 ** 100)
BF16_SUBLANES = 16
V_ROWS = HEAD_DIM + BF16_SUBLANES
ROUTER_LANES = LANES
ROUTER_UNITS = -(-(N_EXPERTS + N_GROUPS) // BF16_SUBLANES) * BF16_SUBLANES
GROUP_UNIT0 = N_EXPERTS

PROJ_ROWS = 512
MERGE_ROWS = 512
EXPERT_ROWS = 256
DISPATCH_ROWS = 2048
COMBINE_ROWS = 512
DMA_UNROLL = 8


def _rmsnorm(x, g):
    return x * lax.rsqrt(jnp.mean(x * x, axis=-1, keepdims=True) + NORM_EPS) * g


def _gelu(a):
    return 0.5 * a * (1.0 + lax.erf(a * SQRT_HALF))


def _sigmoid(a):
    return 1.0 / (1.0 + jnp.exp(-a))


def _dot(a, b):
    return jnp.dot(a, b, preferred_element_type=F32)


def _proj_kernel(x_ref, ng_ref, w_ref, bg_ref, lng_ref, lnb_ref, ws_ref, bs_ref,
                 ya_ref, qT_ref, k_ref, vT_ref, gb_ref, kmean_ref,
                 xn_scr, vln_scr, mix_scr):
    rows = x_ref.shape[0]
    d = D_MODEL
    xn_scr[...] = _rmsnorm(x_ref[...], ng_ref[...]).astype(BF16)

    def seg(i):
        return _dot(xn_scr[...], w_ref[:, i * d:(i + 1) * d])

    hd = HEAD_DIM

    v = _gelu(seg(1))
    mu = jnp.mean(v, axis=-1, keepdims=True)
    vc = v - mu
    var = jnp.mean(vc * vc, axis=-1, keepdims=True)
    vln_scr[...] = (vc * lax.rsqrt(var + NORM_EPS) * lng_ref[...] + lnb_ref[...]).astype(BF16)

    mix_scr[...] = _gelu(seg(0)) * _sigmoid(seg(5) + bg_ref[:, :d])

    q = seg(2) * SCORE_SCALE2
    for h in range(ATT_HEADS):
        qT_ref[h] = q[:, h * hd:(h + 1) * hd].T.astype(BF16)

    t_idx = lax.broadcasted_iota(jnp.int32, (GMLP_CHUNK, GMLP_CHUNK), 0)
    s_idx = lax.broadcasted_iota(jnp.int32, (GMLP_CHUNK, GMLP_CHUNK), 1)
    causal = t_idx >= s_idx
    gd = d // GMLP_GROUPS
    n_chunks = rows // GMLP_CHUNK
    for g in range(GMLP_GROUPS):
        ws = jnp.where(causal, ws_ref[g], 0.0).astype(BF16)
        bias = bs_ref[g]
        vg = jnp.concatenate(
            [vln_scr[c * GMLP_CHUNK:(c + 1) * GMLP_CHUNK, g * gd:(g + 1) * gd]
             for c in range(n_chunks)], axis=1)
        mixed = _dot(ws, vg)
        for c in range(n_chunks):
            blk_rows = slice(c * GMLP_CHUNK, (c + 1) * GMLP_CHUNK)
            blk_cols = slice(g * gd, (g + 1) * gd)
            ya_ref[blk_rows, blk_cols] = (
                mix_scr[blk_rows, blk_cols] * (mixed[:, c * gd:(c + 1) * gd] + bias)).astype(BF16)

    v = seg(4)
    for blk in range(rows // MOBA_BLOCK):
        r0 = blk * MOBA_BLOCK
        for h in range(ATT_HEADS):
            vT_ref[h, blk, :hd, :] = v[r0:r0 + MOBA_BLOCK, h * hd:(h + 1) * hd].T.astype(BF16)
            vT_ref[h, blk, hd:, :] = jnp.ones((V_ROWS - hd, MOBA_BLOCK), BF16)

    gb_ref[...] = _sigmoid(seg(6) + bg_ref[:, d:]).astype(BF16)

    k = seg(3)
    k_ref[...] = k.astype(BF16)
    for blk in range(rows // MOBA_BLOCK):
        r0 = blk * MOBA_BLOCK
        kmean_ref[0, blk:blk + 1, :] = jnp.mean(k[r0:r0 + MOBA_BLOCK, :], axis=0, keepdims=True)


def _proj_call(x2, batch, norm_g, w_in, b_gates, ln_g, ln_b, w_spatial, b_spatial):
    t, d = x2.shape
    rows = PROJ_ROWS
    seq = t // batch
    assert seq % rows == 0 and rows % MOBA_BLOCK == 0 and rows % GMLP_CHUNK == 0
    n_tiles = t // rows
    tiles_per_seq = seq // rows
    blocks_per_tile = rows // MOBA_BLOCK
    nb = seq // MOBA_BLOCK
    row_spec = pl.BlockSpec((rows, d), lambda i: (i, 0))
    const2 = lambda i: (0, 0)
    const3 = lambda i: (0, 0, 0)
    act = jax.ShapeDtypeStruct((t, d), BF16)
    qT_spec = pl.BlockSpec((None, ATT_HEADS, HEAD_DIM, rows),
                           lambda i: (i // tiles_per_seq, 0, 0, i % tiles_per_seq))
    vT_spec = pl.BlockSpec((None, ATT_HEADS, blocks_per_tile, V_ROWS, MOBA_BLOCK),
                           lambda i: (i // tiles_per_seq, 0, i % tiles_per_seq, 0, 0))
    return pl.pallas_call(
        _proj_kernel,
        grid=(n_tiles,),
        in_specs=[
            row_spec,
            pl.BlockSpec((1, d), const2),
            pl.BlockSpec((d, N_SEGMENTS * d), const2, pipeline_mode=pl.Buffered(1)),
            pl.BlockSpec((1, 2 * d), const2),
            pl.BlockSpec((1, d), const2),
            pl.BlockSpec((1, d), const2),
            pl.BlockSpec((GMLP_GROUPS, GMLP_CHUNK, GMLP_CHUNK), const3),
            pl.BlockSpec((GMLP_GROUPS, GMLP_CHUNK, 1), const3),
        ],
        out_specs=[row_spec, qT_spec, row_spec, vT_spec, row_spec,
                   pl.BlockSpec((1, blocks_per_tile, d), lambda i: (i, 0, 0))],
        out_shape=[act,
                   jax.ShapeDtypeStruct((batch, ATT_HEADS, HEAD_DIM, seq), BF16),
                   act,
                   jax.ShapeDtypeStruct((batch, ATT_HEADS, nb, V_ROWS, MOBA_BLOCK), BF16),
                   act,
                   jax.ShapeDtypeStruct((n_tiles, blocks_per_tile, d), F32)],
        scratch_shapes=[pltpu.VMEM((rows, d), BF16), pltpu.VMEM((rows, d), BF16),
                        pltpu.VMEM((rows, d), F32)],
        compiler_params=pltpu.CompilerParams(
            dimension_semantics=("arbitrary",), vmem_limit_bytes=VMEM_LIMIT_BYTES),
        name="proj_gmlp",
    )(x2, norm_g, w_in, b_gates, ln_g, ln_b, w_spatial, b_spatial)


def _t5_bucket_np(n):
    n = np.maximum(n, 0)
    max_exact = REL_BUCKETS // 2
    nf = np.maximum(n, max_exact).astype(np.float32)
    large = max_exact + (np.log(nf / max_exact) / math.log(REL_MAX_DIST / max_exact)
                         * (REL_BUCKETS - max_exact)).astype(np.int32)
    large = np.minimum(large, REL_BUCKETS - 1)
    return np.where(n < max_exact, n, large).astype(np.int32)


def _bucket_tiles():
    kpos = np.arange(MOBA_BLOCK, dtype=np.int32)[:, None]
    qpos = np.arange(MOBA_BLOCK, dtype=np.int32)[None, :]
    rel = qpos - kpos
    return np.stack([_t5_bucket_np(rel), _t5_bucket_np(rel + MOBA_BLOCK)])


def _bias_kernel(relb_ref, bucket_ref, out_ref):
    h = pl.program_id(0)
    far = relb_ref[REL_BUCKETS - 1, h]
    k_idx = lax.broadcasted_iota(jnp.int32, (MOBA_BLOCK, MOBA_BLOCK), 0)
    q_idx = lax.broadcasted_iota(jnp.int32, (MOBA_BLOCK, MOBA_BLOCK), 1)
    for tile in range(2):
        bucket = bucket_ref[tile]
        bias = jnp.zeros((MOBA_BLOCK, MOBA_BLOCK), F32)
        for b in range(REL_BUCKETS):
            bias = jnp.where(bucket == b, relb_ref[b, h], bias)
        bias2 = (bias - far) * LOG2E
        if tile == BIAS_OWN:
            bias2 = jnp.where(q_idx >= k_idx, bias2, MASK_NEG)
        out_ref[0, tile] = bias2


def _bias_call(rel_bias):
    buckets = _bucket_tiles()
    return pl.pallas_call(
        _bias_kernel,
        grid=(ATT_HEADS,),
        in_specs=[pl.BlockSpec(memory_space=pltpu.SMEM),
                  pl.BlockSpec((2, MOBA_BLOCK, MOBA_BLOCK), lambda h: (0, 0, 0))],
        out_specs=pl.BlockSpec((1, N_BIAS_TILES, MOBA_BLOCK, MOBA_BLOCK), lambda h: (h, 0, 0, 0)),
        out_shape=jax.ShapeDtypeStruct((ATT_HEADS, N_BIAS_TILES, MOBA_BLOCK, MOBA_BLOCK), F32),
        compiler_params=pltpu.CompilerParams(dimension_semantics=("arbitrary",)),
        name="t5_bias_tiles",
    )(rel_bias, jnp.asarray(buckets))


BIAS_OWN, BIAS_PREV = range(2)
N_BIAS_TILES = 2


def _attn_kernel(qT_ref, k_ref, vT_ref, kmh_ref, kml_ref, oh_ref, bias_ref, o_ref,
                 qp_scr, m_scr, acc_scr):
    i = pl.program_id(1)
    nh, nb = kmh_ref.shape[0], kmh_ref.shape[1]
    blk = MOBA_BLOCK
    hd = HEAD_DIM

    n_idx = lax.broadcasted_iota(jnp.int32, (nb, blk), 0)
    past = n_idx < i
    for h in range(nh):
        qT = qT_ref[h]
        gate = _dot(kmh_ref[h], qT) + _dot(kml_ref[h], qT)
        gate = jnp.where(past, gate, -jnp.inf)
        rank = jnp.zeros((nb, blk), F32)
        for m in range(nb):
            row = gate[m:m + 1, :]
            beats = (row > gate) | ((row == gate) & (m < n_idx))
            rank = rank + jnp.where(beats, 1.0, 0.0)
        keep = (past & (rank < MOBA_TOPK)) | (n_idx == i)
        sel = jnp.where(keep, 0.0, MASK_NEG)
        sel = jnp.concatenate([sel, jnp.zeros((hd - nb, blk), F32)], axis=0)
        qp_scr[h] = jnp.concatenate([qT, sel.astype(BF16)], axis=0)

    def scores(h, j):
        r0 = pl.multiple_of(j * blk, blk)
        kp = jnp.concatenate([k_ref[pl.ds(r0, blk), h * hd:(h + 1) * hd], oh_ref[j]], axis=1)
        return _dot(kp, qp_scr[h])

    def fold(blocks, bias_tiles, first):
        ss = []
        for h in range(nh):
            parts = []
            for j, tile in zip(blocks, bias_tiles):
                s = scores(h, j)
                parts.append(s if tile is None else s + bias_ref[h, tile])
            ss.append(parts)
        ps, alphas = [], []
        for h in range(nh):
            m_new = functools.reduce(
                jnp.maximum, [jnp.max(s, axis=0, keepdims=True) for s in ss[h]])
            if not first:
                m_old = m_scr[h]
                m_new = jnp.maximum(m_old, m_new)
                alphas.append(jnp.exp2(m_old - m_new))
            m_scr[h] = m_new
            ps.append([jnp.exp2(s - m_new).astype(BF16) for s in ss[h]])
        for h in range(nh):
            pv = functools.reduce(
                lambda a, b: a + b, [_dot(vT_ref[h, j], x) for j, x in zip(blocks, ps[h])])
            acc_scr[h] = pv if first else alphas[h] * acc_scr[h] + pv

    @pl.when(i == 0)
    def _():
        fold([i], [BIAS_OWN], first=True)

    @pl.when(i >= 1)
    def _():
        fold([i, i - 1], [BIAS_OWN, BIAS_PREV], first=True)

    n_far = i - 1

    def far_pair(p, carry):
        fold([2 * p, 2 * p + 1], [None, None], first=False)
        return carry

    lax.fori_loop(0, n_far // 2, far_pair, 0)

    @pl.when((n_far >= 1) & (n_far % 2 == 1))
    def _():
        fold([n_far - 1], [None], first=False)

    for h in range(nh):
        y = acc_scr[h, :hd, :] * (1.0 / acc_scr[h, hd:hd + 1, :])
        o_ref[:, h * hd:(h + 1) * hd] = y.T.astype(o_ref.dtype)


def _attn_call(qT, k2, vT, km_hi, km_lo, bias_tiles):
    b, h, nb, v_rows, blk = vT.shape
    hd = HEAD_DIM
    s = nb * blk
    onehot = np.zeros((nb, blk, LANES), np.float32)
    for j in range(nb):
        onehot[j, :, j] = 1.0
    once = pl.Buffered(1)
    return pl.pallas_call(
        _attn_kernel,
        grid=(b, nb),
        in_specs=[
            pl.BlockSpec((None, h, hd, blk), lambda bi, i: (bi, 0, 0, i)),
            pl.BlockSpec((s, h * hd), lambda bi, i: (bi, 0)),
            pl.BlockSpec((None, h, nb, v_rows, blk), lambda bi, i: (bi, 0, 0, 0, 0)),
            pl.BlockSpec((None, h, nb, hd), lambda bi, i: (bi, 0, 0, 0)),
            pl.BlockSpec((None, h, nb, hd), lambda bi, i: (bi, 0, 0, 0)),
            pl.BlockSpec((nb, blk, LANES), lambda bi, i: (0, 0, 0), pipeline_mode=once),
            pl.BlockSpec((h, N_BIAS_TILES, blk, blk), lambda bi, i: (0, 0, 0, 0),
                         pipeline_mode=once),
        ],
        out_specs=pl.BlockSpec((blk, h * hd), lambda bi, i: (bi * nb + i, 0)),
        out_shape=jax.ShapeDtypeStruct((b * s, h * hd), BF16),
        scratch_shapes=[pltpu.VMEM((h, 2 * hd, blk), BF16), pltpu.VMEM((h, 1, blk), F32),
                        pltpu.VMEM((h, v_rows, blk), F32)],
        compiler_params=pltpu.CompilerParams(
            dimension_semantics=("arbitrary", "arbitrary"),
            vmem_limit_bytes=VMEM_LIMIT_BYTES),
        name="moba_attention",
    )(qT, k2, vT, km_hi, km_lo, jnp.asarray(onehot, BF16), bias_tiles)


ROUTE_E1, ROUTE_E2, ROUTE_W1, ROUTE_W2, ROUTE_R1, ROUTE_R2 = range(6)


def _dot_nt(a, b):
    return lax.dot_general(a, b, (((1,), (1,)), ((), ())), preferred_element_type=F32)


def _store_token_major(ref, x):
    rows = x.shape[0]
    for s in range(TOKEN_SUBLANES):
        ref[pl.ds(s, rows, stride=TOKEN_SUBLANES), :] = x[:, s * LANES:(s + 1) * LANES]


def _load_token_major(ref, rows):
    return jnp.concatenate(
        [ref[pl.ds(s, rows, stride=TOKEN_SUBLANES), :] for s in range(TOKEN_SUBLANES)], axis=1)


def _merge_kernel(x_ref, ya_ref, gb_ref, yb_ref, wo_ref, ng_ref, wrh_ref, wrl_ref, br_ref,
                  h_ref, xn_ref, route_ref, route_t_ref, counts_ref, run_scr):
    @pl.when(pl.program_id(0) == 0)
    def _():
        run_scr[...] = jnp.zeros_like(run_scr)

    f = lambda r: r[...].astype(F32)
    mix = (f(ya_ref) + f(gb_ref) * f(yb_ref)).astype(BF16)
    h = x_ref[...] + _dot(mix, wo_ref[...])
    h_ref[...] = h
    xn = _rmsnorm(h, ng_ref[...])
    _store_token_major(xn_ref, xn)
    rows = xn.shape[0]

    x_hi = xn.astype(BF16)
    x_lo = (xn - x_hi.astype(F32)).astype(BF16)
    logits = (_dot_nt(wrh_ref[...], x_hi) + _dot_nt(wrh_ref[...], x_lo)
              + _dot_nt(wrl_ref[...], x_hi) + br_ref[...])
    unit = lax.broadcasted_iota(jnp.int32, logits.shape, 0).astype(F32)
    big = float(ROUTER_UNITS)
    neg_inf = -jnp.inf

    gl = jnp.where((unit >= GROUP_UNIT0) & (unit < GROUP_UNIT0 + N_GROUPS), logits, neg_inf)
    gmax = jnp.max(gl, axis=0, keepdims=True)
    g_w = 1.0 / jnp.sum(jnp.exp(gl - gmax), axis=0, keepdims=True)
    g_idx = jnp.min(jnp.where(gl == gmax, unit, big), axis=0, keepdims=True) - GROUP_UNIT0

    e0 = g_idx * EXPERTS_PER_GROUP
    el = jnp.where((unit >= e0) & (unit < e0 + EXPERTS_PER_GROUP), logits, neg_inf)
    m1 = jnp.max(el, axis=0, keepdims=True)
    i1 = jnp.min(jnp.where(el == m1, unit, big), axis=0, keepdims=True)
    el2 = jnp.where(unit == i1, neg_inf, el)
    m2 = jnp.max(el2, axis=0, keepdims=True)
    i2 = jnp.min(jnp.where(el2 == m2, unit, big), axis=0, keepdims=True)
    e2 = jnp.exp(m2 - m1)
    den = 1.0 + e2
    w1 = (1.0 / den) * g_w
    w2 = (e2 / den) * g_w

    hit1 = unit == i1
    hit2 = unit == i2
    onehot = jnp.where(hit1, 1.0, jnp.where(hit2, 1.0, 0.0))
    c_idx = lax.broadcasted_iota(jnp.int32, (rows, rows), 0)
    r_idx = lax.broadcasted_iota(jnp.int32, (rows, rows), 1)
    earlier = jnp.where(c_idx < r_idx, 1.0, 0.0).astype(BF16)
    prefix = run_scr[...] + _dot(onehot.astype(BF16), earlier)
    rank1 = jnp.sum(jnp.where(hit1, prefix, 0.0), axis=0, keepdims=True)
    rank2 = jnp.sum(jnp.where(hit2, prefix, 0.0), axis=0, keepdims=True)
    run_scr[...] = run_scr[...] + jnp.sum(onehot, axis=1, keepdims=True)
    counts_ref[...] = run_scr[...]

    route_t = jnp.concatenate(
        [i1, i2, w1, w2, rank1, rank2, jnp.zeros((ROUTER_LANES - 6, rows), F32)], axis=0)
    route_t_ref[0] = route_t[:ROUTE_ROWS]
    route_ref[...] = route_t.T


def _merge_call(x2, ya, gb, yb, w_out, norm_g, wr_hi, wr_lo, b_router):
    t, d = x2.shape
    rows = MERGE_ROWS
    assert t % rows == 0 and d == TOKEN_SUBLANES * LANES
    n_tiles = t // rows
    row_spec = pl.BlockSpec((rows, d), lambda i: (i, 0))
    const2 = lambda i: (0, 0)
    return pl.pallas_call(
        _merge_kernel,
        grid=(n_tiles,),
        in_specs=[row_spec, row_spec, row_spec, row_spec,
                  pl.BlockSpec((d, d), const2),
                  pl.BlockSpec((1, d), const2),
                  pl.BlockSpec((ROUTER_UNITS, d), const2),
                  pl.BlockSpec((ROUTER_UNITS, d), const2),
                  pl.BlockSpec((ROUTER_UNITS, 1), const2)],
        out_specs=[row_spec,
                   pl.BlockSpec((rows * TOKEN_SUBLANES, LANES), lambda i: (i, 0)),
                   pl.BlockSpec((rows, ROUTER_LANES), lambda i: (i, 0)),
                   pl.BlockSpec((1, ROUTE_ROWS, rows), lambda i: (i, 0, 0)),
                   pl.BlockSpec((ROUTER_UNITS, 1), const2)],
        out_shape=[jax.ShapeDtypeStruct((t, d), F32),
                   jax.ShapeDtypeStruct((t * TOKEN_SUBLANES, LANES), F32),
                   jax.ShapeDtypeStruct((t, ROUTER_LANES), F32),
                   jax.ShapeDtypeStruct((n_tiles, ROUTE_ROWS, rows), F32),
                   jax.ShapeDtypeStruct((ROUTER_UNITS, 1), F32)],
        scratch_shapes=[pltpu.VMEM((ROUTER_UNITS, 1), F32)],
        compiler_params=pltpu.CompilerParams(
            dimension_semantics=("arbitrary",), vmem_limit_bytes=VMEM_LIMIT_BYTES),
        name="merge_outproj_router",
    )(x2, ya, gb, yb, w_out, norm_g, wr_hi, wr_lo, b_router)


def _token_rows(ref, token):
    return ref.at[pl.ds(pl.multiple_of(token * TOKEN_SUBLANES, TOKEN_SUBLANES), TOKEN_SUBLANES)]


def _dispatch_kernel(last_ref, nreal_ref, pos1_ref, pos2_ref, xn_ref, xs_hbm, zero_scr, sem):
    rows = pos1_ref.shape[2]
    tile = EXPERT_ROWS
    n_tiles = xs_hbm.shape[0] // (tile * TOKEN_SUBLANES)

    def zero_tile(j):
        start = pl.multiple_of(j * (tile * TOKEN_SUBLANES), tile * TOKEN_SUBLANES)
        return pltpu.make_async_copy(
            zero_scr, xs_hbm.at[pl.ds(start, tile * TOKEN_SUBLANES)], sem)

    @pl.when(pl.program_id(0) == 0)
    def _():
        zero_scr[...] = jnp.zeros_like(zero_scr)
        for e in range(N_EXPERTS):
            @pl.when(last_ref[e] >= 0)
            def _():
                zero_tile(last_ref[e]).start()

        def tail_start(j, carry):
            zero_tile(j).start()
            return carry

        lax.fori_loop(nreal_ref[0], n_tiles, tail_start, 0)

        for e in range(N_EXPERTS):
            @pl.when(last_ref[e] >= 0)
            def _():
                zero_tile(0).wait()

        def tail_wait(j, carry):
            zero_tile(0).wait()
            return carry

        lax.fori_loop(nreal_ref[0], n_tiles, tail_wait, 0)

    def issue(g, carry):
        for u in range(DMA_UNROLL):
            r = g * DMA_UNROLL + u
            src = _token_rows(xn_ref, r)
            pltpu.make_async_copy(src, _token_rows(xs_hbm, pos1_ref[0, 0, r]), sem).start(0)
            pltpu.make_async_copy(src, _token_rows(xs_hbm, pos2_ref[0, 0, r]), sem).start(1)
        return carry

    lax.fori_loop(0, rows // DMA_UNROLL, issue, 0)

    for _ in range(2):
        pltpu.make_async_copy(xn_ref, xs_hbm.at[pl.ds(0, rows * TOKEN_SUBLANES)], sem).wait()


def _dispatch_call(last_tile, n_real, pos1, pos2, xn, n_sorted_rows):
    n_steps, _, rows = pos1.shape
    smem_row = pl.BlockSpec((1, 1, rows), lambda i, lt, nr: (i, 0, 0), memory_space=pltpu.SMEM)
    return pl.pallas_call(
        _dispatch_kernel,
        grid_spec=pltpu.PrefetchScalarGridSpec(
            num_scalar_prefetch=2,
            grid=(n_steps,),
            in_specs=[smem_row, smem_row,
                      pl.BlockSpec((rows * TOKEN_SUBLANES, LANES), lambda i, lt, nr: (i, 0))],
            out_specs=pl.BlockSpec(memory_space=pl.ANY),
            scratch_shapes=[pltpu.VMEM((EXPERT_ROWS * TOKEN_SUBLANES, LANES), F32),
                            pltpu.SemaphoreType.DMA(())],
        ),
        out_shape=jax.ShapeDtypeStruct((n_sorted_rows * TOKEN_SUBLANES, LANES), F32),
        compiler_params=pltpu.CompilerParams(dimension_semantics=("arbitrary",)),
        name="moe_dispatch",
    )(last_tile, n_real, pos1, pos2, xn)


EXPERT_IN_SLOTS = 3
EXPERT_OUT_SLOTS = 2


def _expert_kernel(first_ref, end_ref, nreal_ref, xs_hbm, w1_ref, w3_ref, w2_ref, ys_hbm,
                   w1_scr, w3_scr, w2_scr, xbuf, ybuf, in_sems, out_sems):
    e = pl.program_id(0)
    tile_rows = EXPERT_ROWS * TOKEN_SUBLANES
    n_real = nreal_ref[0]
    n_tiles = xs_hbm.shape[0] // tile_rows

    def tile_of(ref, t):
        return ref.at[pl.ds(pl.multiple_of(t * tile_rows, tile_rows), tile_rows)]

    def in_copy(t):
        slot = t % EXPERT_IN_SLOTS
        return pltpu.make_async_copy(tile_of(xs_hbm, t), xbuf.at[slot], in_sems.at[slot])

    def out_copy(t):
        slot = t % EXPERT_OUT_SLOTS
        return pltpu.make_async_copy(ybuf.at[slot], tile_of(ys_hbm, t), out_sems.at[slot])

    @pl.when(e == 0)
    def _():
        for t in range(EXPERT_IN_SLOTS - 1):
            @pl.when(t < n_real)
            def _():
                in_copy(t).start()

    w1_scr[...] = w1_ref[...].astype(BF16)
    w3_scr[...] = w3_ref[...].astype(BF16)
    w2_scr[...] = w2_ref[...].astype(BF16)

    def tile(t, carry):
        ahead = t + EXPERT_IN_SLOTS - 1

        @pl.when(ahead < n_real)
        def _():
            in_copy(ahead).start()

        in_copy(t).wait()

        @pl.when(t >= EXPERT_OUT_SLOTS)
        def _():
            out_copy(t - EXPERT_OUT_SLOTS).wait()

        x = _load_token_major(xbuf.at[t % EXPERT_IN_SLOTS], EXPERT_ROWS).astype(BF16)
        a = _dot(x, w1_scr[...])
        b = _dot(x, w3_scr[...])
        hid = (a * _sigmoid(a)) * b
        _store_token_major(ybuf.at[t % EXPERT_OUT_SLOTS], _dot(hid.astype(BF16), w2_scr[...]))
        out_copy(t).start()
        return carry

    lax.fori_loop(first_ref[e], end_ref[e], tile, 0)

    @pl.when(e == pl.num_programs(0) - 1)
    def _():
        for back in range(EXPERT_OUT_SLOTS, 0, -1):
            @pl.when(n_real - back >= 0)
            def _():
                out_copy(n_real - back).wait()
        ybuf[0] = jnp.zeros(ybuf.shape[1:], F32)

        def tail_start(t, carry):
            pltpu.make_async_copy(ybuf.at[0], tile_of(ys_hbm, t), out_sems.at[0]).start()
            return carry

        def tail_wait(t, carry):
            pltpu.make_async_copy(ybuf.at[0], tile_of(ys_hbm, t), out_sems.at[0]).wait()
            return carry

        lax.fori_loop(n_real, n_tiles, tail_start, 0)
        lax.fori_loop(n_real, n_tiles, tail_wait, 0)


def _expert_call(first_tile, end_tile, n_real, xs, w1, w3, w2):
    n_experts, d, d_expert = w1.shape
    tile_rows = EXPERT_ROWS * TOKEN_SUBLANES
    per_expert = lambda e, f, n, nr: (e, 0, 0)
    return pl.pallas_call(
        _expert_kernel,
        grid_spec=pltpu.PrefetchScalarGridSpec(
            num_scalar_prefetch=3,
            grid=(n_experts,),
            in_specs=[pl.BlockSpec(memory_space=pl.ANY),
                      pl.BlockSpec((None, d, d_expert), per_expert),
                      pl.BlockSpec((None, d, d_expert), per_expert),
                      pl.BlockSpec((None, d_expert, d), per_expert)],
            out_specs=pl.BlockSpec(memory_space=pl.ANY),
            scratch_shapes=[pltpu.VMEM((d, d_expert), BF16), pltpu.VMEM((d, d_expert), BF16),
                            pltpu.VMEM((d_expert, d), BF16),
                            pltpu.VMEM((EXPERT_IN_SLOTS, tile_rows, LANES), F32),
                            pltpu.VMEM((EXPERT_OUT_SLOTS, tile_rows, LANES), F32),
                            pltpu.SemaphoreType.DMA((EXPERT_IN_SLOTS,)),
                            pltpu.SemaphoreType.DMA((EXPERT_OUT_SLOTS,))],
        ),
        out_shape=jax.ShapeDtypeStruct(xs.shape, F32),
        compiler_params=pltpu.CompilerParams(
            dimension_semantics=("arbitrary",), vmem_limit_bytes=VMEM_LIMIT_BYTES),
        name="moe_experts",
    )(first_tile, end_tile, n_real, xs, w1, w3, w2)


def _combine_kernel(p1_first, p2_first, p1_next, p2_next, ys_hbm, h_ref, route_ref, ng_ref,
                    out_ref, buf, sems):
    i = pl.program_id(0)
    n = pl.num_programs(0)
    rows = h_ref.shape[0]

    def issue(p1_ref, p2_ref, slot):
        def body(g, carry):
            for u in range(DMA_UNROLL):
                r = g * DMA_UNROLL + u
                for which, p_ref in ((0, p1_ref), (1, p2_ref)):
                    pltpu.make_async_copy(_token_rows(ys_hbm, p_ref[0, 0, r]),
                                          _token_rows(buf.at[2 * slot + which], r),
                                          sems.at[slot]).start(which)
            return carry
        lax.fori_loop(0, rows // DMA_UNROLL, body, 0)

    @pl.when(i == 0)
    def _():
        issue(p1_first, p2_first, 0)

    @pl.when(i + 1 < n)
    def _():
        issue(p1_next, p2_next, (i + 1) % 2)

    slot = i % 2
    for which in range(2):
        pltpu.make_async_copy(ys_hbm.at[pl.ds(0, rows * TOKEN_SUBLANES)],
                              buf.at[2 * slot + which], sems.at[slot]).wait()

    route = route_ref[...]
    w1 = route[:, ROUTE_W1:ROUTE_W1 + 1]
    w2 = route[:, ROUTE_W2:ROUTE_W2 + 1]
    y = (h_ref[...] + w1 * _load_token_major(buf.at[2 * slot], rows)
         + w2 * _load_token_major(buf.at[2 * slot + 1], rows))
    out_ref[...] = _rmsnorm(y, ng_ref[...])


def _combine_call(pos1, pos2, ys, h, route, norm_g):
    t, d = h.shape
    n_steps, _, rows = pos1.shape
    row_spec = pl.BlockSpec((rows, d), lambda i: (i, 0))
    first = pl.BlockSpec((1, 1, rows), lambda i: (0, 0, 0), memory_space=pltpu.SMEM)
    nxt = pl.BlockSpec((1, 1, rows), lambda i: (jnp.minimum(i + 1, n_steps - 1), 0, 0),
                       memory_space=pltpu.SMEM)
    return pl.pallas_call(
        _combine_kernel,
        grid=(n_steps,),
        in_specs=[first, first, nxt, nxt,
                  pl.BlockSpec(memory_space=pl.ANY),
                  row_spec,
                  pl.BlockSpec((rows, ROUTER_LANES), lambda i: (i, 0)),
                  pl.BlockSpec((1, d), lambda i: (0, 0))],
        out_specs=row_spec,
        out_shape=jax.ShapeDtypeStruct((t, d), F32),
        scratch_shapes=[pltpu.VMEM((4, rows * TOKEN_SUBLANES, LANES), F32),
                        pltpu.SemaphoreType.DMA((2,))],
        compiler_params=pltpu.CompilerParams(
            dimension_semantics=("arbitrary",), vmem_limit_bytes=VMEM_LIMIT_BYTES),
        name="moe_combine",
    )(pos1, pos2, pos1, pos2, ys, h, route, norm_g)


def _sparse_moe(xn, route, route_t, counts, h, w1, w3, w2, norm_g):
    t = h.shape[0]
    tile = EXPERT_ROWS
    n_tiles = (2 * t) // tile + N_EXPERTS
    expert = jnp.arange(N_EXPERTS, dtype=jnp.int32)
    counts = counts[:N_EXPERTS, 0].astype(jnp.int32)
    group_tiles = (counts + tile - 1) // tile
    end_tile = jnp.sum(jnp.where(expert[None, :] <= expert[:, None], group_tiles[None, :], 0), axis=1)
    first_tile = end_tile - group_tiles
    n_real = end_tile[-1:]
    last_tile = jnp.where(group_tiles > 0, end_tile - 1, -1)

    def positions(e_row, r_row):
        e = route_t[:, e_row, :].astype(jnp.int32)
        start = jnp.zeros_like(e)
        for k in range(N_EXPERTS):
            start = jnp.where(e == k, first_tile[k] * tile, start)
        return start + route_t[:, r_row, :].astype(jnp.int32)

    pos1 = positions(ROUTE_E1, ROUTE_R1)
    pos2 = positions(ROUTE_E2, ROUTE_R2)
    per_step = lambda pos, rows: pos.reshape(t // rows, 1, rows)
    xs = _dispatch_call(last_tile, n_real, per_step(pos1, DISPATCH_ROWS),
                        per_step(pos2, DISPATCH_ROWS), xn, n_tiles * tile)
    ys = _expert_call(first_tile, end_tile, n_real, xs, w1, w3, w2)
    return _combine_call(per_step(pos1, COMBINE_ROWS), per_step(pos2, COMBINE_ROWS), ys, h,
                         route, norm_g)


def _layer(h, norm_mix_g, w_in, b_gates, gmlp_ln_g, gmlp_ln_b, w_spatial, b_spatial, bias_tiles,
           w_out, norm_ffn_g, w_group_router, b_group_router, w_expert_router, b_expert_router,
           w1, w3, w2, norm_out_g):
    b, s, d = h.shape
    t = b * s
    nb = s // MOBA_BLOCK
    x2 = h.reshape(t, d)
    row = lambda v: v.reshape(1, -1)

    ya, qT, k, vT, gb, kmean = _proj_call(
        x2, b, row(norm_mix_g), w_in.astype(BF16), row(b_gates), row(gmlp_ln_g), row(gmlp_ln_b),
        w_spatial, b_spatial[:, :, None])

    km = jnp.transpose(kmean.reshape(b, nb, ATT_HEADS, HEAD_DIM), (0, 2, 1, 3))
    km_hi = km.astype(BF16)
    km_lo = (km - km_hi.astype(F32)).astype(BF16)
    yb = _attn_call(qT, k, vT, km_hi, km_lo, bias_tiles)

    w_router = jnp.concatenate(
        [jnp.transpose(w_expert_router, (0, 2, 1)).reshape(N_EXPERTS, d), w_group_router.T,
         jnp.zeros((ROUTER_UNITS - N_EXPERTS - N_GROUPS, d), F32)], axis=0)
    b_router = jnp.concatenate(
        [b_expert_router.reshape(-1), b_group_router,
         jnp.zeros((ROUTER_UNITS - N_EXPERTS - N_GROUPS,), F32)]).reshape(ROUTER_UNITS, 1)
    wr_hi = w_router.astype(BF16)
    wr_lo = (w_router - wr_hi.astype(F32)).astype(BF16)
    h2, xn, route, route_t, counts = _merge_call(
        x2, ya, gb, yb, w_out.astype(BF16), row(norm_ffn_g), wr_hi, wr_lo, b_router)

    out = _sparse_moe(xn, route, route_t, counts, h2, w1, w3, w2, row(norm_out_g))
    return out.reshape(b, s, d)


def kernel(x, norm_mix_g, w_in, b_gates, gmlp_ln_g, gmlp_ln_b, w_spatial, b_spatial, rel_bias, w_out, norm_ffn_g, w_group_router, b_group_router, w_expert_router, b_expert_router, w1, w3, w2, norm_final_g):
    depth = w_in.shape[0]
    assert depth == 1, "the final rmsnorm is fused into the last layer's combine kernel"
    bias_tiles = _bias_call(rel_bias)
    return _layer(x, norm_mix_g[0], w_in[0], b_gates[0], gmlp_ln_g[0], gmlp_ln_b[0], w_spatial[0],
                  b_spatial[0], bias_tiles, w_out[0], norm_ffn_g[0], w_group_router[0],
                  b_group_router[0], w_expert_router[0], b_expert_router[0], w1[0], w3[0], w2[0],
                  norm_final_g)
```

```python
import functools
import math

import numpy as np
import jax
import jax.numpy as jnp
from jax import lax
from jax.experimental import pallas as pl
from jax.experimental.pallas import tpu as pltpu

F32 = jnp.float32
BF16 = jnp.bfloat16

D_MODEL = 1024
NORM_EPS = 1e-6
GMLP_GROUPS = 8
GMLP_CHUNK = 128
ATT_HEADS = 8
HEAD_DIM = 128
MOBA_BLOCK = 256
MOBA_TOPK = 3
REL_BUCKETS = 32
REL_MAX_DIST = 128
N_GROUPS = 4
EXPERTS_PER_GROUP = 8
N_EXPERTS = N_GROUPS * EXPERTS_PER_GROUP
D_EXPERT = 256
N_SEGMENTS = 7

LANES = 128
TOKEN_SUBLANES = 8
ROUTE_ROWS = 8
VMEM_LIMIT_BYTES = 56 * 1024 * 1024

SQRT_HALF = math.sqrt(0.5)
LOG2E = math.log2(math.e)
SCORE_SCALE2 = (HEAD_DIM ** -0.5) * LOG2E
MASK_NEG = -(2.0 ** 100)
BF16_SUBLANES = 16
V_ROWS = HEAD_DIM + BF16_SUBLANES
ROUTER_LANES = LANES
ROUTER_UNITS = -(-(N_EXPERTS + N_GROUPS) // BF16_SUBLANES) * BF16_SUBLANES
GROUP_UNIT0 = N_EXPERTS

PROJ_ROWS = 512
MERGE_ROWS = 512
EXPERT_ROWS = 256
DISPATCH_ROWS = 2048
COMBINE_ROWS = 256
DMA_UNROLL = 8


def _rmsnorm(x, g):
    return x * lax.rsqrt(jnp.mean(x * x, axis=-1, keepdims=True) + NORM_EPS) * g


def _gelu(a):
    return 0.5 * a * (1.0 + lax.erf(a * SQRT_HALF))


def _sigmoid(a):
    return 1.0 / (1.0 + jnp.exp(-a))


def _dot(a, b):
    return jnp.dot(a, b, preferred_element_type=F32)


def _proj_kernel(x_ref, ng_ref, w_ref, bg_ref, lng_ref, lnb_ref, ws_ref, bs_ref,
                 ya_ref, qT_ref, k_ref, vT_ref, gb_ref, kmean_ref,
                 xn_scr, vln_scr, mix_scr):
    rows = x_ref.shape[0]
    d = D_MODEL
    xn_scr[...] = _rmsnorm(x_ref[...], ng_ref[...]).astype(BF16)

    def seg(i):
        return _dot(xn_scr[...], w_ref[:, i * d:(i + 1) * d])

    hd = HEAD_DIM

    v = _gelu(seg(1))
    mu = jnp.mean(v, axis=-1, keepdims=True)
    vc = v - mu
    var = jnp.mean(vc * vc, axis=-1, keepdims=True)
    vln_scr[...] = (vc * lax.rsqrt(var + NORM_EPS) * lng_ref[...] + lnb_ref[...]).astype(BF16)

    mix_scr[...] = _gelu(seg(0)) * _sigmoid(seg(5) + bg_ref[:, :d])

    q = seg(2) * SCORE_SCALE2
    for h in range(ATT_HEADS):
        qT_ref[h] = q[:, h * hd:(h + 1) * hd].T.astype(BF16)

    t_idx = lax.broadcasted_iota(jnp.int32, (GMLP_CHUNK, GMLP_CHUNK), 0)
    s_idx = lax.broadcasted_iota(jnp.int32, (GMLP_CHUNK, GMLP_CHUNK), 1)
    causal = t_idx >= s_idx
    gd = d // GMLP_GROUPS
    n_chunks = rows // GMLP_CHUNK
    for g in range(GMLP_GROUPS):
        ws = jnp.where(causal, ws_ref[g], 0.0).astype(BF16)
        bias = bs_ref[g]
        vg = jnp.concatenate(
            [vln_scr[c * GMLP_CHUNK:(c + 1) * GMLP_CHUNK, g * gd:(g + 1) * gd]
             for c in range(n_chunks)], axis=1)
        mixed = _dot(ws, vg)
        for c in range(n_chunks):
            blk_rows = slice(c * GMLP_CHUNK, (c + 1) * GMLP_CHUNK)
            blk_cols = slice(g * gd, (g + 1) * gd)
            ya_ref[blk_rows, blk_cols] = (
                mix_scr[blk_rows, blk_cols] * (mixed[:, c * gd:(c + 1) * gd] + bias)).astype(BF16)

    v = seg(4)
    for blk in range(rows // MOBA_BLOCK):
        r0 = blk * MOBA_BLOCK
        for h in range(ATT_HEADS):
            vT_ref[h, blk, :hd, :] = v[r0:r0 + MOBA_BLOCK, h * hd:(h + 1) * hd].T.astype(BF16)
            vT_ref[h, blk, hd:, :] = jnp.ones((V_ROWS - hd, MOBA_BLOCK), BF16)

    gb_ref[...] = _sigmoid(seg(6) + bg_ref[:, d:]).astype(BF16)

    k = seg(3)
    k_ref[...] = k.astype(BF16)
    for blk in range(rows // MOBA_BLOCK):
        r0 = blk * MOBA_BLOCK
        kmean_ref[0, blk:blk + 1, :] = jnp.mean(k[r0:r0 + MOBA_BLOCK, :], axis=0, keepdims=True)


def _proj_call(x2, batch, norm_g, w_in, b_gates, ln_g, ln_b, w_spatial, b_spatial):
    t, d = x2.shape
    rows = PROJ_ROWS
    seq = t // batch
    assert seq % rows == 0 and rows % MOBA_BLOCK == 0 and rows % GMLP_CHUNK == 0
    n_tiles = t // rows
    tiles_per_seq = seq // rows
    blocks_per_tile = rows // MOBA_BLOCK
    nb = seq // MOBA_BLOCK
    row_spec = pl.BlockSpec((rows, d), lambda i: (i, 0))
    const2 = lambda i: (0, 0)
    const3 = lambda i: (0, 0, 0)
    act = jax.ShapeDtypeStruct((t, d), BF16)
    qT_spec = pl.BlockSpec((None, ATT_HEADS, HEAD_DIM, rows),
                           lambda i: (i // tiles_per_seq, 0, 0, i % tiles_per_seq))
    vT_spec = pl.BlockSpec((None, ATT_HEADS, blocks_per_tile, V_ROWS, MOBA_BLOCK),
                           lambda i: (i // tiles_per_seq, 0, i % tiles_per_seq, 0, 0))
    return pl.pallas_call(
        _proj_kernel,
        grid=(n_tiles,),
        in_specs=[
            row_spec,
            pl.BlockSpec((1, d), const2),
            pl.BlockSpec((d, N_SEGMENTS * d), const2, pipeline_mode=pl.Buffered(1)),
            pl.BlockSpec((1, 2 * d), const2),
            pl.BlockSpec((1, d), const2),
            pl.BlockSpec((1, d), const2),
            pl.BlockSpec((GMLP_GROUPS, GMLP_CHUNK, GMLP_CHUNK), const3),
            pl.BlockSpec((GMLP_GROUPS, GMLP_CHUNK, 1), const3),
        ],
        out_specs=[row_spec, qT_spec, row_spec, vT_spec, row_spec,
                   pl.BlockSpec((1, blocks_per_tile, d), lambda i: (i, 0, 0))],
        out_shape=[act,
                   jax.ShapeDtypeStruct((batch, ATT_HEADS, HEAD_DIM, seq), BF16),
                   act,
                   jax.ShapeDtypeStruct((batch, ATT_HEADS, nb, V_ROWS, MOBA_BLOCK), BF16),
                   act,
                   jax.ShapeDtypeStruct((n_tiles, blocks_per_tile, d), F32)],
        scratch_shapes=[pltpu.VMEM((rows, d), BF16), pltpu.VMEM((rows, d), BF16),
                        pltpu.VMEM((rows, d), F32)],
        compiler_params=pltpu.CompilerParams(
            dimension_semantics=("arbitrary",), vmem_limit_bytes=VMEM_LIMIT_BYTES),
        name="proj_gmlp",
    )(x2, norm_g, w_in, b_gates, ln_g, ln_b, w_spatial, b_spatial)


def _t5_bucket_np(n):
    n = np.maximum(n, 0)
    max_exact = REL_BUCKETS // 2
    nf = np.maximum(n, max_exact).astype(np.float32)
    large = max_exact + (np.log(nf / max_exact) / math.log(REL_MAX_DIST / max_exact)
                         * (REL_BUCKETS - max_exact)).astype(np.int32)
    large = np.minimum(large, REL_BUCKETS - 1)
    return np.where(n < max_exact, n, large).astype(np.int32)


def _bucket_tiles():
    kpos = np.arange(MOBA_BLOCK, dtype=np.int32)[:, None]
    qpos = np.arange(MOBA_BLOCK, dtype=np.int32)[None, :]
    rel = qpos - kpos
    return np.stack([_t5_bucket_np(rel), _t5_bucket_np(rel + MOBA_BLOCK)])


def _bias_kernel(relb_ref, bucket_ref, out_ref):
    h = pl.program_id(0)
    far = relb_ref[REL_BUCKETS - 1, h]
    k_idx = lax.broadcasted_iota(jnp.int32, (MOBA_BLOCK, MOBA_BLOCK), 0)
    q_idx = lax.broadcasted_iota(jnp.int32, (MOBA_BLOCK, MOBA_BLOCK), 1)
    for tile in range(2):
        bucket = bucket_ref[tile]
        bias = jnp.zeros((MOBA_BLOCK, MOBA_BLOCK), F32)
        for b in range(REL_BUCKETS):
            bias = jnp.where(bucket == b, relb_ref[b, h], bias)
        bias2 = (bias - far) * LOG2E
        if tile == BIAS_OWN:
            bias2 = jnp.where(q_idx >= k_idx, bias2, MASK_NEG)
        out_ref[0, tile] = bias2


def _bias_call(rel_bias):
    buckets = _bucket_tiles()
    return pl.pallas_call(
        _bias_kernel,
        grid=(ATT_HEADS,),
        in_specs=[pl.BlockSpec(memory_space=pltpu.SMEM),
                  pl.BlockSpec((2, MOBA_BLOCK, MOBA_BLOCK), lambda h: (0, 0, 0))],
        out_specs=pl.BlockSpec((1, N_BIAS_TILES, MOBA_BLOCK, MOBA_BLOCK), lambda h: (h, 0, 0, 0)),
        out_shape=jax.ShapeDtypeStruct((ATT_HEADS, N_BIAS_TILES, MOBA_BLOCK, MOBA_BLOCK), F32),
        compiler_params=pltpu.CompilerParams(dimension_semantics=("arbitrary",)),
        name="t5_bias_tiles",
    )(rel_bias, jnp.asarray(buckets))


BIAS_OWN, BIAS_PREV = range(2)
N_BIAS_TILES = 2


def _attn_kernel(qT_ref, k_ref, vT_ref, kmh_ref, kml_ref, oh_ref, bias_ref, o_ref,
                 qp_scr, m_scr, acc_scr):
    i = pl.program_id(1)
    nh, nb = kmh_ref.shape[0], kmh_ref.shape[1]
    blk = MOBA_BLOCK
    hd = HEAD_DIM

    n_idx = lax.broadcasted_iota(jnp.int32, (nb, blk), 0)
    past = n_idx < i
    for h in range(nh):
        qT = qT_ref[h]
        gate = _dot(kmh_ref[h], qT) + _dot(kml_ref[h], qT)
        gate = jnp.where(past, gate, -jnp.inf)
        rank = jnp.zeros((nb, blk), F32)
        for m in range(nb):
            row = gate[m:m + 1, :]
            beats = (row > gate) | ((row == gate) & (m < n_idx))
            rank = rank + jnp.where(beats, 1.0, 0.0)
        keep = (past & (rank < MOBA_TOPK)) | (n_idx == i)
        sel = jnp.where(keep, 0.0, MASK_NEG)
        sel = jnp.concatenate([sel, jnp.zeros((hd - nb, blk), F32)], axis=0)
        qp_scr[h] = jnp.concatenate([qT, sel.astype(BF16)], axis=0)

    def scores(h, j):
        r0 = pl.multiple_of(j * blk, blk)
        kp = jnp.concatenate([k_ref[pl.ds(r0, blk), h * hd:(h + 1) * hd], oh_ref[j]], axis=1)
        return _dot(kp, qp_scr[h])

    def fold(blocks, bias_tiles, first):
        ss = []
        for h in range(nh):
            parts = []
            for j, tile in zip(blocks, bias_tiles):
                s = scores(h, j)
                parts.append(s if tile is None else s + bias_ref[h, tile])
            ss.append(parts)
        ps, alphas = [], []
        for h in range(nh):
            m_new = functools.reduce(
                jnp.maximum, [jnp.max(s, axis=0, keepdims=True) for s in ss[h]])
            if not first:
                m_old = m_scr[h]
                m_new = jnp.maximum(m_old, m_new)
                alphas.append(jnp.exp2(m_old - m_new))
            m_scr[h] = m_new
            ps.append([jnp.exp2(s - m_new).astype(BF16) for s in ss[h]])
        for h in range(nh):
            pv = functools.reduce(
                lambda a, b: a + b, [_dot(vT_ref[h, j], x) for j, x in zip(blocks, ps[h])])
            acc_scr[h] = pv if first else alphas[h] * acc_scr[h] + pv

    @pl.when(i == 0)
    def _():
        fold([i], [BIAS_OWN], first=True)

    @pl.when(i >= 1)
    def _():
        fold([i, i - 1], [BIAS_OWN, BIAS_PREV], first=True)

    n_far = i - 1

    def far_pair(p, carry):
        fold([2 * p, 2 * p + 1], [None, None], first=False)
        return carry

    lax.fori_loop(0, n_far // 2, far_pair, 0)

    @pl.when((n_far >= 1) & (n_far % 2 == 1))
    def _():
        fold([n_far - 1], [None], first=False)

    for h in range(nh):
        y = acc_scr[h, :hd, :] * (1.0 / acc_scr[h, hd:hd + 1, :])
        o_ref[:, h * hd:(h + 1) * hd] = y.T.astype(o_ref.dtype)


def _attn_call(qT, k2, vT, km_hi, km_lo, bias_tiles):
    b, h, nb, v_rows, blk = vT.shape
    hd = HEAD_DIM
    s = nb * blk
    onehot = np.zeros((nb, blk, LANES), np.float32)
    for j in range(nb):
        onehot[j, :, j] = 1.0
    once = pl.Buffered(1)
    return pl.pallas_call(
        _attn_kernel,
        grid=(b, nb),
        in_specs=[
            pl.BlockSpec((None, h, hd, blk), lambda bi, i: (bi, 0, 0, i)),
            pl.BlockSpec((s, h * hd), lambda bi, i: (bi, 0)),
            pl.BlockSpec((None, h, nb, v_rows, blk), lambda bi, i: (bi, 0, 0, 0, 0)),
            pl.BlockSpec((None, h, nb, hd), lambda bi, i: (bi, 0, 0, 0)),
            pl.BlockSpec((None, h, nb, hd), lambda bi, i: (bi, 0, 0, 0)),
            pl.BlockSpec((nb, blk, LANES), lambda bi, i: (0, 0, 0), pipeline_mode=once),
            pl.BlockSpec((h, N_BIAS_TILES, blk, blk), lambda bi, i: (0, 0, 0, 0),
                         pipeline_mode=once),
        ],
        out_specs=pl.BlockSpec((blk, h * hd), lambda bi, i: (bi * nb + i, 0)),
        out_shape=jax.ShapeDtypeStruct((b * s, h * hd), BF16),
        scratch_shapes=[pltpu.VMEM((h, 2 * hd, blk), BF16), pltpu.VMEM((h, 1, blk), F32),
                        pltpu.VMEM((h, v_rows, blk), F32)],
        compiler_params=pltpu.CompilerParams(
            dimension_semantics=("arbitrary", "arbitrary"),
            vmem_limit_bytes=VMEM_LIMIT_BYTES),
        name="moba_attention",
    )(qT, k2, vT, km_hi, km_lo, jnp.asarray(onehot, BF16), bias_tiles)


ROUTE_E1, ROUTE_E2, ROUTE_W1, ROUTE_W2, ROUTE_R1, ROUTE_R2 = range(6)


def _dot_nt(a, b):
    return lax.dot_general(a, b, (((1,), (1,)), ((), ())), preferred_element_type=F32)


def _store_token_major(ref, x):
    rows = x.shape[0]
    for s in range(TOKEN_SUBLANES):
        ref[pl.ds(s, rows, stride=TOKEN_SUBLANES), :] = x[:, s * LANES:(s + 1) * LANES]


def _load_token_major(ref, rows):
    return jnp.concatenate(
        [ref[pl.ds(s, rows, stride=TOKEN_SUBLANES), :] for s in range(TOKEN_SUBLANES)], axis=1)


def _merge_kernel(x_ref, ya_ref, gb_ref, yb_ref, wo_ref, ng_ref, wrh_ref, wrl_ref, br_ref,
                  h_ref, xn_ref, route_ref, route_t_ref, counts_ref, run_scr):
    @pl.when(pl.program_id(0) == 0)
    def _():
        run_scr[...] = jnp.zeros_like(run_scr)

    f = lambda r: r[...].astype(F32)
    mix = (f(ya_ref) + f(gb_ref) * f(yb_ref)).astype(BF16)
    h = x_ref[...] + _dot(mix, wo_ref[...])
    h_ref[...] = h
    xn = _rmsnorm(h, ng_ref[...])
    _store_token_major(xn_ref, xn)
    rows = xn.shape[0]

    x_hi = xn.astype(BF16)
    x_lo = (xn - x_hi.astype(F32)).astype(BF16)
    logits = (_dot_nt(wrh_ref[...], x_hi) + _dot_nt(wrh_ref[...], x_lo)
              + _dot_nt(wrl_ref[...], x_hi) + br_ref[...])
    unit = lax.broadcasted_iota(jnp.int32, logits.shape, 0).astype(F32)
    big = float(ROUTER_UNITS)
    neg_inf = -jnp.inf

    gl = jnp.where((unit >= GROUP_UNIT0) & (unit < GROUP_UNIT0 + N_GROUPS), logits, neg_inf)
    gmax = jnp.max(gl, axis=0, keepdims=True)
    g_w = 1.0 / jnp.sum(jnp.exp(gl - gmax), axis=0, keepdims=True)
    g_idx = jnp.min(jnp.where(gl == gmax, unit, big), axis=0, keepdims=True) - GROUP_UNIT0

    e0 = g_idx * EXPERTS_PER_GROUP
    el = jnp.where((unit >= e0) & (unit < e0 + EXPERTS_PER_GROUP), logits, neg_inf)
    m1 = jnp.max(el, axis=0, keepdims=True)
    i1 = jnp.min(jnp.where(el == m1, unit, big), axis=0, keepdims=True)
    el2 = jnp.where(unit == i1, neg_inf, el)
    m2 = jnp.max(el2, axis=0, keepdims=True)
    i2 = jnp.min(jnp.where(el2 == m2, unit, big), axis=0, keepdims=True)
    e2 = jnp.exp(m2 - m1)
    den = 1.0 + e2
    w1 = (1.0 / den) * g_w
    w2 = (e2 / den) * g_w

    hit1 = unit == i1
    hit2 = unit == i2
    onehot = jnp.where(hit1, 1.0, jnp.where(hit2, 1.0, 0.0))
    c_idx = lax.broadcasted_iota(jnp.int32, (rows, rows), 0)
    r_idx = lax.broadcasted_iota(jnp.int32, (rows, rows), 1)
    earlier = jnp.where(c_idx < r_idx, 1.0, 0.0).astype(BF16)
    prefix = run_scr[...] + _dot(onehot.astype(BF16), earlier)
    rank1 = jnp.sum(jnp.where(hit1, prefix, 0.0), axis=0, keepdims=True)
    rank2 = jnp.sum(jnp.where(hit2, prefix, 0.0), axis=0, keepdims=True)
    run_scr[...] = run_scr[...] + jnp.sum(onehot, axis=1, keepdims=True)
    counts_ref[...] = run_scr[...]

    route_t = jnp.concatenate(
        [i1, i2, w1, w2, rank1, rank2, jnp.zeros((ROUTER_LANES - 6, rows), F32)], axis=0)
    route_t_ref[0] = route_t[:ROUTE_ROWS]
    route_ref[...] = route_t.T


def _merge_call(x2, ya, gb, yb, w_out, norm_g, wr_hi, wr_lo, b_router):
    t, d = x2.shape
    rows = MERGE_ROWS
    assert t % rows == 0 and d == TOKEN_SUBLANES * LANES
    n_tiles = t // rows
    row_spec = pl.BlockSpec((rows, d), lambda i: (i, 0))
    const2 = lambda i: (0, 0)
    return pl.pallas_call(
        _merge_kernel,
        grid=(n_tiles,),
        in_specs=[row_spec, row_spec, row_spec, row_spec,
                  pl.BlockSpec((d, d), const2),
                  pl.BlockSpec((1, d), const2),
                  pl.BlockSpec((ROUTER_UNITS, d), const2),
                  pl.BlockSpec((ROUTER_UNITS, d), const2),
                  pl.BlockSpec((ROUTER_UNITS, 1), const2)],
        out_specs=[row_spec,
                   pl.BlockSpec((rows * TOKEN_SUBLANES, LANES), lambda i: (i, 0)),
                   pl.BlockSpec((rows, ROUTER_LANES), lambda i: (i, 0)),
                   pl.BlockSpec((1, ROUTE_ROWS, rows), lambda i: (i, 0, 0)),
                   pl.BlockSpec((ROUTER_UNITS, 1), const2)],
        out_shape=[jax.ShapeDtypeStruct((t, d), F32),
                   jax.ShapeDtypeStruct((t * TOKEN_SUBLANES, LANES), F32),
                   jax.ShapeDtypeStruct((t, ROUTER_LANES), F32),
                   jax.ShapeDtypeStruct((n_tiles, ROUTE_ROWS, rows), F32),
                   jax.ShapeDtypeStruct((ROUTER_UNITS, 1), F32)],
        scratch_shapes=[pltpu.VMEM((ROUTER_UNITS, 1), F32)],
        compiler_params=pltpu.CompilerParams(
            dimension_semantics=("arbitrary",), vmem_limit_bytes=VMEM_LIMIT_BYTES),
        name="merge_outproj_router",
    )(x2, ya, gb, yb, w_out, norm_g, wr_hi, wr_lo, b_router)


def _token_rows(ref, token):
    return ref.at[pl.ds(pl.multiple_of(token * TOKEN_SUBLANES, TOKEN_SUBLANES), TOKEN_SUBLANES)]


def _dispatch_kernel(last_ref, nreal_ref, pos1_ref, pos2_ref, xn_ref, xs_hbm, zero_scr, sem):
    rows = pos1_ref.shape[2]
    tile = EXPERT_ROWS
    n_tiles = xs_hbm.shape[0] // (tile * TOKEN_SUBLANES)

    def zero_tile(j):
        start = pl.multiple_of(j * (tile * TOKEN_SUBLANES), tile * TOKEN_SUBLANES)
        return pltpu.make_async_copy(
            zero_scr, xs_hbm.at[pl.ds(start, tile * TOKEN_SUBLANES)], sem)

    @pl.when(pl.program_id(0) == 0)
    def _():
        zero_scr[...] = jnp.zeros_like(zero_scr)
        for e in range(N_EXPERTS):
            @pl.when(last_ref[e] >= 0)
            def _():
                zero_tile(last_ref[e]).start()

        def tail_start(j, carry):
            zero_tile(j).start()
            return carry

        lax.fori_loop(nreal_ref[0], n_tiles, tail_start, 0)

        for e in range(N_EXPERTS):
            @pl.when(last_ref[e] >= 0)
            def _():
                zero_tile(0).wait()

        def tail_wait(j, carry):
            zero_tile(0).wait()
            return carry

        lax.fori_loop(nreal_ref[0], n_tiles, tail_wait, 0)

    def issue(g, carry):
        for u in range(DMA_UNROLL):
            r = g * DMA_UNROLL + u
            src = _token_rows(xn_ref, r)
            pltpu.make_async_copy(src, _token_rows(xs_hbm, pos1_ref[0, 0, r]), sem).start(0)
            pltpu.make_async_copy(src, _token_rows(xs_hbm, pos2_ref[0, 0, r]), sem).start(1)
        return carry

    lax.fori_loop(0, rows // DMA_UNROLL, issue, 0)

    for _ in range(2):
        pltpu.make_async_copy(xn_ref, xs_hbm.at[pl.ds(0, rows * TOKEN_SUBLANES)], sem).wait()


def _dispatch_call(last_tile, n_real, pos1, pos2, xn, n_sorted_rows):
    n_steps, _, rows = pos1.shape
    smem_row = pl.BlockSpec((1, 1, rows), lambda i, lt, nr: (i, 0, 0), memory_space=pltpu.SMEM)
    return pl.pallas_call(
        _dispatch_kernel,
        grid_spec=pltpu.PrefetchScalarGridSpec(
            num_scalar_prefetch=2,
            grid=(n_steps,),
            in_specs=[smem_row, smem_row,
                      pl.BlockSpec((rows * TOKEN_SUBLANES, LANES), lambda i, lt, nr: (i, 0))],
            out_specs=pl.BlockSpec(memory_space=pl.ANY),
            scratch_shapes=[pltpu.VMEM((EXPERT_ROWS * TOKEN_SUBLANES, LANES), F32),
                            pltpu.SemaphoreType.DMA(())],
        ),
        out_shape=jax.ShapeDtypeStruct((n_sorted_rows * TOKEN_SUBLANES, LANES), F32),
        compiler_params=pltpu.CompilerParams(dimension_semantics=("arbitrary",)),
        name="moe_dispatch",
    )(last_tile, n_real, pos1, pos2, xn)


EXPERT_IN_SLOTS = 3
EXPERT_OUT_SLOTS = 2


def _expert_kernel(first_ref, end_ref, nreal_ref, xs_hbm, w1_ref, w3_ref, w2_ref, ys_hbm,
                   w1_scr, w3_scr, w2_scr, xbuf, ybuf, in_sems, out_sems):
    e = pl.program_id(0)
    tile_rows = EXPERT_ROWS * TOKEN_SUBLANES
    n_real = nreal_ref[0]
    n_tiles = xs_hbm.shape[0] // tile_rows

    def tile_of(ref, t):
        return ref.at[pl.ds(pl.multiple_of(t * tile_rows, tile_rows), tile_rows)]

    def in_copy(t):
        slot = t % EXPERT_IN_SLOTS
        return pltpu.make_async_copy(tile_of(xs_hbm, t), xbuf.at[slot], in_sems.at[slot])

    def out_copy(t):
        slot = t % EXPERT_OUT_SLOTS
        return pltpu.make_async_copy(ybuf.at[slot], tile_of(ys_hbm, t), out_sems.at[slot])

    @pl.when(e == 0)
    def _():
        for t in range(EXPERT_IN_SLOTS - 1):
            @pl.when(t < n_real)
            def _():
                in_copy(t).start()

    w1_scr[...] = w1_ref[...].astype(BF16)
    w3_scr[...] = w3_ref[...].astype(BF16)
    w2_scr[...] = w2_ref[...].astype(BF16)

    def tile(t, carry):
        ahead = t + EXPERT_IN_SLOTS - 1

        @pl.when(ahead < n_real)
        def _():
            in_copy(ahead).start()

        in_copy(t).wait()

        @pl.when(t >= EXPERT_OUT_SLOTS)
        def _():
            out_copy(t - EXPERT_OUT_SLOTS).wait()

        x = _load_token_major(xbuf.at[t % EXPERT_IN_SLOTS], EXPERT_ROWS).astype(BF16)
        a = _dot(x, w1_scr[...])
        b = _dot(x, w3_scr[...])
        hid = (a * _sigmoid(a)) * b
        _store_token_major(ybuf.at[t % EXPERT_OUT_SLOTS], _dot(hid.astype(BF16), w2_scr[...]))
        out_copy(t).start()
        return carry

    lax.fori_loop(first_ref[e], end_ref[e], tile, 0)

    @pl.when(e == pl.num_programs(0) - 1)
    def _():
        for back in range(EXPERT_OUT_SLOTS, 0, -1):
            @pl.when(n_real - back >= 0)
            def _():
                out_copy(n_real - back).wait()
        ybuf[0] = jnp.zeros(ybuf.shape[1:], F32)

        def tail_start(t, carry):
            pltpu.make_async_copy(ybuf.at[0], tile_of(ys_hbm, t), out_sems.at[0]).start()
            return carry

        def tail_wait(t, carry):
            pltpu.make_async_copy(ybuf.at[0], tile_of(ys_hbm, t), out_sems.at[0]).wait()
            return carry

        lax.fori_loop(n_real, n_tiles, tail_start, 0)
        lax.fori_loop(n_real, n_tiles, tail_wait, 0)


def _expert_call(first_tile, end_tile, n_real, xs, w1, w3, w2):
    n_experts, d, d_expert = w1.shape
    tile_rows = EXPERT_ROWS * TOKEN_SUBLANES
    per_expert = lambda e, f, n, nr: (e, 0, 0)
    return pl.pallas_call(
        _expert_kernel,
        grid_spec=pltpu.PrefetchScalarGridSpec(
            num_scalar_prefetch=3,
            grid=(n_experts,),
            in_specs=[pl.BlockSpec(memory_space=pl.ANY),
                      pl.BlockSpec((None, d, d_expert), per_expert),
                      pl.BlockSpec((None, d, d_expert), per_expert),
                      pl.BlockSpec((None, d_expert, d), per_expert)],
            out_specs=pl.BlockSpec(memory_space=pl.ANY),
            scratch_shapes=[pltpu.VMEM((d, d_expert), BF16), pltpu.VMEM((d, d_expert), BF16),
                            pltpu.VMEM((d_expert, d), BF16),
                            pltpu.VMEM((EXPERT_IN_SLOTS, tile_rows, LANES), F32),
                            pltpu.VMEM((EXPERT_OUT_SLOTS, tile_rows, LANES), F32),
                            pltpu.SemaphoreType.DMA((EXPERT_IN_SLOTS,)),
                            pltpu.SemaphoreType.DMA((EXPERT_OUT_SLOTS,))],
        ),
        out_shape=jax.ShapeDtypeStruct(xs.shape, F32),
        compiler_params=pltpu.CompilerParams(
            dimension_semantics=("arbitrary",), vmem_limit_bytes=VMEM_LIMIT_BYTES),
        name="moe_experts",
    )(first_tile, end_tile, n_real, xs, w1, w3, w2)


def _combine_kernel(p1_first, p2_first, p1_next, p2_next, ys_hbm, h_ref, route_ref, ng_ref,
                    out_ref, buf, sems):
    i = pl.program_id(0)
    n = pl.num_programs(0)
    rows = h_ref.shape[0]

    def issue(p1_ref, p2_ref, slot):
        def body(g, carry):
            for u in range(DMA_UNROLL):
                r = g * DMA_UNROLL + u
                for which, p_ref in ((0, p1_ref), (1, p2_ref)):
                    pltpu.make_async_copy(_token_rows(ys_hbm, p_ref[0, 0, r]),
                                          _token_rows(buf.at[2 * slot + which], r),
                                          sems.at[slot]).start(which)
            return carry
        lax.fori_loop(0, rows // DMA_UNROLL, body, 0)

    @pl.when(i == 0)
    def _():
        issue(p1_first, p2_first, 0)

    @pl.when(i + 1 < n)
    def _():
        issue(p1_next, p2_next, (i + 1) % 2)

    slot = i % 2
    for which in range(2):
        pltpu.make_async_copy(ys_hbm.at[pl.ds(0, rows * TOKEN_SUBLANES)],
                              buf.at[2 * slot + which], sems.at[slot]).wait()

    route = route_ref[...]
    w1 = route[:, ROUTE_W1:ROUTE_W1 + 1]
    w2 = route[:, ROUTE_W2:ROUTE_W2 + 1]
    y = (h_ref[...] + w1 * _load_token_major(buf.at[2 * slot], rows)
         + w2 * _load_token_major(buf.at[2 * slot + 1], rows))
    out_ref[...] = _rmsnorm(y, ng_ref[...])


def _combine_call(pos1, pos2, ys, h, route, norm_g):
    t, d = h.shape
    n_steps, _, rows = pos1.shape
    row_spec = pl.BlockSpec((rows, d), lambda i: (i, 0))
    first = pl.BlockSpec((1, 1, rows), lambda i: (0, 0, 0), memory_space=pltpu.SMEM)
    nxt = pl.BlockSpec((1, 1, rows), lambda i: (jnp.minimum(i + 1, n_steps - 1), 0, 0),
                       memory_space=pltpu.SMEM)
    return pl.pallas_call(
        _combine_kernel,
        grid=(n_steps,),
        in_specs=[first, first, nxt, nxt,
                  pl.BlockSpec(memory_space=pl.ANY),
                  row_spec,
                  pl.BlockSpec((rows, ROUTER_LANES), lambda i: (i, 0)),
                  pl.BlockSpec((1, d), lambda i: (0, 0))],
        out_specs=row_spec,
        out_shape=jax.ShapeDtypeStruct((t, d), F32),
        scratch_shapes=[pltpu.VMEM((4, rows * TOKEN_SUBLANES, LANES), F32),
                        pltpu.SemaphoreType.DMA((2,))],
        compiler_params=pltpu.CompilerParams(
            dimension_semantics=("arbitrary",), vmem_limit_bytes=VMEM_LIMIT_BYTES),
        name="moe_combine",
    )(pos1, pos2, pos1, pos2, ys, h, route, norm_g)


def _sparse_moe(xn, route, route_t, counts, h, w1, w3, w2, norm_g):
    t = h.shape[0]
    tile = EXPERT_ROWS
    n_tiles = (2 * t) // tile + N_EXPERTS
    expert = jnp.arange(N_EXPERTS, dtype=jnp.int32)
    counts = counts[:N_EXPERTS, 0].astype(jnp.int32)
    group_tiles = (counts + tile - 1) // tile
    end_tile = jnp.sum(jnp.where(expert[None, :] <= expert[:, None], group_tiles[None, :], 0), axis=1)
    first_tile = end_tile - group_tiles
    n_real = end_tile[-1:]
    last_tile = jnp.where(group_tiles > 0, end_tile - 1, -1)

    def positions(e_row, r_row):
        e = route_t[:, e_row, :].astype(jnp.int32)
        start = jnp.zeros_like(e)
        for k in range(N_EXPERTS):
            start = jnp.where(e == k, first_tile[k] * tile, start)
        return start + route_t[:, r_row, :].astype(jnp.int32)

    pos1 = positions(ROUTE_E1, ROUTE_R1)
    pos2 = positions(ROUTE_E2, ROUTE_R2)
    per_step = lambda pos, rows: pos.reshape(t // rows, 1, rows)
    xs = _dispatch_call(last_tile, n_real, per_step(pos1, DISPATCH_ROWS),
                        per_step(pos2, DISPATCH_ROWS), xn, n_tiles * tile)
    ys = _expert_call(first_tile, end_tile, n_real, xs, w1, w3, w2)
    return _combine_call(per_step(pos1, COMBINE_ROWS), per_step(pos2, COMBINE_ROWS), ys, h,
                         route, norm_g)


def _layer(h, norm_mix_g, w_in, b_gates, gmlp_ln_g, gmlp_ln_b, w_spatial, b_spatial, bias_tiles,
           w_out, norm_ffn_g, w_group_router, b_group_router, w_expert_router, b_expert_router,
           w1, w3, w2, norm_out_g):
    b, s, d = h.shape
    t = b * s
    nb = s // MOBA_BLOCK
    x2 = h.reshape(t, d)
    row = lambda v: v.reshape(1, -1)

    ya, qT, k, vT, gb, kmean = _proj_call(
        x2, b, row(norm_mix_g), w_in.astype(BF16), row(b_gates), row(gmlp_ln_g), row(gmlp_ln_b),
        w_spatial, b_spatial[:, :, None])

    km = jnp.transpose(kmean.reshape(b, nb, ATT_HEADS, HEAD_DIM), (0, 2, 1, 3))
    km_hi = km.astype(BF16)
    km_lo = (km - km_hi.astype(F32)).astype(BF16)
    yb = _attn_call(qT, k, vT, km_hi, km_lo, bias_tiles)

    w_router = jnp.concatenate(
        [jnp.transpose(w_expert_router, (0, 2, 1)).reshape(N_EXPERTS, d), w_group_router.T,
         jnp.zeros((ROUTER_UNITS - N_EXPERTS - N_GROUPS, d), F32)], axis=0)
    b_router = jnp.concatenate(
        [b_expert_router.reshape(-1), b_group_router,
         jnp.zeros((ROUTER_UNITS - N_EXPERTS - N_GROUPS,), F32)]).reshape(ROUTER_UNITS, 1)
    wr_hi = w_router.astype(BF16)
    wr_lo = (w_router - wr_hi.astype(F32)).astype(BF16)
    h2, xn, route, route_t, counts = _merge_call(
        x2, ya, gb, yb, w_out.astype(BF16), row(norm_ffn_g), wr_hi, wr_lo, b_router)

    out = _sparse_moe(xn, route, route_t, counts, h2, w1, w3, w2, row(norm_out_g))
    return out.reshape(b, s, d)


def kernel(x, norm_mix_g, w_in, b_gates, gmlp_ln_g, gmlp_ln_b, w_spatial, b_spatial, rel_bias, w_out, norm_ffn_g, w_group_router, b_group_router, w_expert_router, b_expert_router, w1, w3, w2, norm_final_g):
    depth = w_in.shape[0]
    assert depth == 1, "the final rmsnorm is fused into the last layer's combine kernel"
    bias_tiles = _bias_call(rel_bias)
    return _layer(x, norm_mix_g[0], w_in[0], b_gates[0], gmlp_ln_g[0], gmlp_ln_b[0], w_spatial[0],
                  b_spatial[0], bias_tiles, w_out[0], norm_ffn_g[0], w_group_router[0],
                  b_group_router[0], w_expert_router[0], b_expert_router[0], w1[0], w3[0], w2[0],
                  norm_final_g)
```

```python
import functools
import math

import numpy as np
import jax
import jax.numpy as jnp
from jax import lax
from jax.experimental import pallas as pl
from jax.experimental.pallas import tpu as pltpu

F32 = jnp.float32
BF16 = jnp.bfloat16

D_MODEL = 1024
NORM_EPS = 1e-6
GMLP_GROUPS = 8
GMLP_CHUNK = 128
ATT_HEADS = 8
HEAD_DIM = 128
MOBA_BLOCK = 256
MOBA_TOPK = 3
REL_BUCKETS = 32
REL_MAX_DIST = 128
N_GROUPS = 4
EXPERTS_PER_GROUP = 8
N_EXPERTS = N_GROUPS * EXPERTS_PER_GROUP
D_EXPERT = 256
N_SEGMENTS = 7

LANES = 128
TOKEN_SUBLANES = 8
ROUTE_ROWS = 8
VMEM_LIMIT_BYTES = 56 * 1024 * 1024

SQRT_HALF = math.sqrt(0.5)
LOG2E = math.log2(math.e)
SCORE_SCALE2 = (HEAD_DIM ** -0.5) * LOG2E
MASK_NEG = -(2.0 ** 100)
BF16_SUBLANES = 16
V_ROWS = HEAD_DIM + BF16_SUBLANES
ROUTER_LANES = LANES
ROUTER_UNITS = -(-(N_EXPERTS + N_GROUPS) // BF16_SUBLANES) * BF16_SUBLANES
GROUP_UNIT0 = N_EXPERTS

PROJ_ROWS = 512
MERGE_ROWS = 1024
EXPERT_ROWS = 256
DISPATCH_ROWS = 2048
COMBINE_ROWS = 256
DMA_UNROLL = 8


def _rmsnorm(x, g):
    return x * lax.rsqrt(jnp.mean(x * x, axis=-1, keepdims=True) + NORM_EPS) * g


def _gelu(a):
    return 0.5 * a * (1.0 + lax.erf(a * SQRT_HALF))


def _sigmoid(a):
    return 1.0 / (1.0 + jnp.exp(-a))


def _dot(a, b):
    return jnp.dot(a, b, preferred_element_type=F32)


def _proj_kernel(x_ref, ng_ref, w_ref, bg_ref, lng_ref, lnb_ref, ws_ref, bs_ref,
                 ya_ref, qT_ref, k_ref, vT_ref, gb_ref, kmean_ref,
                 xn_scr, vln_scr, mix_scr):
    rows = x_ref.shape[0]
    d = D_MODEL
    xn_scr[...] = _rmsnorm(x_ref[...], ng_ref[...]).astype(BF16)

    def seg(i):
        return _dot(xn_scr[...], w_ref[:, i * d:(i + 1) * d])

    hd = HEAD_DIM

    v = _gelu(seg(1))
    mu = jnp.mean(v, axis=-1, keepdims=True)
    vc = v - mu
    var = jnp.mean(vc * vc, axis=-1, keepdims=True)
    vln_scr[...] = (vc * lax.rsqrt(var + NORM_EPS) * lng_ref[...] + lnb_ref[...]).astype(BF16)

    mix_scr[...] = _gelu(seg(0)) * _sigmoid(seg(5) + bg_ref[:, :d])

    q = seg(2) * SCORE_SCALE2
    for h in range(ATT_HEADS):
        qT_ref[h] = q[:, h * hd:(h + 1) * hd].T.astype(BF16)

    t_idx = lax.broadcasted_iota(jnp.int32, (GMLP_CHUNK, GMLP_CHUNK), 0)
    s_idx = lax.broadcasted_iota(jnp.int32, (GMLP_CHUNK, GMLP_CHUNK), 1)
    causal = t_idx >= s_idx
    gd = d // GMLP_GROUPS
    n_chunks = rows // GMLP_CHUNK
    for g in range(GMLP_GROUPS):
        ws = jnp.where(causal, ws_ref[g], 0.0).astype(BF16)
        bias = bs_ref[g]
        vg = jnp.concatenate(
            [vln_scr[c * GMLP_CHUNK:(c + 1) * GMLP_CHUNK, g * gd:(g + 1) * gd]
             for c in range(n_chunks)], axis=1)
        mixed = _dot(ws, vg)
        for c in range(n_chunks):
            blk_rows = slice(c * GMLP_CHUNK, (c + 1) * GMLP_CHUNK)
            blk_cols = slice(g * gd, (g + 1) * gd)
            ya_ref[blk_rows, blk_cols] = (
                mix_scr[blk_rows, blk_cols] * (mixed[:, c * gd:(c + 1) * gd] + bias)).astype(BF16)

    v = seg(4)
    for blk in range(rows // MOBA_BLOCK):
        r0 = blk * MOBA_BLOCK
        for h in range(ATT_HEADS):
            vT_ref[h, blk, :hd, :] = v[r0:r0 + MOBA_BLOCK, h * hd:(h + 1) * hd].T.astype(BF16)
            vT_ref[h, blk, hd:, :] = jnp.ones((V_ROWS - hd, MOBA_BLOCK), BF16)

    gb_ref[...] = _sigmoid(seg(6) + bg_ref[:, d:]).astype(BF16)

    k = seg(3)
    k_ref[...] = k.astype(BF16)
    for blk in range(rows // MOBA_BLOCK):
        r0 = blk * MOBA_BLOCK
        kmean_ref[0, blk:blk + 1, :] = jnp.mean(k[r0:r0 + MOBA_BLOCK, :], axis=0, keepdims=True)


def _proj_call(x2, batch, norm_g, w_in, b_gates, ln_g, ln_b, w_spatial, b_spatial):
    t, d = x2.shape
    rows = PROJ_ROWS
    seq = t // batch
    assert seq % rows == 0 and rows % MOBA_BLOCK == 0 and rows % GMLP_CHUNK == 0
    n_tiles = t // rows
    tiles_per_seq = seq // rows
    blocks_per_tile = rows // MOBA_BLOCK
    nb = seq // MOBA_BLOCK
    row_spec = pl.BlockSpec((rows, d), lambda i: (i, 0))
    const2 = lambda i: (0, 0)
    const3 = lambda i: (0, 0, 0)
    act = jax.ShapeDtypeStruct((t, d), BF16)
    qT_spec = pl.BlockSpec((None, ATT_HEADS, HEAD_DIM, rows),
                           lambda i: (i // tiles_per_seq, 0, 0, i % tiles_per_seq))
    vT_spec = pl.BlockSpec((None, ATT_HEADS, blocks_per_tile, V_ROWS, MOBA_BLOCK),
                           lambda i: (i // tiles_per_seq, 0, i % tiles_per_seq, 0, 0))
    return pl.pallas_call(
        _proj_kernel,
        grid=(n_tiles,),
        in_specs=[
            row_spec,
            pl.BlockSpec((1, d), const2),
            pl.BlockSpec((d, N_SEGMENTS * d), const2, pipeline_mode=pl.Buffered(1)),
            pl.BlockSpec((1, 2 * d), const2),
            pl.BlockSpec((1, d), const2),
            pl.BlockSpec((1, d), const2),
            pl.BlockSpec((GMLP_GROUPS, GMLP_CHUNK, GMLP_CHUNK), const3),
            pl.BlockSpec((GMLP_GROUPS, GMLP_CHUNK, 1), const3),
        ],
        out_specs=[row_spec, qT_spec, row_spec, vT_spec, row_spec,
                   pl.BlockSpec((1, blocks_per_tile, d), lambda i: (i, 0, 0))],
        out_shape=[act,
                   jax.ShapeDtypeStruct((batch, ATT_HEADS, HEAD_DIM, seq), BF16),
                   act,
                   jax.ShapeDtypeStruct((batch, ATT_HEADS, nb, V_ROWS, MOBA_BLOCK), BF16),
                   act,
                   jax.ShapeDtypeStruct((n_tiles, blocks_per_tile, d), F32)],
        scratch_shapes=[pltpu.VMEM((rows, d), BF16), pltpu.VMEM((rows, d), BF16),
                        pltpu.VMEM((rows, d), F32)],
        compiler_params=pltpu.CompilerParams(
            dimension_semantics=("arbitrary",), vmem_limit_bytes=VMEM_LIMIT_BYTES),
        name="proj_gmlp",
    )(x2, norm_g, w_in, b_gates, ln_g, ln_b, w_spatial, b_spatial)


def _t5_bucket_np(n):
    n = np.maximum(n, 0)
    max_exact = REL_BUCKETS // 2
    nf = np.maximum(n, max_exact).astype(np.float32)
    large = max_exact + (np.log(nf / max_exact) / math.log(REL_MAX_DIST / max_exact)
                         * (REL_BUCKETS - max_exact)).astype(np.int32)
    large = np.minimum(large, REL_BUCKETS - 1)
    return np.where(n < max_exact, n, large).astype(np.int32)


def _bucket_tiles():
    kpos = np.arange(MOBA_BLOCK, dtype=np.int32)[:, None]
    qpos = np.arange(MOBA_BLOCK, dtype=np.int32)[None, :]
    rel = qpos - kpos
    return np.stack([_t5_bucket_np(rel), _t5_bucket_np(rel + MOBA_BLOCK)])


def _bias_kernel(relb_ref, bucket_ref, out_ref):
    h = pl.program_id(0)
    far = relb_ref[REL_BUCKETS - 1, h]
    k_idx = lax.broadcasted_iota(jnp.int32, (MOBA_BLOCK, MOBA_BLOCK), 0)
    q_idx = lax.broadcasted_iota(jnp.int32, (MOBA_BLOCK, MOBA_BLOCK), 1)
    for tile in range(2):
        bucket = bucket_ref[tile]
        bias = jnp.zeros((MOBA_BLOCK, MOBA_BLOCK), F32)
        for b in range(REL_BUCKETS):
            bias = jnp.where(bucket == b, relb_ref[b, h], bias)
        bias2 = (bias - far) * LOG2E
        if tile == BIAS_OWN:
            bias2 = jnp.where(q_idx >= k_idx, bias2, MASK_NEG)
        out_ref[0, tile] = bias2


def _bias_call(rel_bias):
    buckets = _bucket_tiles()
    return pl.pallas_call(
        _bias_kernel,
        grid=(ATT_HEADS,),
        in_specs=[pl.BlockSpec(memory_space=pltpu.SMEM),
                  pl.BlockSpec((2, MOBA_BLOCK, MOBA_BLOCK), lambda h: (0, 0, 0))],
        out_specs=pl.BlockSpec((1, N_BIAS_TILES, MOBA_BLOCK, MOBA_BLOCK), lambda h: (h, 0, 0, 0)),
        out_shape=jax.ShapeDtypeStruct((ATT_HEADS, N_BIAS_TILES, MOBA_BLOCK, MOBA_BLOCK), F32),
        compiler_params=pltpu.CompilerParams(dimension_semantics=("arbitrary",)),
        name="t5_bias_tiles",
    )(rel_bias, jnp.asarray(buckets))


BIAS_OWN, BIAS_PREV = range(2)
N_BIAS_TILES = 2


def _attn_kernel(qT_ref, k_ref, vT_ref, kmh_ref, kml_ref, oh_ref, bias_ref, o_ref,
                 qp_scr, m_scr, acc_scr):
    i = pl.program_id(1)
    nh, nb = kmh_ref.shape[0], kmh_ref.shape[1]
    blk = MOBA_BLOCK
    hd = HEAD_DIM

    n_idx = lax.broadcasted_iota(jnp.int32, (nb, blk), 0)
    past = n_idx < i
    for h in range(nh):
        qT = qT_ref[h]
        gate = _dot(kmh_ref[h], qT) + _dot(kml_ref[h], qT)
        gate = jnp.where(past, gate, -jnp.inf)
        rank = jnp.zeros((nb, blk), F32)
        for m in range(nb):
            row = gate[m:m + 1, :]
            beats = (row > gate) | ((row == gate) & (m < n_idx))
            rank = rank + jnp.where(beats, 1.0, 0.0)
        keep = (past & (rank < MOBA_TOPK)) | (n_idx == i)
        sel = jnp.where(keep, 0.0, MASK_NEG)
        sel = jnp.concatenate([sel, jnp.zeros((hd - nb, blk), F32)], axis=0)
        qp_scr[h] = jnp.concatenate([qT, sel.astype(BF16)], axis=0)

    def scores(h, j):
        r0 = pl.multiple_of(j * blk, blk)
        kp = jnp.concatenate([k_ref[pl.ds(r0, blk), h * hd:(h + 1) * hd], oh_ref[j]], axis=1)
        return _dot(kp, qp_scr[h])

    def fold(blocks, bias_tiles, first):
        ss = []
        for h in range(nh):
            parts = []
            for j, tile in zip(blocks, bias_tiles):
                s = scores(h, j)
                parts.append(s if tile is None else s + bias_ref[h, tile])
            ss.append(parts)
        ps, alphas = [], []
        for h in range(nh):
            m_new = functools.reduce(
                jnp.maximum, [jnp.max(s, axis=0, keepdims=True) for s in ss[h]])
            if not first:
                m_old = m_scr[h]
                m_new = jnp.maximum(m_old, m_new)
                alphas.append(jnp.exp2(m_old - m_new))
            m_scr[h] = m_new
            ps.append([jnp.exp2(s - m_new).astype(BF16) for s in ss[h]])
        for h in range(nh):
            pv = functools.reduce(
                lambda a, b: a + b, [_dot(vT_ref[h, j], x) for j, x in zip(blocks, ps[h])])
            acc_scr[h] = pv if first else alphas[h] * acc_scr[h] + pv

    @pl.when(i == 0)
    def _():
        fold([i], [BIAS_OWN], first=True)

    @pl.when(i >= 1)
    def _():
        fold([i, i - 1], [BIAS_OWN, BIAS_PREV], first=True)

    n_far = i - 1

    def far_pair(p, carry):
        fold([2 * p, 2 * p + 1], [None, None], first=False)
        return carry

    lax.fori_loop(0, n_far // 2, far_pair, 0)

    @pl.when((n_far >= 1) & (n_far % 2 == 1))
    def _():
        fold([n_far - 1], [None], first=False)

    for h in range(nh):
        y = acc_scr[h, :hd, :] * (1.0 / acc_scr[h, hd:hd + 1, :])
        o_ref[:, h * hd:(h + 1) * hd] = y.T.astype(o_ref.dtype)


def _attn_call(qT, k2, vT, km_hi, km_lo, bias_tiles):
    b, h, nb, v_rows, blk = vT.shape
    hd = HEAD_DIM
    s = nb * blk
    onehot = np.zeros((nb, blk, LANES), np.float32)
    for j in range(nb):
        onehot[j, :, j] = 1.0
    once = pl.Buffered(1)
    return pl.pallas_call(
        _attn_kernel,
        grid=(b, nb),
        in_specs=[
            pl.BlockSpec((None, h, hd, blk), lambda bi, i: (bi, 0, 0, i)),
            pl.BlockSpec((s, h * hd), lambda bi, i: (bi, 0)),
            pl.BlockSpec((None, h, nb, v_rows, blk), lambda bi, i: (bi, 0, 0, 0, 0)),
            pl.BlockSpec((None, h, nb, hd), lambda bi, i: (bi, 0, 0, 0)),
            pl.BlockSpec((None, h, nb, hd), lambda bi, i: (bi, 0, 0, 0)),
            pl.BlockSpec((nb, blk, LANES), lambda bi, i: (0, 0, 0), pipeline_mode=once),
            pl.BlockSpec((h, N_BIAS_TILES, blk, blk), lambda bi, i: (0, 0, 0, 0),
                         pipeline_mode=once),
        ],
        out_specs=pl.BlockSpec((blk, h * hd), lambda bi, i: (bi * nb + i, 0)),
        out_shape=jax.ShapeDtypeStruct((b * s, h * hd), BF16),
        scratch_shapes=[pltpu.VMEM((h, 2 * hd, blk), BF16), pltpu.VMEM((h, 1, blk), F32),
                        pltpu.VMEM((h, v_rows, blk), F32)],
        compiler_params=pltpu.CompilerParams(
            dimension_semantics=("arbitrary", "arbitrary"),
            vmem_limit_bytes=VMEM_LIMIT_BYTES),
        name="moba_attention",
    )(qT, k2, vT, km_hi, km_lo, jnp.asarray(onehot, BF16), bias_tiles)


ROUTE_E1, ROUTE_E2, ROUTE_W1, ROUTE_W2, ROUTE_R1, ROUTE_R2 = range(6)


def _dot_nt(a, b):
    return lax.dot_general(a, b, (((1,), (1,)), ((), ())), preferred_element_type=F32)


def _store_token_major(ref, x):
    rows = x.shape[0]
    for s in range(TOKEN_SUBLANES):
        ref[pl.ds(s, rows, stride=TOKEN_SUBLANES), :] = x[:, s * LANES:(s + 1) * LANES]


def _load_token_major(ref, rows):
    return jnp.concatenate(
        [ref[pl.ds(s, rows, stride=TOKEN_SUBLANES), :] for s in range(TOKEN_SUBLANES)], axis=1)


def _merge_kernel(x_ref, ya_ref, gb_ref, yb_ref, wo_ref, ng_ref, wrh_ref, wrl_ref, br_ref,
                  h_ref, xn_ref, route_ref, route_t_ref, counts_ref, run_scr):
    @pl.when(pl.program_id(0) == 0)
    def _():
        run_scr[...] = jnp.zeros_like(run_scr)

    f = lambda r: r[...].astype(F32)
    mix = (f(ya_ref) + f(gb_ref) * f(yb_ref)).astype(BF16)
    h = x_ref[...] + _dot(mix, wo_ref[...])
    h_ref[...] = h
    xn = _rmsnorm(h, ng_ref[...])
    _store_token_major(xn_ref, xn)
    rows = xn.shape[0]

    x_hi = xn.astype(BF16)
    x_lo = (xn - x_hi.astype(F32)).astype(BF16)
    logits = (_dot_nt(wrh_ref[...], x_hi) + _dot_nt(wrh_ref[...], x_lo)
              + _dot_nt(wrl_ref[...], x_hi) + br_ref[...])
    unit = lax.broadcasted_iota(jnp.int32, logits.shape, 0).astype(F32)
    big = float(ROUTER_UNITS)
    neg_inf = -jnp.inf

    gl = jnp.where((unit >= GROUP_UNIT0) & (unit < GROUP_UNIT0 + N_GROUPS), logits, neg_inf)
    gmax = jnp.max(gl, axis=0, keepdims=True)
    g_w = 1.0 / jnp.sum(jnp.exp(gl - gmax), axis=0, keepdims=True)
    g_idx = jnp.min(jnp.where(gl == gmax, unit, big), axis=0, keepdims=True) - GROUP_UNIT0

    e0 = g_idx * EXPERTS_PER_GROUP
    el = jnp.where((unit >= e0) & (unit < e0 + EXPERTS_PER_GROUP), logits, neg_inf)
    m1 = jnp.max(el, axis=0, keepdims=True)
    i1 = jnp.min(jnp.where(el == m1, unit, big), axis=0, keepdims=True)
    el2 = jnp.where(unit == i1, neg_inf, el)
    m2 = jnp.max(el2, axis=0, keepdims=True)
    i2 = jnp.min(jnp.where(el2 == m2, unit, big), axis=0, keepdims=True)
    e2 = jnp.exp(m2 - m1)
    den = 1.0 + e2
    w1 = (1.0 / den) * g_w
    w2 = (e2 / den) * g_w

    hit1 = unit == i1
    hit2 = unit == i2
    onehot = jnp.where(hit1, 1.0, jnp.where(hit2, 1.0, 0.0))
    c_idx = lax.broadcasted_iota(jnp.int32, (rows, rows), 0)
    r_idx = lax.broadcasted_iota(jnp.int32, (rows, rows), 1)
    earlier = jnp.where(c_idx < r_idx, 1.0, 0.0).astype(BF16)
    prefix = run_scr[...] + _dot(onehot.astype(BF16), earlier)
    rank1 = jnp.sum(jnp.where(hit1, prefix, 0.0), axis=0, keepdims=True)
    rank2 = jnp.sum(jnp.where(hit2, prefix, 0.0), axis=0, keepdims=True)
    run_scr[...] = run_scr[...] + jnp.sum(onehot, axis=1, keepdims=True)
    counts_ref[...] = run_scr[...]

    route_t = jnp.concatenate(
        [i1, i2, w1, w2, rank1, rank2, jnp.zeros((ROUTER_LANES - 6, rows), F32)], axis=0)
    route_t_ref[0] = route_t[:ROUTE_ROWS]
    route_ref[...] = route_t.T


def _merge_call(x2, ya, gb, yb, w_out, norm_g, wr_hi, wr_lo, b_router):
    t, d = x2.shape
    rows = MERGE_ROWS
    assert t % rows == 0 and d == TOKEN_SUBLANES * LANES
    n_tiles = t // rows
    row_spec = pl.BlockSpec((rows, d), lambda i: (i, 0))
    const2 = lambda i: (0, 0)
    return pl.pallas_call(
        _merge_kernel,
        grid=(n_tiles,),
        in_specs=[row_spec, row_spec, row_spec, row_spec,
                  pl.BlockSpec((d, d), const2),
                  pl.BlockSpec((1, d), const2),
                  pl.BlockSpec((ROUTER_UNITS, d), const2),
                  pl.BlockSpec((ROUTER_UNITS, d), const2),
                  pl.BlockSpec((ROUTER_UNITS, 1), const2)],
        out_specs=[row_spec,
                   pl.BlockSpec((rows * TOKEN_SUBLANES, LANES), lambda i: (i, 0)),
                   pl.BlockSpec((rows, ROUTER_LANES), lambda i: (i, 0)),
                   pl.BlockSpec((1, ROUTE_ROWS, rows), lambda i: (i, 0, 0)),
                   pl.BlockSpec((ROUTER_UNITS, 1), const2)],
        out_shape=[jax.ShapeDtypeStruct((t, d), F32),
                   jax.ShapeDtypeStruct((t * TOKEN_SUBLANES, LANES), F32),
                   jax.ShapeDtypeStruct((t, ROUTER_LANES), F32),
                   jax.ShapeDtypeStruct((n_tiles, ROUTE_ROWS, rows), F32),
                   jax.ShapeDtypeStruct((ROUTER_UNITS, 1), F32)],
        scratch_shapes=[pltpu.VMEM((ROUTER_UNITS, 1), F32)],
        compiler_params=pltpu.CompilerParams(
            dimension_semantics=("arbitrary",), vmem_limit_bytes=VMEM_LIMIT_BYTES),
        name="merge_outproj_router",
    )(x2, ya, gb, yb, w_out, norm_g, wr_hi, wr_lo, b_router)


def _token_rows(ref, token):
    return ref.at[pl.ds(pl.multiple_of(token * TOKEN_SUBLANES, TOKEN_SUBLANES), TOKEN_SUBLANES)]


def _dispatch_kernel(last_ref, nreal_ref, pos1_ref, pos2_ref, xn_ref, xs_hbm, zero_scr, sem):
    rows = pos1_ref.shape[2]
    tile = EXPERT_ROWS
    n_tiles = xs_hbm.shape[0] // (tile * TOKEN_SUBLANES)

    def zero_tile(j):
        start = pl.multiple_of(j * (tile * TOKEN_SUBLANES), tile * TOKEN_SUBLANES)
        return pltpu.make_async_copy(
            zero_scr, xs_hbm.at[pl.ds(start, tile * TOKEN_SUBLANES)], sem)

    @pl.when(pl.program_id(0) == 0)
    def _():
        zero_scr[...] = jnp.zeros_like(zero_scr)
        for e in range(N_EXPERTS):
            @pl.when(last_ref[e] >= 0)
            def _():
                zero_tile(last_ref[e]).start()

        def tail_start(j, carry):
            zero_tile(j).start()
            return carry

        lax.fori_loop(nreal_ref[0], n_tiles, tail_start, 0)

        for e in range(N_EXPERTS):
            @pl.when(last_ref[e] >= 0)
            def _():
                zero_tile(0).wait()

        def tail_wait(j, carry):
            zero_tile(0).wait()
            return carry

        lax.fori_loop(nreal_ref[0], n_tiles, tail_wait, 0)

    def issue(g, carry):
        for u in range(DMA_UNROLL):
            r = g * DMA_UNROLL + u
            src = _token_rows(xn_ref, r)
            pltpu.make_async_copy(src, _token_rows(xs_hbm, pos1_ref[0, 0, r]), sem).start(0)
            pltpu.make_async_copy(src, _token_rows(xs_hbm, pos2_ref[0, 0, r]), sem).start(1)
        return carry

    lax.fori_loop(0, rows // DMA_UNROLL, issue, 0)

    for _ in range(2):
        pltpu.make_async_copy(xn_ref, xs_hbm.at[pl.ds(0, rows * TOKEN_SUBLANES)], sem).wait()


def _dispatch_call(last_tile, n_real, pos1, pos2, xn, n_sorted_rows):
    n_steps, _, rows = pos1.shape
    smem_row = pl.BlockSpec((1, 1, rows), lambda i, lt, nr: (i, 0, 0), memory_space=pltpu.SMEM)
    return pl.pallas_call(
        _dispatch_kernel,
        grid_spec=pltpu.PrefetchScalarGridSpec(
            num_scalar_prefetch=2,
            grid=(n_steps,),
            in_specs=[smem_row, smem_row,
                      pl.BlockSpec((rows * TOKEN_SUBLANES, LANES), lambda i, lt, nr: (i, 0))],
            out_specs=pl.BlockSpec(memory_space=pl.ANY),
            scratch_shapes=[pltpu.VMEM((EXPERT_ROWS * TOKEN_SUBLANES, LANES), F32),
                            pltpu.SemaphoreType.DMA(())],
        ),
        out_shape=jax.ShapeDtypeStruct((n_sorted_rows * TOKEN_SUBLANES, LANES), F32),
        compiler_params=pltpu.CompilerParams(dimension_semantics=("arbitrary",)),
        name="moe_dispatch",
    )(last_tile, n_real, pos1, pos2, xn)


EXPERT_IN_SLOTS = 3
EXPERT_OUT_SLOTS = 2


def _expert_kernel(first_ref, end_ref, nreal_ref, xs_hbm, w1_ref, w3_ref, w2_ref, ys_hbm,
                   w1_scr, w3_scr, w2_scr, xbuf, ybuf, in_sems, out_sems):
    e = pl.program_id(0)
    tile_rows = EXPERT_ROWS * TOKEN_SUBLANES
    n_real = nreal_ref[0]
    n_tiles = xs_hbm.shape[0] // tile_rows

    def tile_of(ref, t):
        return ref.at[pl.ds(pl.multiple_of(t * tile_rows, tile_rows), tile_rows)]

    def in_copy(t):
        slot = t % EXPERT_IN_SLOTS
        return pltpu.make_async_copy(tile_of(xs_hbm, t), xbuf.at[slot], in_sems.at[slot])

    def out_copy(t):
        slot = t % EXPERT_OUT_SLOTS
        return pltpu.make_async_copy(ybuf.at[slot], tile_of(ys_hbm, t), out_sems.at[slot])

    @pl.when(e == 0)
    def _():
        for t in range(EXPERT_IN_SLOTS - 1):
            @pl.when(t < n_real)
            def _():
                in_copy(t).start()

    w1_scr[...] = w1_ref[...].astype(BF16)
    w3_scr[...] = w3_ref[...].astype(BF16)
    w2_scr[...] = w2_ref[...].astype(BF16)

    def tile(t, carry):
        ahead = t + EXPERT_IN_SLOTS - 1

        @pl.when(ahead < n_real)
        def _():
            in_copy(ahead).start()

        in_copy(t).wait()

        @pl.when(t >= EXPERT_OUT_SLOTS)
        def _():
            out_copy(t - EXPERT_OUT_SLOTS).wait()

        x = _load_token_major(xbuf.at[t % EXPERT_IN_SLOTS], EXPERT_ROWS).astype(BF16)
        a = _dot(x, w1_scr[...])
        b = _dot(x, w3_scr[...])
        hid = (a * _sigmoid(a)) * b
        _store_token_major(ybuf.at[t % EXPERT_OUT_SLOTS], _dot(hid.astype(BF16), w2_scr[...]))
        out_copy(t).start()
        return carry

    lax.fori_loop(first_ref[e], end_ref[e], tile, 0)

    @pl.when(e == pl.num_programs(0) - 1)
    def _():
        for back in range(EXPERT_OUT_SLOTS, 0, -1):
            @pl.when(n_real - back >= 0)
            def _():
                out_copy(n_real - back).wait()
        ybuf[0] = jnp.zeros(ybuf.shape[1:], F32)

        def tail_start(t, carry):
            pltpu.make_async_copy(ybuf.at[0], tile_of(ys_hbm, t), out_sems.at[0]).start()
            return carry

        def tail_wait(t, carry):
            pltpu.make_async_copy(ybuf.at[0], tile_of(ys_hbm, t), out_sems.at[0]).wait()
            return carry

        lax.fori_loop(n_real, n_tiles, tail_start, 0)
        lax.fori_loop(n_real, n_tiles, tail_wait, 0)


def _expert_call(first_tile, end_tile, n_real, xs, w1, w3, w2):
    n_experts, d, d_expert = w1.shape
    tile_rows = EXPERT_ROWS * TOKEN_SUBLANES
    per_expert = lambda e, f, n, nr: (e, 0, 0)
    return pl.pallas_call(
        _expert_kernel,
        grid_spec=pltpu.PrefetchScalarGridSpec(
            num_scalar_prefetch=3,
            grid=(n_experts,),
            in_specs=[pl.BlockSpec(memory_space=pl.ANY),
                      pl.BlockSpec((None, d, d_expert), per_expert),
                      pl.BlockSpec((None, d, d_expert), per_expert),
                      pl.BlockSpec((None, d_expert, d), per_expert)],
            out_specs=pl.BlockSpec(memory_space=pl.ANY),
            scratch_shapes=[pltpu.VMEM((d, d_expert), BF16), pltpu.VMEM((d, d_expert), BF16),
                            pltpu.VMEM((d_expert, d), BF16),
                            pltpu.VMEM((EXPERT_IN_SLOTS, tile_rows, LANES), F32),
                            pltpu.VMEM((EXPERT_OUT_SLOTS, tile_rows, LANES), F32),
                            pltpu.SemaphoreType.DMA((EXPERT_IN_SLOTS,)),
                            pltpu.SemaphoreType.DMA((EXPERT_OUT_SLOTS,))],
        ),
        out_shape=jax.ShapeDtypeStruct(xs.shape, F32),
        compiler_params=pltpu.CompilerParams(
            dimension_semantics=("arbitrary",), vmem_limit_bytes=VMEM_LIMIT_BYTES),
        name="moe_experts",
    )(first_tile, end_tile, n_real, xs, w1, w3, w2)


def _combine_kernel(p1_first, p2_first, p1_next, p2_next, ys_hbm, h_ref, route_ref, ng_ref,
                    out_ref, buf, sems):
    i = pl.program_id(0)
    n = pl.num_programs(0)
    rows = h_ref.shape[0]

    def issue(p1_ref, p2_ref, slot):
        def body(g, carry):
            for u in range(DMA_UNROLL):
                r = g * DMA_UNROLL + u
                for which, p_ref in ((0, p1_ref), (1, p2_ref)):
                    pltpu.make_async_copy(_token_rows(ys_hbm, p_ref[0, 0, r]),
                                          _token_rows(buf.at[2 * slot + which], r),
                                          sems.at[slot]).start(which)
            return carry
        lax.fori_loop(0, rows // DMA_UNROLL, body, 0)

    @pl.when(i == 0)
    def _():
        issue(p1_first, p2_first, 0)

    @pl.when(i + 1 < n)
    def _():
        issue(p1_next, p2_next, (i + 1) % 2)

    slot = i % 2
    for which in range(2):
        pltpu.make_async_copy(ys_hbm.at[pl.ds(0, rows * TOKEN_SUBLANES)],
                              buf.at[2 * slot + which], sems.at[slot]).wait()

    route = route_ref[...]
    w1 = route[:, ROUTE_W1:ROUTE_W1 + 1]
    w2 = route[:, ROUTE_W2:ROUTE_W2 + 1]
    y = (h_ref[...] + w1 * _load_token_major(buf.at[2 * slot], rows)
         + w2 * _load_token_major(buf.at[2 * slot + 1], rows))
    out_ref[...] = _rmsnorm(y, ng_ref[...])


def _combine_call(pos1, pos2, ys, h, route, norm_g):
    t, d = h.shape
    n_steps, _, rows = pos1.shape
    row_spec = pl.BlockSpec((rows, d), lambda i: (i, 0))
    first = pl.BlockSpec((1, 1, rows), lambda i: (0, 0, 0), memory_space=pltpu.SMEM)
    nxt = pl.BlockSpec((1, 1, rows), lambda i: (jnp.minimum(i + 1, n_steps - 1), 0, 0),
                       memory_space=pltpu.SMEM)
    return pl.pallas_call(
        _combine_kernel,
        grid=(n_steps,),
        in_specs=[first, first, nxt, nxt,
                  pl.BlockSpec(memory_space=pl.ANY),
                  row_spec,
                  pl.BlockSpec((rows, ROUTER_LANES), lambda i: (i, 0)),
                  pl.BlockSpec((1, d), lambda i: (0, 0))],
        out_specs=row_spec,
        out_shape=jax.ShapeDtypeStruct((t, d), F32),
        scratch_shapes=[pltpu.VMEM((4, rows * TOKEN_SUBLANES, LANES), F32),
                        pltpu.SemaphoreType.DMA((2,))],
        compiler_params=pltpu.CompilerParams(
            dimension_semantics=("arbitrary",), vmem_limit_bytes=VMEM_LIMIT_BYTES),
        name="moe_combine",
    )(pos1, pos2, pos1, pos2, ys, h, route, norm_g)


def _sparse_moe(xn, route, route_t, counts, h, w1, w3, w2, norm_g):
    t = h.shape[0]
    tile = EXPERT_ROWS
    n_tiles = (2 * t) // tile + N_EXPERTS
    expert = jnp.arange(N_EXPERTS, dtype=jnp.int32)
    counts = counts[:N_EXPERTS, 0].astype(jnp.int32)
    group_tiles = (counts + tile - 1) // tile
    end_tile = jnp.sum(jnp.where(expert[None, :] <= expert[:, None], group_tiles[None, :], 0), axis=1)
    first_tile = end_tile - group_tiles
    n_real = end_tile[-1:]
    last_tile = jnp.where(group_tiles > 0, end_tile - 1, -1)

    def positions(e_row, r_row):
        e = route_t[:, e_row, :].astype(jnp.int32)
        start = jnp.zeros_like(e)
        for k in range(N_EXPERTS):
            start = jnp.where(e == k, first_tile[k] * tile, start)
        return start + route_t[:, r_row, :].astype(jnp.int32)

    pos1 = positions(ROUTE_E1, ROUTE_R1)
    pos2 = positions(ROUTE_E2, ROUTE_R2)
    per_step = lambda pos, rows: pos.reshape(t // rows, 1, rows)
    xs = _dispatch_call(last_tile, n_real, per_step(pos1, DISPATCH_ROWS),
                        per_step(pos2, DISPATCH_ROWS), xn, n_tiles * tile)
    ys = _expert_call(first_tile, end_tile, n_real, xs, w1, w3, w2)
    return _combine_call(per_step(pos1, COMBINE_ROWS), per_step(pos2, COMBINE_ROWS), ys, h,
                         route, norm_g)


def _layer(h, norm_mix_g, w_in, b_gates, gmlp_ln_g, gmlp_ln_b, w_spatial, b_spatial, bias_tiles,
           w_out, norm_ffn_g, w_group_router, b_group_router, w_expert_router, b_expert_router,
           w1, w3, w2, norm_out_g):
    b, s, d = h.shape
    t = b * s
    nb = s // MOBA_BLOCK
    x2 = h.reshape(t, d)
    row = lambda v: v.reshape(1, -1)

    ya, qT, k, vT, gb, kmean = _proj_call(
        x2, b, row(norm_mix_g), w_in.astype(BF16), row(b_gates), row(gmlp_ln_g), row(gmlp_ln_b),
        w_spatial, b_spatial[:, :, None])

    km = jnp.transpose(kmean.reshape(b, nb, ATT_HEADS, HEAD_DIM), (0, 2, 1, 3))
    km_hi = km.astype(BF16)
    km_lo = (km - km_hi.astype(F32)).astype(BF16)
    yb = _attn_call(qT, k, vT, km_hi, km_lo, bias_tiles)

    w_router = jnp.concatenate(
        [jnp.transpose(w_expert_router, (0, 2, 1)).reshape(N_EXPERTS, d), w_group_router.T,
         jnp.zeros((ROUTER_UNITS - N_EXPERTS - N_GROUPS, d), F32)], axis=0)
    b_router = jnp.concatenate(
        [b_expert_router.reshape(-1), b_group_router,
         jnp.zeros((ROUTER_UNITS - N_EXPERTS - N_GROUPS,), F32)]).reshape(ROUTER_UNITS, 1)
    wr_hi = w_router.astype(BF16)
    wr_lo = (w_router - wr_hi.astype(F32)).astype(BF16)
    h2, xn, route, route_t, counts = _merge_call(
        x2, ya, gb, yb, w_out.astype(BF16), row(norm_ffn_g), wr_hi, wr_lo, b_router)

    out = _sparse_moe(xn, route, route_t, counts, h2, w1, w3, w2, row(norm_out_g))
    return out.reshape(b, s, d)


def kernel(x, norm_mix_g, w_in, b_gates, gmlp_ln_g, gmlp_ln_b, w_spatial, b_spatial, rel_bias, w_out, norm_ffn_g, w_group_router, b_group_router, w_expert_router, b_expert_router, w1, w3, w2, norm_final_g):
    depth = w_in.shape[0]
    assert depth == 1, "the final rmsnorm is fused into the last layer's combine kernel"
    bias_tiles = _bias_call(rel_bias)
    return _layer(x, norm_mix_g[0], w_in[0], b_gates[0], gmlp_ln_g[0], gmlp_ln_b[0], w_spatial[0],
                  b_spatial[0], bias_tiles, w_out[0], norm_ffn_g[0], w_group_router[0],
                  b_group_router[0], w_expert_router[0], b_expert_router[0], w1[0], w3[0], w2[0],
                  norm_final_g)
```

```python
import functools
import math

import numpy as np
import jax
import jax.numpy as jnp
from jax import lax
from jax.experimental import pallas as pl
from jax.experimental.pallas import tpu as pltpu

F32 = jnp.float32
BF16 = jnp.bfloat16

D_MODEL = 1024
NORM_EPS = 1e-6
GMLP_GROUPS = 8
GMLP_CHUNK = 128
ATT_HEADS = 8
HEAD_DIM = 128
MOBA_BLOCK = 256
MOBA_TOPK = 3
REL_BUCKETS = 32
REL_MAX_DIST = 128
N_GROUPS = 4
EXPERTS_PER_GROUP = 8
N_EXPERTS = N_GROUPS * EXPERTS_PER_GROUP
D_EXPERT = 256
N_SEGMENTS = 7

LANES = 128
TOKEN_SUBLANES = 8
ROUTE_ROWS = 8
VMEM_LIMIT_BYTES = 56 * 1024 * 1024

SQRT_HALF = math.sqrt(0.5)
LOG2E = math.log2(math.e)
SCORE_SCALE2 = (HEAD_DIM ** -0.5) * LOG2E
MASK_NEG = -(2.0 ** 100)
BF16_SUBLANES = 16
V_ROWS = HEAD_DIM + BF16_SUBLANES
ROUTER_LANES = LANES
ROUTER_UNITS = -(-(N_EXPERTS + N_GROUPS) // BF16_SUBLANES) * BF16_SUBLANES
GROUP_UNIT0 = N_EXPERTS

PROJ_ROWS = 512
MERGE_ROWS = 1024
EXPERT_ROWS = 256
DISPATCH_ROWS = 2048
COMBINE_ROWS = 256
DMA_UNROLL = 8


def _rmsnorm(x, g):
    return x * lax.rsqrt(jnp.mean(x * x, axis=-1, keepdims=True) + NORM_EPS) * g


def _gelu(a):
    return 0.5 * a * (1.0 + lax.erf(a * SQRT_HALF))


def _sigmoid(a):
    return 1.0 / (1.0 + jnp.exp(-a))


def _dot(a, b):
    return jnp.dot(a, b, preferred_element_type=F32)


def _proj_kernel(x_ref, ng_ref, w_ref, bg_ref, lng_ref, lnb_ref, ws_ref, bs_ref,
                 ya_ref, qT_ref, k_ref, vT_ref, gb_ref, kmean_ref,
                 xn_scr, vln_scr, mix_scr):
    rows = x_ref.shape[0]
    d = D_MODEL
    xn_scr[...] = _rmsnorm(x_ref[...], ng_ref[...]).astype(BF16)

    def seg(i):
        return _dot(xn_scr[...], w_ref[:, i * d:(i + 1) * d])

    hd = HEAD_DIM

    v = _gelu(seg(1))
    mu = jnp.mean(v, axis=-1, keepdims=True)
    vc = v - mu
    var = jnp.mean(vc * vc, axis=-1, keepdims=True)
    vln_scr[...] = (vc * lax.rsqrt(var + NORM_EPS) * lng_ref[...] + lnb_ref[...]).astype(BF16)

    mix_scr[...] = _gelu(seg(0)) * _sigmoid(seg(5) + bg_ref[:, :d])

    q = seg(2) * SCORE_SCALE2
    for h in range(ATT_HEADS):
        qT_ref[h] = q[:, h * hd:(h + 1) * hd].T.astype(BF16)

    t_idx = lax.broadcasted_iota(jnp.int32, (GMLP_CHUNK, GMLP_CHUNK), 0)
    s_idx = lax.broadcasted_iota(jnp.int32, (GMLP_CHUNK, GMLP_CHUNK), 1)
    causal = t_idx >= s_idx
    gd = d // GMLP_GROUPS
    n_chunks = rows // GMLP_CHUNK
    for g in range(GMLP_GROUPS):
        ws = jnp.where(causal, ws_ref[g], 0.0).astype(BF16)
        bias = bs_ref[g]
        vg = jnp.concatenate(
            [vln_scr[c * GMLP_CHUNK:(c + 1) * GMLP_CHUNK, g * gd:(g + 1) * gd]
             for c in range(n_chunks)], axis=1)
        mixed = _dot(ws, vg)
        for c in range(n_chunks):
            blk_rows = slice(c * GMLP_CHUNK, (c + 1) * GMLP_CHUNK)
            blk_cols = slice(g * gd, (g + 1) * gd)
            ya_ref[blk_rows, blk_cols] = (
                mix_scr[blk_rows, blk_cols] * (mixed[:, c * gd:(c + 1) * gd] + bias)).astype(BF16)

    v = seg(4)
    for blk in range(rows // MOBA_BLOCK):
        r0 = blk * MOBA_BLOCK
        for h in range(ATT_HEADS):
            vT_ref[h, blk, :hd, :] = v[r0:r0 + MOBA_BLOCK, h * hd:(h + 1) * hd].T.astype(BF16)
            vT_ref[h, blk, hd:, :] = jnp.ones((V_ROWS - hd, MOBA_BLOCK), BF16)

    gb_ref[...] = _sigmoid(seg(6) + bg_ref[:, d:]).astype(BF16)

    k = seg(3)
    k_ref[...] = k.astype(BF16)
    for blk in range(rows // MOBA_BLOCK):
        r0 = blk * MOBA_BLOCK
        kmean_ref[0, blk:blk + 1, :] = jnp.mean(k[r0:r0 + MOBA_BLOCK, :], axis=0, keepdims=True)


def _proj_call(x2, batch, norm_g, w_in, b_gates, ln_g, ln_b, w_spatial, b_spatial):
    t, d = x2.shape
    rows = PROJ_ROWS
    seq = t // batch
    assert seq % rows == 0 and rows % MOBA_BLOCK == 0 and rows % GMLP_CHUNK == 0
    n_tiles = t // rows
    tiles_per_seq = seq // rows
    blocks_per_tile = rows // MOBA_BLOCK
    nb = seq // MOBA_BLOCK
    row_spec = pl.BlockSpec((rows, d), lambda i: (i, 0))
    const2 = lambda i: (0, 0)
    const3 = lambda i: (0, 0, 0)
    act = jax.ShapeDtypeStruct((t, d), BF16)
    qT_spec = pl.BlockSpec((None, ATT_HEADS, HEAD_DIM, rows),
                           lambda i: (i // tiles_per_seq, 0, 0, i % tiles_per_seq))
    vT_spec = pl.BlockSpec((None, ATT_HEADS, blocks_per_tile, V_ROWS, MOBA_BLOCK),
                           lambda i: (i // tiles_per_seq, 0, i % tiles_per_seq, 0, 0))
    return pl.pallas_call(
        _proj_kernel,
        grid=(n_tiles,),
        in_specs=[
            row_spec,
            pl.BlockSpec((1, d), const2),
            pl.BlockSpec((d, N_SEGMENTS * d), const2, pipeline_mode=pl.Buffered(1)),
            pl.BlockSpec((1, 2 * d), const2),
            pl.BlockSpec((1, d), const2),
            pl.BlockSpec((1, d), const2),
            pl.BlockSpec((GMLP_GROUPS, GMLP_CHUNK, GMLP_CHUNK), const3),
            pl.BlockSpec((GMLP_GROUPS, GMLP_CHUNK, 1), const3),
        ],
        out_specs=[row_spec, qT_spec, row_spec, vT_spec, row_spec,
                   pl.BlockSpec((1, blocks_per_tile, d), lambda i: (i, 0, 0))],
        out_shape=[act,
                   jax.ShapeDtypeStruct((batch, ATT_HEADS, HEAD_DIM, seq), BF16),
                   act,
                   jax.ShapeDtypeStruct((batch, ATT_HEADS, nb, V_ROWS, MOBA_BLOCK), BF16),
                   act,
                   jax.ShapeDtypeStruct((n_tiles, blocks_per_tile, d), F32)],
        scratch_shapes=[pltpu.VMEM((rows, d), BF16), pltpu.VMEM((rows, d), BF16),
                        pltpu.VMEM((rows, d), F32)],
        compiler_params=pltpu.CompilerParams(
            dimension_semantics=("arbitrary",), vmem_limit_bytes=VMEM_LIMIT_BYTES),
        name="proj_gmlp",
    )(x2, norm_g, w_in, b_gates, ln_g, ln_b, w_spatial, b_spatial)


def _t5_bucket_np(n):
    n = np.maximum(n, 0)
    max_exact = REL_BUCKETS // 2
    nf = np.maximum(n, max_exact).astype(np.float32)
    large = max_exact + (np.log(nf / max_exact) / math.log(REL_MAX_DIST / max_exact)
                         * (REL_BUCKETS - max_exact)).astype(np.int32)
    large = np.minimum(large, REL_BUCKETS - 1)
    return np.where(n < max_exact, n, large).astype(np.int32)


def _bucket_tiles():
    kpos = np.arange(MOBA_BLOCK, dtype=np.int32)[:, None]
    qpos = np.arange(MOBA_BLOCK, dtype=np.int32)[None, :]
    rel = qpos - kpos
    return np.stack([_t5_bucket_np(rel), _t5_bucket_np(rel + MOBA_BLOCK)])


def _bias_kernel(relb_ref, bucket_ref, out_ref):
    h = pl.program_id(0)
    far = relb_ref[REL_BUCKETS - 1, h]
    k_idx = lax.broadcasted_iota(jnp.int32, (MOBA_BLOCK, MOBA_BLOCK), 0)
    q_idx = lax.broadcasted_iota(jnp.int32, (MOBA_BLOCK, MOBA_BLOCK), 1)
    for tile in range(2):
        bucket = bucket_ref[tile]
        bias = jnp.zeros((MOBA_BLOCK, MOBA_BLOCK), F32)
        for b in range(REL_BUCKETS):
            bias = jnp.where(bucket == b, relb_ref[b, h], bias)
        bias2 = (bias - far) * LOG2E
        if tile == BIAS_OWN:
            bias2 = jnp.where(q_idx >= k_idx, bias2, MASK_NEG)
        out_ref[0, tile] = bias2


def _bias_call(rel_bias):
    buckets = _bucket_tiles()
    return pl.pallas_call(
        _bias_kernel,
        grid=(ATT_HEADS,),
        in_specs=[pl.BlockSpec(memory_space=pltpu.SMEM),
                  pl.BlockSpec((2, MOBA_BLOCK, MOBA_BLOCK), lambda h: (0, 0, 0))],
        out_specs=pl.BlockSpec((1, N_BIAS_TILES, MOBA_BLOCK, MOBA_BLOCK), lambda h: (h, 0, 0, 0)),
        out_shape=jax.ShapeDtypeStruct((ATT_HEADS, N_BIAS_TILES, MOBA_BLOCK, MOBA_BLOCK), F32),
        compiler_params=pltpu.CompilerParams(dimension_semantics=("arbitrary",)),
        name="t5_bias_tiles",
    )(rel_bias, jnp.asarray(buckets))


BIAS_OWN, BIAS_PREV = range(2)
N_BIAS_TILES = 2


def _attn_kernel(qT_ref, k_ref, vT_ref, kmh_ref, kml_ref, oh_ref, bias_ref, o_ref,
                 qp_scr, m_scr, acc_scr):
    i = pl.program_id(1)
    nh, nb = kmh_ref.shape[0], kmh_ref.shape[1]
    blk = MOBA_BLOCK
    hd = HEAD_DIM

    n_idx = lax.broadcasted_iota(jnp.int32, (nb, blk), 0)
    past = n_idx < i
    for h in range(nh):
        qT = qT_ref[h]
        gate = _dot(kmh_ref[h], qT) + _dot(kml_ref[h], qT)
        gate = jnp.where(past, gate, -jnp.inf)
        rank = jnp.zeros((nb, blk), F32)
        for m in range(nb):
            row = gate[m:m + 1, :]
            beats = (row > gate) | ((row == gate) & (m < n_idx))
            rank = rank + jnp.where(beats, 1.0, 0.0)
        keep = (past & (rank < MOBA_TOPK)) | (n_idx == i)
        sel = jnp.where(keep, 0.0, MASK_NEG)
        sel = jnp.concatenate([sel, jnp.zeros((hd - nb, blk), F32)], axis=0)
        qp_scr[h] = jnp.concatenate([qT, sel.astype(BF16)], axis=0)

    def scores(h, j):
        r0 = pl.multiple_of(j * blk, blk)
        kp = jnp.concatenate([k_ref[pl.ds(r0, blk), h * hd:(h + 1) * hd], oh_ref[j]], axis=1)
        return _dot(kp, qp_scr[h])

    def fold(blocks, bias_tiles, first):
        ss = []
        for h in range(nh):
            parts = []
            for j, tile in zip(blocks, bias_tiles):
                s = scores(h, j)
                parts.append((s if tile is None else s + bias_ref[h, tile]).astype(BF16))
            ss.append(parts)
        ps, alphas = [], []
        for h in range(nh):
            m_new = functools.reduce(
                jnp.maximum, [jnp.max(s, axis=0, keepdims=True) for s in ss[h]]).astype(F32)
            if not first:
                m_old = m_scr[h]
                m_new = jnp.maximum(m_old, m_new)
                alphas.append(jnp.exp2(m_old - m_new))
            m_scr[h] = m_new
            ps.append([jnp.exp2(s - m_new.astype(BF16)) for s in ss[h]])
        for h in range(nh):
            pv = functools.reduce(
                lambda a, b: a + b, [_dot(vT_ref[h, j], x) for j, x in zip(blocks, ps[h])])
            acc_scr[h] = pv if first else alphas[h] * acc_scr[h] + pv

    @pl.when(i == 0)
    def _():
        fold([i], [BIAS_OWN], first=True)

    @pl.when(i >= 1)
    def _():
        fold([i, i - 1], [BIAS_OWN, BIAS_PREV], first=True)

    n_far = i - 1

    def far_pair(p, carry):
        fold([2 * p, 2 * p + 1], [None, None], first=False)
        return carry

    lax.fori_loop(0, n_far // 2, far_pair, 0)

    @pl.when((n_far >= 1) & (n_far % 2 == 1))
    def _():
        fold([n_far - 1], [None], first=False)

    for h in range(nh):
        y = acc_scr[h, :hd, :] * (1.0 / acc_scr[h, hd:hd + 1, :])
        o_ref[:, h * hd:(h + 1) * hd] = y.T.astype(o_ref.dtype)


def _attn_call(qT, k2, vT, km_hi, km_lo, bias_tiles):
    b, h, nb, v_rows, blk = vT.shape
    hd = HEAD_DIM
    s = nb * blk
    onehot = np.zeros((nb, blk, LANES), np.float32)
    for j in range(nb):
        onehot[j, :, j] = 1.0
    once = pl.Buffered(1)
    return pl.pallas_call(
        _attn_kernel,
        grid=(b, nb),
        in_specs=[
            pl.BlockSpec((None, h, hd, blk), lambda bi, i: (bi, 0, 0, i)),
            pl.BlockSpec((s, h * hd), lambda bi, i: (bi, 0)),
            pl.BlockSpec((None, h, nb, v_rows, blk), lambda bi, i: (bi, 0, 0, 0, 0)),
            pl.BlockSpec((None, h, nb, hd), lambda bi, i: (bi, 0, 0, 0)),
            pl.BlockSpec((None, h, nb, hd), lambda bi, i: (bi, 0, 0, 0)),
            pl.BlockSpec((nb, blk, LANES), lambda bi, i: (0, 0, 0), pipeline_mode=once),
            pl.BlockSpec((h, N_BIAS_TILES, blk, blk), lambda bi, i: (0, 0, 0, 0),
                         pipeline_mode=once),
        ],
        out_specs=pl.BlockSpec((blk, h * hd), lambda bi, i: (bi * nb + i, 0)),
        out_shape=jax.ShapeDtypeStruct((b * s, h * hd), BF16),
        scratch_shapes=[pltpu.VMEM((h, 2 * hd, blk), BF16), pltpu.VMEM((h, 1, blk), F32),
                        pltpu.VMEM((h, v_rows, blk), F32)],
        compiler_params=pltpu.CompilerParams(
            dimension_semantics=("arbitrary", "arbitrary"),
            vmem_limit_bytes=VMEM_LIMIT_BYTES),
        name="moba_attention",
    )(qT, k2, vT, km_hi, km_lo, jnp.asarray(onehot, BF16), bias_tiles)


ROUTE_E1, ROUTE_E2, ROUTE_W1, ROUTE_W2, ROUTE_R1, ROUTE_R2 = range(6)


def _dot_nt(a, b):
    return lax.dot_general(a, b, (((1,), (1,)), ((), ())), preferred_element_type=F32)


def _store_token_major(ref, x):
    rows = x.shape[0]
    for s in range(TOKEN_SUBLANES):
        ref[pl.ds(s, rows, stride=TOKEN_SUBLANES), :] = x[:, s * LANES:(s + 1) * LANES]


def _load_token_major(ref, rows):
    return jnp.concatenate(
        [ref[pl.ds(s, rows, stride=TOKEN_SUBLANES), :] for s in range(TOKEN_SUBLANES)], axis=1)


def _merge_kernel(x_ref, ya_ref, gb_ref, yb_ref, wo_ref, ng_ref, wrh_ref, wrl_ref, br_ref,
                  h_ref, xn_ref, route_ref, route_t_ref, counts_ref, run_scr):
    @pl.when(pl.program_id(0) == 0)
    def _():
        run_scr[...] = jnp.zeros_like(run_scr)

    f = lambda r: r[...].astype(F32)
    mix = (f(ya_ref) + f(gb_ref) * f(yb_ref)).astype(BF16)
    h = x_ref[...] + _dot(mix, wo_ref[...])
    h_ref[...] = h
    xn = _rmsnorm(h, ng_ref[...])
    _store_token_major(xn_ref, xn)
    rows = xn.shape[0]

    x_hi = xn.astype(BF16)
    x_lo = (xn - x_hi.astype(F32)).astype(BF16)
    logits = (_dot_nt(wrh_ref[...], x_hi) + _dot_nt(wrh_ref[...], x_lo)
              + _dot_nt(wrl_ref[...], x_hi) + br_ref[...])
    unit = lax.broadcasted_iota(jnp.int32, logits.shape, 0).astype(F32)
    big = float(ROUTER_UNITS)
    neg_inf = -jnp.inf

    gl = jnp.where((unit >= GROUP_UNIT0) & (unit < GROUP_UNIT0 + N_GROUPS), logits, neg_inf)
    gmax = jnp.max(gl, axis=0, keepdims=True)
    g_w = 1.0 / jnp.sum(jnp.exp(gl - gmax), axis=0, keepdims=True)
    g_idx = jnp.min(jnp.where(gl == gmax, unit, big), axis=0, keepdims=True) - GROUP_UNIT0

    e0 = g_idx * EXPERTS_PER_GROUP
    el = jnp.where((unit >= e0) & (unit < e0 + EXPERTS_PER_GROUP), logits, neg_inf)
    m1 = jnp.max(el, axis=0, keepdims=True)
    i1 = jnp.min(jnp.where(el == m1, unit, big), axis=0, keepdims=True)
    el2 = jnp.where(unit == i1, neg_inf, el)
    m2 = jnp.max(el2, axis=0, keepdims=True)
    i2 = jnp.min(jnp.where(el2 == m2, unit, big), axis=0, keepdims=True)
    e2 = jnp.exp(m2 - m1)
    den = 1.0 + e2
    w1 = (1.0 / den) * g_w
    w2 = (e2 / den) * g_w

    hit1 = unit == i1
    hit2 = unit == i2
    onehot = jnp.where(hit1, 1.0, jnp.where(hit2, 1.0, 0.0))
    c_idx = lax.broadcasted_iota(jnp.int32, (rows, rows), 0)
    r_idx = lax.broadcasted_iota(jnp.int32, (rows, rows), 1)
    earlier = jnp.where(c_idx < r_idx, 1.0, 0.0).astype(BF16)
    prefix = run_scr[...] + _dot(onehot.astype(BF16), earlier)
    rank1 = jnp.sum(jnp.where(hit1, prefix, 0.0), axis=0, keepdims=True)
    rank2 = jnp.sum(jnp.where(hit2, prefix, 0.0), axis=0, keepdims=True)
    run_scr[...] = run_scr[...] + jnp.sum(onehot, axis=1, keepdims=True)
    counts_ref[...] = run_scr[...]

    route_t = jnp.concatenate(
        [i1, i2, w1, w2, rank1, rank2, jnp.zeros((ROUTER_LANES - 6, rows), F32)], axis=0)
    route_t_ref[0] = route_t[:ROUTE_ROWS]
    route_ref[...] = route_t.T


def _merge_call(x2, ya, gb, yb, w_out, norm_g, wr_hi, wr_lo, b_router):
    t, d = x2.shape
    rows = MERGE_ROWS
    assert t % rows == 0 and d == TOKEN_SUBLANES * LANES
    n_tiles = t // rows
    row_spec = pl.BlockSpec((rows, d), lambda i: (i, 0))
    const2 = lambda i: (0, 0)
    return pl.pallas_call(
        _merge_kernel,
        grid=(n_tiles,),
        in_specs=[row_spec, row_spec, row_spec, row_spec,
                  pl.BlockSpec((d, d), const2),
                  pl.BlockSpec((1, d), const2),
                  pl.BlockSpec((ROUTER_UNITS, d), const2),
                  pl.BlockSpec((ROUTER_UNITS, d), const2),
                  pl.BlockSpec((ROUTER_UNITS, 1), const2)],
        out_specs=[row_spec,
                   pl.BlockSpec((rows * TOKEN_SUBLANES, LANES), lambda i: (i, 0)),
                   pl.BlockSpec((rows, ROUTER_LANES), lambda i: (i, 0)),
                   pl.BlockSpec((1, ROUTE_ROWS, rows), lambda i: (i, 0, 0)),
                   pl.BlockSpec((ROUTER_UNITS, 1), const2)],
        out_shape=[jax.ShapeDtypeStruct((t, d), F32),
                   jax.ShapeDtypeStruct((t * TOKEN_SUBLANES, LANES), F32),
                   jax.ShapeDtypeStruct((t, ROUTER_LANES), F32),
                   jax.ShapeDtypeStruct((n_tiles, ROUTE_ROWS, rows), F32),
                   jax.ShapeDtypeStruct((ROUTER_UNITS, 1), F32)],
        scratch_shapes=[pltpu.VMEM((ROUTER_UNITS, 1), F32)],
        compiler_params=pltpu.CompilerParams(
            dimension_semantics=("arbitrary",), vmem_limit_bytes=VMEM_LIMIT_BYTES),
        name="merge_outproj_router",
    )(x2, ya, gb, yb, w_out, norm_g, wr_hi, wr_lo, b_router)


def _token_rows(ref, token):
    return ref.at[pl.ds(pl.multiple_of(token * TOKEN_SUBLANES, TOKEN_SUBLANES), TOKEN_SUBLANES)]


def _dispatch_kernel(last_ref, nreal_ref, pos1_ref, pos2_ref, xn_ref, xs_hbm, zero_scr, sem):
    rows = pos1_ref.shape[2]
    tile = EXPERT_ROWS
    n_tiles = xs_hbm.shape[0] // (tile * TOKEN_SUBLANES)

    def zero_tile(j):
        start = pl.multiple_of(j * (tile * TOKEN_SUBLANES), tile * TOKEN_SUBLANES)
        return pltpu.make_async_copy(
            zero_scr, xs_hbm.at[pl.ds(start, tile * TOKEN_SUBLANES)], sem)

    @pl.when(pl.program_id(0) == 0)
    def _():
        zero_scr[...] = jnp.zeros_like(zero_scr)
        for e in range(N_EXPERTS):
            @pl.when(last_ref[e] >= 0)
            def _():
                zero_tile(last_ref[e]).start()

        def tail_start(j, carry):
            zero_tile(j).start()
            return carry

        lax.fori_loop(nreal_ref[0], n_tiles, tail_start, 0)

        for e in range(N_EXPERTS):
            @pl.when(last_ref[e] >= 0)
            def _():
                zero_tile(0).wait()

        def tail_wait(j, carry):
            zero_tile(0).wait()
            return carry

        lax.fori_loop(nreal_ref[0], n_tiles, tail_wait, 0)

    def issue(g, carry):
        for u in range(DMA_UNROLL):
            r = g * DMA_UNROLL + u
            src = _token_rows(xn_ref, r)
            pltpu.make_async_copy(src, _token_rows(xs_hbm, pos1_ref[0, 0, r]), sem).start(0)
            pltpu.make_async_copy(src, _token_rows(xs_hbm, pos2_ref[0, 0, r]), sem).start(1)
        return carry

    lax.fori_loop(0, rows // DMA_UNROLL, issue, 0)

    for _ in range(2):
        pltpu.make_async_copy(xn_ref, xs_hbm.at[pl.ds(0, rows * TOKEN_SUBLANES)], sem).wait()


def _dispatch_call(last_tile, n_real, pos1, pos2, xn, n_sorted_rows):
    n_steps, _, rows = pos1.shape
    smem_row = pl.BlockSpec((1, 1, rows), lambda i, lt, nr: (i, 0, 0), memory_space=pltpu.SMEM)
    return pl.pallas_call(
        _dispatch_kernel,
        grid_spec=pltpu.PrefetchScalarGridSpec(
            num_scalar_prefetch=2,
            grid=(n_steps,),
            in_specs=[smem_row, smem_row,
                      pl.BlockSpec((rows * TOKEN_SUBLANES, LANES), lambda i, lt, nr: (i, 0))],
            out_specs=pl.BlockSpec(memory_space=pl.ANY),
            scratch_shapes=[pltpu.VMEM((EXPERT_ROWS * TOKEN_SUBLANES, LANES), F32),
                            pltpu.SemaphoreType.DMA(())],
        ),
        out_shape=jax.ShapeDtypeStruct((n_sorted_rows * TOKEN_SUBLANES, LANES), F32),
        compiler_params=pltpu.CompilerParams(dimension_semantics=("arbitrary",)),
        name="moe_dispatch",
    )(last_tile, n_real, pos1, pos2, xn)


EXPERT_IN_SLOTS = 3
EXPERT_OUT_SLOTS = 2


def _expert_kernel(first_ref, end_ref, nreal_ref, xs_hbm, w1_ref, w3_ref, w2_ref, ys_hbm,
                   w1_scr, w3_scr, w2_scr, xbuf, ybuf, in_sems, out_sems):
    e = pl.program_id(0)
    tile_rows = EXPERT_ROWS * TOKEN_SUBLANES
    n_real = nreal_ref[0]
    n_tiles = xs_hbm.shape[0] // tile_rows

    def tile_of(ref, t):
        return ref.at[pl.ds(pl.multiple_of(t * tile_rows, tile_rows), tile_rows)]

    def in_copy(t):
        slot = t % EXPERT_IN_SLOTS
        return pltpu.make_async_copy(tile_of(xs_hbm, t), xbuf.at[slot], in_sems.at[slot])

    def out_copy(t):
        slot = t % EXPERT_OUT_SLOTS
        return pltpu.make_async_copy(ybuf.at[slot], tile_of(ys_hbm, t), out_sems.at[slot])

    @pl.when(e == 0)
    def _():
        for t in range(EXPERT_IN_SLOTS - 1):
            @pl.when(t < n_real)
            def _():
                in_copy(t).start()

    w1_scr[...] = w1_ref[...].astype(BF16)
    w3_scr[...] = w3_ref[...].astype(BF16)
    w2_scr[...] = w2_ref[...].astype(BF16)

    def tile(t, carry):
        ahead = t + EXPERT_IN_SLOTS - 1

        @pl.when(ahead < n_real)
        def _():
            in_copy(ahead).start()

        in_copy(t).wait()

        @pl.when(t >= EXPERT_OUT_SLOTS)
        def _():
            out_copy(t - EXPERT_OUT_SLOTS).wait()

        x = _load_token_major(xbuf.at[t % EXPERT_IN_SLOTS], EXPERT_ROWS).astype(BF16)
        a = _dot(x, w1_scr[...])
        b = _dot(x, w3_scr[...])
        hid = (a * _sigmoid(a)) * b
        _store_token_major(ybuf.at[t % EXPERT_OUT_SLOTS], _dot(hid.astype(BF16), w2_scr[...]))
        out_copy(t).start()
        return carry

    lax.fori_loop(first_ref[e], end_ref[e], tile, 0)

    @pl.when(e == pl.num_programs(0) - 1)
    def _():
        for back in range(EXPERT_OUT_SLOTS, 0, -1):
            @pl.when(n_real - back >= 0)
            def _():
                out_copy(n_real - back).wait()
        ybuf[0] = jnp.zeros(ybuf.shape[1:], F32)

        def tail_start(t, carry):
            pltpu.make_async_copy(ybuf.at[0], tile_of(ys_hbm, t), out_sems.at[0]).start()
            return carry

        def tail_wait(t, carry):
            pltpu.make_async_copy(ybuf.at[0], tile_of(ys_hbm, t), out_sems.at[0]).wait()
            return carry

        lax.fori_loop(n_real, n_tiles, tail_start, 0)
        lax.fori_loop(n_real, n_tiles, tail_wait, 0)


def _expert_call(first_tile, end_tile, n_real, xs, w1, w3, w2):
    n_experts, d, d_expert = w1.shape
    tile_rows = EXPERT_ROWS * TOKEN_SUBLANES
    per_expert = lambda e, f, n, nr: (e, 0, 0)
    return pl.pallas_call(
        _expert_kernel,
        grid_spec=pltpu.PrefetchScalarGridSpec(
            num_scalar_prefetch=3,
            grid=(n_experts,),
            in_specs=[pl.BlockSpec(memory_space=pl.ANY),
                      pl.BlockSpec((None, d, d_expert), per_expert),
                      pl.BlockSpec((None, d, d_expert), per_expert),
                      pl.BlockSpec((None, d_expert, d), per_expert)],
            out_specs=pl.BlockSpec(memory_space=pl.ANY),
            scratch_shapes=[pltpu.VMEM((d, d_expert), BF16), pltpu.VMEM((d, d_expert), BF16),
                            pltpu.VMEM((d_expert, d), BF16),
                            pltpu.VMEM((EXPERT_IN_SLOTS, tile_rows, LANES), F32),
                            pltpu.VMEM((EXPERT_OUT_SLOTS, tile_rows, LANES), F32),
                            pltpu.SemaphoreType.DMA((EXPERT_IN_SLOTS,)),
                            pltpu.SemaphoreType.DMA((EXPERT_OUT_SLOTS,))],
        ),
        out_shape=jax.ShapeDtypeStruct(xs.shape, F32),
        compiler_params=pltpu.CompilerParams(
            dimension_semantics=("arbitrary",), vmem_limit_bytes=VMEM_LIMIT_BYTES),
        name="moe_experts",
    )(first_tile, end_tile, n_real, xs, w1, w3, w2)


def _combine_kernel(p1_first, p2_first, p1_next, p2_next, ys_hbm, h_ref, route_ref, ng_ref,
                    out_ref, buf, sems):
    i = pl.program_id(0)
    n = pl.num_programs(0)
    rows = h_ref.shape[0]

    def issue(p1_ref, p2_ref, slot):
        def body(g, carry):
            for u in range(DMA_UNROLL):
                r = g * DMA_UNROLL + u
                for which, p_ref in ((0, p1_ref), (1, p2_ref)):
                    pltpu.make_async_copy(_token_rows(ys_hbm, p_ref[0, 0, r]),
                                          _token_rows(buf.at[2 * slot + which], r),
                                          sems.at[slot]).start(which)
            return carry
        lax.fori_loop(0, rows // DMA_UNROLL, body, 0)

    @pl.when(i == 0)
    def _():
        issue(p1_first, p2_first, 0)

    @pl.when(i + 1 < n)
    def _():
        issue(p1_next, p2_next, (i + 1) % 2)

    slot = i % 2
    for which in range(2):
        pltpu.make_async_copy(ys_hbm.at[pl.ds(0, rows * TOKEN_SUBLANES)],
                              buf.at[2 * slot + which], sems.at[slot]).wait()

    route = route_ref[...]
    w1 = route[:, ROUTE_W1:ROUTE_W1 + 1]
    w2 = route[:, ROUTE_W2:ROUTE_W2 + 1]
    y = (h_ref[...] + w1 * _load_token_major(buf.at[2 * slot], rows)
         + w2 * _load_token_major(buf.at[2 * slot + 1], rows))
    out_ref[...] = _rmsnorm(y, ng_ref[...])


def _combine_call(pos1, pos2, ys, h, route, norm_g):
    t, d = h.shape
    n_steps, _, rows = pos1.shape
    row_spec = pl.BlockSpec((rows, d), lambda i: (i, 0))
    first = pl.BlockSpec((1, 1, rows), lambda i: (0, 0, 0), memory_space=pltpu.SMEM)
    nxt = pl.BlockSpec((1, 1, rows), lambda i: (jnp.minimum(i + 1, n_steps - 1), 0, 0),
                       memory_space=pltpu.SMEM)
    return pl.pallas_call(
        _combine_kernel,
        grid=(n_steps,),
        in_specs=[first, first, nxt, nxt,
                  pl.BlockSpec(memory_space=pl.ANY),
                  row_spec,
                  pl.BlockSpec((rows, ROUTER_LANES), lambda i: (i, 0)),
                  pl.BlockSpec((1, d), lambda i: (0, 0))],
        out_specs=row_spec,
        out_shape=jax.ShapeDtypeStruct((t, d), F32),
        scratch_shapes=[pltpu.VMEM((4, rows * TOKEN_SUBLANES, LANES), F32),
                        pltpu.SemaphoreType.DMA((2,))],
        compiler_params=pltpu.CompilerParams(
            dimension_semantics=("arbitrary",), vmem_limit_bytes=VMEM_LIMIT_BYTES),
        name="moe_combine",
    )(pos1, pos2, pos1, pos2, ys, h, route, norm_g)


def _sparse_moe(xn, route, route_t, counts, h, w1, w3, w2, norm_g):
    t = h.shape[0]
    tile = EXPERT_ROWS
    n_tiles = (2 * t) // tile + N_EXPERTS
    expert = jnp.arange(N_EXPERTS, dtype=jnp.int32)
    counts = counts[:N_EXPERTS, 0].astype(jnp.int32)
    group_tiles = (counts + tile - 1) // tile
    end_tile = jnp.sum(jnp.where(expert[None, :] <= expert[:, None], group_tiles[None, :], 0), axis=1)
    first_tile = end_tile - group_tiles
    n_real = end_tile[-1:]
    last_tile = jnp.where(group_tiles > 0, end_tile - 1, -1)

    def positions(e_row, r_row):
        e = route_t[:, e_row, :].astype(jnp.int32)
        start = jnp.zeros_like(e)
        for k in range(N_EXPERTS):
            start = jnp.where(e == k, first_tile[k] * tile, start)
        return start + route_t[:, r_row, :].astype(jnp.int32)

    pos1 = positions(ROUTE_E1, ROUTE_R1)
    pos2 = positions(ROUTE_E2, ROUTE_R2)
    per_step = lambda pos, rows: pos.reshape(t // rows, 1, rows)
    xs = _dispatch_call(last_tile, n_real, per_step(pos1, DISPATCH_ROWS),
                        per_step(pos2, DISPATCH_ROWS), xn, n_tiles * tile)
    ys = _expert_call(first_tile, end_tile, n_real, xs, w1, w3, w2)
    return _combine_call(per_step(pos1, COMBINE_ROWS), per_step(pos2, COMBINE_ROWS), ys, h,
                         route, norm_g)


def _layer(h, norm_mix_g, w_in, b_gates, gmlp_ln_g, gmlp_ln_b, w_spatial, b_spatial, bias_tiles,
           w_out, norm_ffn_g, w_group_router, b_group_router, w_expert_router, b_expert_router,
           w1, w3, w2, norm_out_g):
    b, s, d = h.shape
    t = b * s
    nb = s // MOBA_BLOCK
    x2 = h.reshape(t, d)
    row = lambda v: v.reshape(1, -1)

    ya, qT, k, vT, gb, kmean = _proj_call(
        x2, b, row(norm_mix_g), w_in.astype(BF16), row(b_gates), row(gmlp_ln_g), row(gmlp_ln_b),
        w_spatial, b_spatial[:, :, None])

    km = jnp.transpose(kmean.reshape(b, nb, ATT_HEADS, HEAD_DIM), (0, 2, 1, 3))
    km_hi = km.astype(BF16)
    km_lo = (km - km_hi.astype(F32)).astype(BF16)
    yb = _attn_call(qT, k, vT, km_hi, km_lo, bias_tiles)

    w_router = jnp.concatenate(
        [jnp.transpose(w_expert_router, (0, 2, 1)).reshape(N_EXPERTS, d), w_group_router.T,
         jnp.zeros((ROUTER_UNITS - N_EXPERTS - N_GROUPS, d), F32)], axis=0)
    b_router = jnp.concatenate(
        [b_expert_router.reshape(-1), b_group_router,
         jnp.zeros((ROUTER_UNITS - N_EXPERTS - N_GROUPS,), F32)]).reshape(ROUTER_UNITS, 1)
    wr_hi = w_router.astype(BF16)
    wr_lo = (w_router - wr_hi.astype(F32)).astype(BF16)
    h2, xn, route, route_t, counts = _merge_call(
        x2, ya, gb, yb, w_out.astype(BF16), row(norm_ffn_g), wr_hi, wr_lo, b_router)

    out = _sparse_moe(xn, route, route_t, counts, h2, w1, w3, w2, row(norm_out_g))
    return out.reshape(b, s, d)


def kernel(x, norm_mix_g, w_in, b_gates, gmlp_ln_g, gmlp_ln_b, w_spatial, b_spatial, rel_bias, w_out, norm_ffn_g, w_group_router, b_group_router, w_expert_router, b_expert_router, w1, w3, w2, norm_final_g):
    depth = w_in.shape[0]
    assert depth == 1, "the final rmsnorm is fused into the last layer's combine kernel"
    bias_tiles = _bias_call(rel_bias)
    return _layer(x, norm_mix_g[0], w_in[0], b_gates[0], gmlp_ln_g[0], gmlp_ln_b[0], w_spatial[0],
                  b_spatial[0], bias_tiles, w_out[0], norm_ffn_g[0], w_group_router[0],
                  b_group_router[0], w_expert_router[0], b_expert_router[0], w1[0], w3[0], w2[0],
                  norm_final_g)
```

```python
import functools
import math

import numpy as np
import jax
import jax.numpy as jnp
from jax import lax
from jax.experimental import pallas as pl
from jax.experimental.pallas import tpu as pltpu

F32 = jnp.float32
BF16 = jnp.bfloat16

D_MODEL = 1024
NORM_EPS = 1e-6
GMLP_GROUPS = 8
GMLP_CHUNK = 128
ATT_HEADS = 8
HEAD_DIM = 128
MOBA_BLOCK = 256
MOBA_TOPK = 3
REL_BUCKETS = 32
REL_MAX_DIST = 128
N_GROUPS = 4
EXPERTS_PER_GROUP = 8
N_EXPERTS = N_GROUPS * EXPERTS_PER_GROUP
D_EXPERT = 256
N_SEGMENTS = 7

LANES = 128
TOKEN_SUBLANES = 8
ROUTE_ROWS = 8
VMEM_LIMIT_BYTES = 56 * 1024 * 1024

SQRT_HALF = math.sqrt(0.5)
LOG2E = math.log2(math.e)
SCORE_SCALE2 = (HEAD_DIM ** -0.5) * LOG2E
MASK_NEG = -(2.0 ** 100)
BF16_SUBLANES = 16
V_ROWS = HEAD_DIM + BF16_SUBLANES
ROUTER_LANES = LANES
ROUTER_UNITS = -(-(N_EXPERTS + N_GROUPS) // BF16_SUBLANES) * BF16_SUBLANES
GROUP_UNIT0 = N_EXPERTS

PROJ_ROWS = 512
MERGE_ROWS = 1024
EXPERT_ROWS = 256
DISPATCH_ROWS = 2048
COMBINE_ROWS = 256
DMA_UNROLL = 8


def _rmsnorm(x, g):
    return x * lax.rsqrt(jnp.mean(x * x, axis=-1, keepdims=True) + NORM_EPS) * g


def _gelu(a):
    return 0.5 * a * (1.0 + lax.erf(a * SQRT_HALF))


def _sigmoid(a):
    return 1.0 / (1.0 + jnp.exp(-a))


def _dot(a, b):
    return jnp.dot(a, b, preferred_element_type=F32)


def _proj_kernel(x_ref, ng_ref, w_ref, bg_ref, lng_ref, lnb_ref, ws_ref, bs_ref,
                 ya_ref, qT_ref, k_ref, vT_ref, gb_ref, kmean_ref,
                 xn_scr, vln_scr, mix_scr):
    rows = x_ref.shape[0]
    d = D_MODEL
    xn_scr[...] = _rmsnorm(x_ref[...], ng_ref[...]).astype(BF16)

    def seg(i):
        return _dot(xn_scr[...], w_ref[:, i * d:(i + 1) * d])

    hd = HEAD_DIM

    v = _gelu(seg(1))
    mu = jnp.mean(v, axis=-1, keepdims=True)
    vc = v - mu
    var = jnp.mean(vc * vc, axis=-1, keepdims=True)
    vln_scr[...] = (vc * lax.rsqrt(var + NORM_EPS) * lng_ref[...] + lnb_ref[...]).astype(BF16)

    mix_scr[...] = _gelu(seg(0)) * _sigmoid(seg(5) + bg_ref[:, :d])

    q = seg(2) * SCORE_SCALE2
    for h in range(ATT_HEADS):
        qT_ref[h] = q[:, h * hd:(h + 1) * hd].T.astype(BF16)

    t_idx = lax.broadcasted_iota(jnp.int32, (GMLP_CHUNK, GMLP_CHUNK), 0)
    s_idx = lax.broadcasted_iota(jnp.int32, (GMLP_CHUNK, GMLP_CHUNK), 1)
    causal = t_idx >= s_idx
    gd = d // GMLP_GROUPS
    n_chunks = rows // GMLP_CHUNK
    for g in range(GMLP_GROUPS):
        ws = jnp.where(causal, ws_ref[g], 0.0).astype(BF16)
        bias = bs_ref[g]
        vg = jnp.concatenate(
            [vln_scr[c * GMLP_CHUNK:(c + 1) * GMLP_CHUNK, g * gd:(g + 1) * gd]
             for c in range(n_chunks)], axis=1)
        mixed = _dot(ws, vg)
        for c in range(n_chunks):
            blk_rows = slice(c * GMLP_CHUNK, (c + 1) * GMLP_CHUNK)
            blk_cols = slice(g * gd, (g + 1) * gd)
            ya_ref[blk_rows, blk_cols] = (
                mix_scr[blk_rows, blk_cols] * (mixed[:, c * gd:(c + 1) * gd] + bias)).astype(BF16)

    v = seg(4)
    for blk in range(rows // MOBA_BLOCK):
        r0 = blk * MOBA_BLOCK
        for h in range(ATT_HEADS):
            vT_ref[h, blk, :hd, :] = v[r0:r0 + MOBA_BLOCK, h * hd:(h + 1) * hd].T.astype(BF16)
            vT_ref[h, blk, hd:, :] = jnp.ones((V_ROWS - hd, MOBA_BLOCK), BF16)

    gb_ref[...] = _sigmoid(seg(6) + bg_ref[:, d:]).astype(BF16)

    k = seg(3)
    k_ref[...] = k.astype(BF16)
    for blk in range(rows // MOBA_BLOCK):
        r0 = blk * MOBA_BLOCK
        kmean_ref[0, blk:blk + 1, :] = jnp.mean(k[r0:r0 + MOBA_BLOCK, :], axis=0, keepdims=True)


def _proj_call(x2, batch, norm_g, w_in, b_gates, ln_g, ln_b, w_spatial, b_spatial):
    t, d = x2.shape
    rows = PROJ_ROWS
    seq = t // batch
    assert seq % rows == 0 and rows % MOBA_BLOCK == 0 and rows % GMLP_CHUNK == 0
    n_tiles = t // rows
    tiles_per_seq = seq // rows
    blocks_per_tile = rows // MOBA_BLOCK
    nb = seq // MOBA_BLOCK
    row_spec = pl.BlockSpec((rows, d), lambda i: (i, 0))
    const2 = lambda i: (0, 0)
    const3 = lambda i: (0, 0, 0)
    act = jax.ShapeDtypeStruct((t, d), BF16)
    qT_spec = pl.BlockSpec((None, ATT_HEADS, HEAD_DIM, rows),
                           lambda i: (i // tiles_per_seq, 0, 0, i % tiles_per_seq))
    vT_spec = pl.BlockSpec((None, ATT_HEADS, blocks_per_tile, V_ROWS, MOBA_BLOCK),
                           lambda i: (i // tiles_per_seq, 0, i % tiles_per_seq, 0, 0))
    return pl.pallas_call(
        _proj_kernel,
        grid=(n_tiles,),
        in_specs=[
            row_spec,
            pl.BlockSpec((1, d), const2),
            pl.BlockSpec((d, N_SEGMENTS * d), const2, pipeline_mode=pl.Buffered(1)),
            pl.BlockSpec((1, 2 * d), const2),
            pl.BlockSpec((1, d), const2),
            pl.BlockSpec((1, d), const2),
            pl.BlockSpec((GMLP_GROUPS, GMLP_CHUNK, GMLP_CHUNK), const3),
            pl.BlockSpec((GMLP_GROUPS, GMLP_CHUNK, 1), const3),
        ],
        out_specs=[row_spec, qT_spec, row_spec, vT_spec, row_spec,
                   pl.BlockSpec((1, blocks_per_tile, d), lambda i: (i, 0, 0))],
        out_shape=[act,
                   jax.ShapeDtypeStruct((batch, ATT_HEADS, HEAD_DIM, seq), BF16),
                   act,
                   jax.ShapeDtypeStruct((batch, ATT_HEADS, nb, V_ROWS, MOBA_BLOCK), BF16),
                   act,
                   jax.ShapeDtypeStruct((n_tiles, blocks_per_tile, d), F32)],
        scratch_shapes=[pltpu.VMEM((rows, d), BF16), pltpu.VMEM((rows, d), BF16),
                        pltpu.VMEM((rows, d), F32)],
        compiler_params=pltpu.CompilerParams(
            dimension_semantics=("arbitrary",), vmem_limit_bytes=VMEM_LIMIT_BYTES),
        name="proj_gmlp",
    )(x2, norm_g, w_in, b_gates, ln_g, ln_b, w_spatial, b_spatial)


def _t5_bucket_np(n):
    n = np.maximum(n, 0)
    max_exact = REL_BUCKETS // 2
    nf = np.maximum(n, max_exact).astype(np.float32)
    large = max_exact + (np.log(nf / max_exact) / math.log(REL_MAX_DIST / max_exact)
                         * (REL_BUCKETS - max_exact)).astype(np.int32)
    large = np.minimum(large, REL_BUCKETS - 1)
    return np.where(n < max_exact, n, large).astype(np.int32)


def _bucket_tiles():
    kpos = np.arange(MOBA_BLOCK, dtype=np.int32)[:, None]
    qpos = np.arange(MOBA_BLOCK, dtype=np.int32)[None, :]
    rel = qpos - kpos
    return np.stack([_t5_bucket_np(rel), _t5_bucket_np(rel + MOBA_BLOCK)])


def _bias_kernel(relb_ref, bucket_ref, out_ref):
    h = pl.program_id(0)
    far = relb_ref[REL_BUCKETS - 1, h]
    k_idx = lax.broadcasted_iota(jnp.int32, (MOBA_BLOCK, MOBA_BLOCK), 0)
    q_idx = lax.broadcasted_iota(jnp.int32, (MOBA_BLOCK, MOBA_BLOCK), 1)
    for tile in range(2):
        bucket = bucket_ref[tile]
        bias = jnp.zeros((MOBA_BLOCK, MOBA_BLOCK), F32)
        for b in range(REL_BUCKETS):
            bias = jnp.where(bucket == b, relb_ref[b, h], bias)
        bias2 = (bias - far) * LOG2E
        if tile == BIAS_OWN:
            bias2 = jnp.where(q_idx >= k_idx, bias2, MASK_NEG)
        out_ref[0, tile] = bias2


def _bias_call(rel_bias):
    buckets = _bucket_tiles()
    return pl.pallas_call(
        _bias_kernel,
        grid=(ATT_HEADS,),
        in_specs=[pl.BlockSpec(memory_space=pltpu.SMEM),
                  pl.BlockSpec((2, MOBA_BLOCK, MOBA_BLOCK), lambda h: (0, 0, 0))],
        out_specs=pl.BlockSpec((1, N_BIAS_TILES, MOBA_BLOCK, MOBA_BLOCK), lambda h: (h, 0, 0, 0)),
        out_shape=jax.ShapeDtypeStruct((ATT_HEADS, N_BIAS_TILES, MOBA_BLOCK, MOBA_BLOCK), F32),
        compiler_params=pltpu.CompilerParams(dimension_semantics=("arbitrary",)),
        name="t5_bias_tiles",
    )(rel_bias, jnp.asarray(buckets))


BIAS_OWN, BIAS_PREV = range(2)
N_BIAS_TILES = 2


def _attn_kernel(qT_ref, k_ref, vT_ref, kmh_ref, kml_ref, oh_ref, bias_ref, o_ref,
                 qp_scr, m_scr, acc_scr):
    i = pl.program_id(1)
    nh, nb = kmh_ref.shape[0], kmh_ref.shape[1]
    blk = MOBA_BLOCK
    hd = HEAD_DIM

    n_idx = lax.broadcasted_iota(jnp.int32, (nb, blk), 0)
    past = n_idx < i
    for h in range(nh):
        qT = qT_ref[h]
        gate = _dot(kmh_ref[h], qT) + _dot(kml_ref[h], qT)
        gate = jnp.where(past, gate, -jnp.inf)
        rank = jnp.zeros((nb, blk), F32)
        for m in range(nb):
            row = gate[m:m + 1, :]
            beats = (row > gate) | ((row == gate) & (m < n_idx))
            rank = rank + jnp.where(beats, 1.0, 0.0)
        keep = (past & (rank < MOBA_TOPK)) | (n_idx == i)
        sel = jnp.where(keep, 0.0, MASK_NEG)
        sel = jnp.concatenate([sel, jnp.zeros((hd - nb, blk), F32)], axis=0)
        qp_scr[h] = jnp.concatenate([qT, sel.astype(BF16)], axis=0)

    def scores(h, j):
        r0 = pl.multiple_of(j * blk, blk)
        kp = jnp.concatenate([k_ref[pl.ds(r0, blk), h * hd:(h + 1) * hd], oh_ref[j]], axis=1)
        return _dot(kp, qp_scr[h])

    def fold(blocks, bias_tiles, first):
        ss = []
        for h in range(nh):
            parts = []
            for j, tile in zip(blocks, bias_tiles):
                s = scores(h, j)
                parts.append((s if tile is None else s + bias_ref[h, tile]).astype(BF16))
            ss.append(parts)
        ps, alphas = [], []
        for h in range(nh):
            m_new = functools.reduce(
                jnp.maximum, [jnp.max(s, axis=0, keepdims=True) for s in ss[h]]).astype(F32)
            if not first:
                m_old = m_scr[h]
                m_new = jnp.maximum(m_old, m_new)
                alphas.append(jnp.exp2(m_old - m_new))
            m_scr[h] = m_new
            ps.append([jnp.exp2(s - m_new.astype(BF16)) for s in ss[h]])
        for h in range(nh):
            pv = functools.reduce(
                lambda a, b: a + b, [_dot(vT_ref[h, j], x) for j, x in zip(blocks, ps[h])])
            acc_scr[h] = pv if first else alphas[h] * acc_scr[h] + pv

    @pl.when(i == 0)
    def _():
        fold([i], [BIAS_OWN], first=True)

    @pl.when(i >= 1)
    def _():
        fold([i, i - 1], [BIAS_OWN, BIAS_PREV], first=True)

    n_far = i - 1

    def far_pair(p, carry):
        fold([2 * p, 2 * p + 1], [None, None], first=False)
        return carry

    lax.fori_loop(0, n_far // 2, far_pair, 0)

    @pl.when((n_far >= 1) & (n_far % 2 == 1))
    def _():
        fold([n_far - 1], [None], first=False)

    for h in range(nh):
        y = acc_scr[h, :hd, :] * (1.0 / acc_scr[h, hd:hd + 1, :])
        o_ref[:, h * hd:(h + 1) * hd] = y.T.astype(o_ref.dtype)


def _attn_call(qT, k2, vT, km_hi, km_lo, bias_tiles):
    b, h, nb, v_rows, blk = vT.shape
    hd = HEAD_DIM
    s = nb * blk
    onehot = np.zeros((nb, blk, LANES), np.float32)
    for j in range(nb):
        onehot[j, :, j] = 1.0
    once = pl.Buffered(1)
    return pl.pallas_call(
        _attn_kernel,
        grid=(b, nb),
        in_specs=[
            pl.BlockSpec((None, h, hd, blk), lambda bi, i: (bi, 0, 0, i)),
            pl.BlockSpec((s, h * hd), lambda bi, i: (bi, 0)),
            pl.BlockSpec((None, h, nb, v_rows, blk), lambda bi, i: (bi, 0, 0, 0, 0)),
            pl.BlockSpec((None, h, nb, hd), lambda bi, i: (bi, 0, 0, 0)),
            pl.BlockSpec((None, h, nb, hd), lambda bi, i: (bi, 0, 0, 0)),
            pl.BlockSpec((nb, blk, LANES), lambda bi, i: (0, 0, 0), pipeline_mode=once),
            pl.BlockSpec((h, N_BIAS_TILES, blk, blk), lambda bi, i: (0, 0, 0, 0),
                         pipeline_mode=once),
        ],
        out_specs=pl.BlockSpec((blk, h * hd), lambda bi, i: (bi * nb + i, 0)),
        out_shape=jax.ShapeDtypeStruct((b * s, h * hd), BF16),
        scratch_shapes=[pltpu.VMEM((h, 2 * hd, blk), BF16), pltpu.VMEM((h, 1, blk), F32),
                        pltpu.VMEM((h, v_rows, blk), F32)],
        compiler_params=pltpu.CompilerParams(
            dimension_semantics=("arbitrary", "arbitrary"),
            vmem_limit_bytes=VMEM_LIMIT_BYTES),
        name="moba_attention",
    )(qT, k2, vT, km_hi, km_lo, jnp.asarray(onehot, BF16), bias_tiles)


ROUTE_E1, ROUTE_E2, ROUTE_W1, ROUTE_W2, ROUTE_R1, ROUTE_R2 = range(6)


def _dot_nt(a, b):
    return lax.dot_general(a, b, (((1,), (1,)), ((), ())), preferred_element_type=F32)


def _store_token_major(ref, x):
    rows = x.shape[0]
    for s in range(TOKEN_SUBLANES):
        ref[pl.ds(s, rows, stride=TOKEN_SUBLANES), :] = x[:, s * LANES:(s + 1) * LANES]


def _load_token_major(ref, rows):
    return jnp.concatenate(
        [ref[pl.ds(s, rows, stride=TOKEN_SUBLANES), :] for s in range(TOKEN_SUBLANES)], axis=1)


def _merge_kernel(x_ref, ya_ref, gb_ref, yb_ref, wo_ref, ng_ref, wrh_ref, wrl_ref, br_ref,
                  h_ref, xn_ref, route_ref, route_t_ref, counts_ref, run_scr):
    @pl.when(pl.program_id(0) == 0)
    def _():
        run_scr[...] = jnp.zeros_like(run_scr)

    f = lambda r: r[...].astype(F32)
    mix = (f(ya_ref) + f(gb_ref) * f(yb_ref)).astype(BF16)
    h = x_ref[...] + _dot(mix, wo_ref[...])
    h_ref[...] = h
    xn = _rmsnorm(h, ng_ref[...])
    _store_token_major(xn_ref, xn)
    rows = xn.shape[0]

    x_hi = xn.astype(BF16)
    x_lo = (xn - x_hi.astype(F32)).astype(BF16)
    logits = (_dot_nt(wrh_ref[...], x_hi) + _dot_nt(wrh_ref[...], x_lo)
              + _dot_nt(wrl_ref[...], x_hi) + br_ref[...])
    unit = lax.broadcasted_iota(jnp.int32, logits.shape, 0).astype(F32)
    big = float(ROUTER_UNITS)
    neg_inf = -jnp.inf

    gl = jnp.where((unit >= GROUP_UNIT0) & (unit < GROUP_UNIT0 + N_GROUPS), logits, neg_inf)
    gmax = jnp.max(gl, axis=0, keepdims=True)
    g_w = 1.0 / jnp.sum(jnp.exp(gl - gmax), axis=0, keepdims=True)
    g_idx = jnp.min(jnp.where(gl == gmax, unit, big), axis=0, keepdims=True) - GROUP_UNIT0

    e0 = g_idx * EXPERTS_PER_GROUP
    el = jnp.where((unit >= e0) & (unit < e0 + EXPERTS_PER_GROUP), logits, neg_inf)
    m1 = jnp.max(el, axis=0, keepdims=True)
    i1 = jnp.min(jnp.where(el == m1, unit, big), axis=0, keepdims=True)
    el2 = jnp.where(unit == i1, neg_inf, el)
    m2 = jnp.max(el2, axis=0, keepdims=True)
    i2 = jnp.min(jnp.where(el2 == m2, unit, big), axis=0, keepdims=True)
    e2 = jnp.exp(m2 - m1)
    den = 1.0 + e2
    w1 = (1.0 / den) * g_w
    w2 = (e2 / den) * g_w

    hit1 = unit == i1
    hit2 = unit == i2
    onehot = jnp.where(hit1, 1.0, jnp.where(hit2, 1.0, 0.0))
    c_idx = lax.broadcasted_iota(jnp.int32, (rows, rows), 0)
    r_idx = lax.broadcasted_iota(jnp.int32, (rows, rows), 1)
    earlier = jnp.where(c_idx < r_idx, 1.0, 0.0).astype(BF16)
    prefix = run_scr[...] + _dot(onehot.astype(BF16), earlier)
    rank1 = jnp.sum(jnp.where(hit1, prefix, 0.0), axis=0, keepdims=True)
    rank2 = jnp.sum(jnp.where(hit2, prefix, 0.0), axis=0, keepdims=True)
    run_scr[...] = run_scr[...] + jnp.sum(onehot, axis=1, keepdims=True)
    counts_ref[...] = run_scr[...]

    route_t = jnp.concatenate(
        [i1, i2, w1, w2, rank1, rank2, jnp.zeros((ROUTER_LANES - 6, rows), F32)], axis=0)
    route_t_ref[0] = route_t[:ROUTE_ROWS]
    route_ref[...] = route_t.T


def _merge_call(x2, ya, gb, yb, w_out, norm_g, wr_hi, wr_lo, b_router):
    t, d = x2.shape
    rows = MERGE_ROWS
    assert t % rows == 0 and d == TOKEN_SUBLANES * LANES
    n_tiles = t // rows
    row_spec = pl.BlockSpec((rows, d), lambda i: (i, 0))
    const2 = lambda i: (0, 0)
    return pl.pallas_call(
        _merge_kernel,
        grid=(n_tiles,),
        in_specs=[row_spec, row_spec, row_spec, row_spec,
                  pl.BlockSpec((d, d), const2),
                  pl.BlockSpec((1, d), const2),
                  pl.BlockSpec((ROUTER_UNITS, d), const2),
                  pl.BlockSpec((ROUTER_UNITS, d), const2),
                  pl.BlockSpec((ROUTER_UNITS, 1), const2)],
        out_specs=[row_spec,
                   pl.BlockSpec((rows * TOKEN_SUBLANES, LANES), lambda i: (i, 0)),
                   pl.BlockSpec((rows, ROUTER_LANES), lambda i: (i, 0)),
                   pl.BlockSpec((1, ROUTE_ROWS, rows), lambda i: (i, 0, 0)),
                   pl.BlockSpec((ROUTER_UNITS, 1), const2)],
        out_shape=[jax.ShapeDtypeStruct((t, d), F32),
                   jax.ShapeDtypeStruct((t * TOKEN_SUBLANES, LANES), F32),
                   jax.ShapeDtypeStruct((t, ROUTER_LANES), F32),
                   jax.ShapeDtypeStruct((n_tiles, ROUTE_ROWS, rows), F32),
                   jax.ShapeDtypeStruct((ROUTER_UNITS, 1), F32)],
        scratch_shapes=[pltpu.VMEM((ROUTER_UNITS, 1), F32)],
        compiler_params=pltpu.CompilerParams(
            dimension_semantics=("arbitrary",), vmem_limit_bytes=VMEM_LIMIT_BYTES),
        name="merge_outproj_router",
    )(x2, ya, gb, yb, w_out, norm_g, wr_hi, wr_lo, b_router)


def _token_rows(ref, token):
    return ref.at[pl.ds(pl.multiple_of(token * TOKEN_SUBLANES, TOKEN_SUBLANES), TOKEN_SUBLANES)]


def _dispatch_kernel(last_ref, nreal_ref, pos1_ref, pos2_ref, xn_ref, xs_hbm, zero_scr, sem):
    rows = pos1_ref.shape[2]
    tile = EXPERT_ROWS
    n_tiles = xs_hbm.shape[0] // (tile * TOKEN_SUBLANES)

    def zero_tile(j):
        start = pl.multiple_of(j * (tile * TOKEN_SUBLANES), tile * TOKEN_SUBLANES)
        return pltpu.make_async_copy(
            zero_scr, xs_hbm.at[pl.ds(start, tile * TOKEN_SUBLANES)], sem)

    @pl.when(pl.program_id(0) == 0)
    def _():
        zero_scr[...] = jnp.zeros_like(zero_scr)
        for e in range(N_EXPERTS):
            @pl.when(last_ref[e] >= 0)
            def _():
                zero_tile(last_ref[e]).start()

        def tail_start(j, carry):
            zero_tile(j).start()
            return carry

        lax.fori_loop(nreal_ref[0], n_tiles, tail_start, 0)

        for e in range(N_EXPERTS):
            @pl.when(last_ref[e] >= 0)
            def _():
                zero_tile(0).wait()

        def tail_wait(j, carry):
            zero_tile(0).wait()
            return carry

        lax.fori_loop(nreal_ref[0], n_tiles, tail_wait, 0)

    def issue(g, carry):
        for u in range(DMA_UNROLL):
            r = g * DMA_UNROLL + u
            src = _token_rows(xn_ref, r)
            pltpu.make_async_copy(src, _token_rows(xs_hbm, pos1_ref[0, 0, r]), sem).start(0)
            pltpu.make_async_copy(src, _token_rows(xs_hbm, pos2_ref[0, 0, r]), sem).start(1)
        return carry

    lax.fori_loop(0, rows // DMA_UNROLL, issue, 0)

    for _ in range(2):
        pltpu.make_async_copy(xn_ref, xs_hbm.at[pl.ds(0, rows * TOKEN_SUBLANES)], sem).wait()


def _dispatch_call(last_tile, n_real, pos1, pos2, xn, n_sorted_rows):
    n_steps, _, rows = pos1.shape
    smem_row = pl.BlockSpec((1, 1, rows), lambda i, lt, nr: (i, 0, 0), memory_space=pltpu.SMEM)
    return pl.pallas_call(
        _dispatch_kernel,
        grid_spec=pltpu.PrefetchScalarGridSpec(
            num_scalar_prefetch=2,
            grid=(n_steps,),
            in_specs=[smem_row, smem_row,
                      pl.BlockSpec((rows * TOKEN_SUBLANES, LANES), lambda i, lt, nr: (i, 0))],
            out_specs=pl.BlockSpec(memory_space=pl.ANY),
            scratch_shapes=[pltpu.VMEM((EXPERT_ROWS * TOKEN_SUBLANES, LANES), F32),
                            pltpu.SemaphoreType.DMA(())],
        ),
        out_shape=jax.ShapeDtypeStruct((n_sorted_rows * TOKEN_SUBLANES, LANES), F32),
        compiler_params=pltpu.CompilerParams(dimension_semantics=("arbitrary",)),
        name="moe_dispatch",
    )(last_tile, n_real, pos1, pos2, xn)


EXPERT_IN_SLOTS = 4
EXPERT_OUT_SLOTS = 3


def _expert_kernel(first_ref, end_ref, nreal_ref, xs_hbm, w1_ref, w3_ref, w2_ref, ys_hbm,
                   w1_scr, w3_scr, w2_scr, xbuf, ybuf, in_sems, out_sems):
    e = pl.program_id(0)
    tile_rows = EXPERT_ROWS * TOKEN_SUBLANES
    n_real = nreal_ref[0]
    n_tiles = xs_hbm.shape[0] // tile_rows

    def tile_of(ref, t):
        return ref.at[pl.ds(pl.multiple_of(t * tile_rows, tile_rows), tile_rows)]

    def in_copy(t):
        slot = t % EXPERT_IN_SLOTS
        return pltpu.make_async_copy(tile_of(xs_hbm, t), xbuf.at[slot], in_sems.at[slot])

    def out_copy(t):
        slot = t % EXPERT_OUT_SLOTS
        return pltpu.make_async_copy(ybuf.at[slot], tile_of(ys_hbm, t), out_sems.at[slot])

    @pl.when(e == 0)
    def _():
        for t in range(EXPERT_IN_SLOTS - 1):
            @pl.when(t < n_real)
            def _():
                in_copy(t).start()

    w1_scr[...] = w1_ref[...].astype(BF16)
    w3_scr[...] = w3_ref[...].astype(BF16)
    w2_scr[...] = w2_ref[...].astype(BF16)

    def tile(t, carry):
        ahead = t + EXPERT_IN_SLOTS - 1

        @pl.when(ahead < n_real)
        def _():
            in_copy(ahead).start()

        in_copy(t).wait()

        @pl.when(t >= EXPERT_OUT_SLOTS)
        def _():
            out_copy(t - EXPERT_OUT_SLOTS).wait()

        x = _load_token_major(xbuf.at[t % EXPERT_IN_SLOTS], EXPERT_ROWS).astype(BF16)
        a = _dot(x, w1_scr[...])
        b = _dot(x, w3_scr[...])
        hid = (a * _sigmoid(a)) * b
        _store_token_major(ybuf.at[t % EXPERT_OUT_SLOTS], _dot(hid.astype(BF16), w2_scr[...]))
        out_copy(t).start()
        return carry

    lax.fori_loop(first_ref[e], end_ref[e], tile, 0)

    @pl.when(e == pl.num_programs(0) - 1)
    def _():
        for back in range(EXPERT_OUT_SLOTS, 0, -1):
            @pl.when(n_real - back >= 0)
            def _():
                out_copy(n_real - back).wait()
        ybuf[0] = jnp.zeros(ybuf.shape[1:], F32)

        def tail_start(t, carry):
            pltpu.make_async_copy(ybuf.at[0], tile_of(ys_hbm, t), out_sems.at[0]).start()
            return carry

        def tail_wait(t, carry):
            pltpu.make_async_copy(ybuf.at[0], tile_of(ys_hbm, t), out_sems.at[0]).wait()
            return carry

        lax.fori_loop(n_real, n_tiles, tail_start, 0)
        lax.fori_loop(n_real, n_tiles, tail_wait, 0)


def _expert_call(first_tile, end_tile, n_real, xs, w1, w3, w2):
    n_experts, d, d_expert = w1.shape
    tile_rows = EXPERT_ROWS * TOKEN_SUBLANES
    per_expert = lambda e, f, n, nr: (e, 0, 0)
    return pl.pallas_call(
        _expert_kernel,
        grid_spec=pltpu.PrefetchScalarGridSpec(
            num_scalar_prefetch=3,
            grid=(n_experts,),
            in_specs=[pl.BlockSpec(memory_space=pl.ANY),
                      pl.BlockSpec((None, d, d_expert), per_expert),
                      pl.BlockSpec((None, d, d_expert), per_expert),
                      pl.BlockSpec((None, d_expert, d), per_expert)],
            out_specs=pl.BlockSpec(memory_space=pl.ANY),
            scratch_shapes=[pltpu.VMEM((d, d_expert), BF16), pltpu.VMEM((d, d_expert), BF16),
                            pltpu.VMEM((d_expert, d), BF16),
                            pltpu.VMEM((EXPERT_IN_SLOTS, tile_rows, LANES), F32),
                            pltpu.VMEM((EXPERT_OUT_SLOTS, tile_rows, LANES), F32),
                            pltpu.SemaphoreType.DMA((EXPERT_IN_SLOTS,)),
                            pltpu.SemaphoreType.DMA((EXPERT_OUT_SLOTS,))],
        ),
        out_shape=jax.ShapeDtypeStruct(xs.shape, F32),
        compiler_params=pltpu.CompilerParams(
            dimension_semantics=("arbitrary",), vmem_limit_bytes=VMEM_LIMIT_BYTES),
        name="moe_experts",
    )(first_tile, end_tile, n_real, xs, w1, w3, w2)


def _combine_kernel(p1_first, p2_first, p1_next, p2_next, ys_hbm, h_ref, route_ref, ng_ref,
                    out_ref, buf, sems):
    i = pl.program_id(0)
    n = pl.num_programs(0)
    rows = h_ref.shape[0]

    def issue(p1_ref, p2_ref, slot):
        def body(g, carry):
            for u in range(DMA_UNROLL):
                r = g * DMA_UNROLL + u
                for which, p_ref in ((0, p1_ref), (1, p2_ref)):
                    pltpu.make_async_copy(_token_rows(ys_hbm, p_ref[0, 0, r]),
                                          _token_rows(buf.at[2 * slot + which], r),
                                          sems.at[slot]).start(which)
            return carry
        lax.fori_loop(0, rows // DMA_UNROLL, body, 0)

    @pl.when(i == 0)
    def _():
        issue(p1_first, p2_first, 0)

    @pl.when(i + 1 < n)
    def _():
        issue(p1_next, p2_next, (i + 1) % 2)

    slot = i % 2
    for which in range(2):
        pltpu.make_async_copy(ys_hbm.at[pl.ds(0, rows * TOKEN_SUBLANES)],
                              buf.at[2 * slot + which], sems.at[slot]).wait()

    route = route_ref[...]
    w1 = route[:, ROUTE_W1:ROUTE_W1 + 1]
    w2 = route[:, ROUTE_W2:ROUTE_W2 + 1]
    y = (h_ref[...] + w1 * _load_token_major(buf.at[2 * slot], rows)
         + w2 * _load_token_major(buf.at[2 * slot + 1], rows))
    out_ref[...] = _rmsnorm(y, ng_ref[...])


def _combine_call(pos1, pos2, ys, h, route, norm_g):
    t, d = h.shape
    n_steps, _, rows = pos1.shape
    row_spec = pl.BlockSpec((rows, d), lambda i: (i, 0))
    first = pl.BlockSpec((1, 1, rows), lambda i: (0, 0, 0), memory_space=pltpu.SMEM)
    nxt = pl.BlockSpec((1, 1, rows), lambda i: (jnp.minimum(i + 1, n_steps - 1), 0, 0),
                       memory_space=pltpu.SMEM)
    return pl.pallas_call(
        _combine_kernel,
        grid=(n_steps,),
        in_specs=[first, first, nxt, nxt,
                  pl.BlockSpec(memory_space=pl.ANY),
                  row_spec,
                  pl.BlockSpec((rows, ROUTER_LANES), lambda i: (i, 0)),
                  pl.BlockSpec((1, d), lambda i: (0, 0))],
        out_specs=row_spec,
        out_shape=jax.ShapeDtypeStruct((t, d), F32),
        scratch_shapes=[pltpu.VMEM((4, rows * TOKEN_SUBLANES, LANES), F32),
                        pltpu.SemaphoreType.DMA((2,))],
        compiler_params=pltpu.CompilerParams(
            dimension_semantics=("arbitrary",), vmem_limit_bytes=VMEM_LIMIT_BYTES),
        name="moe_combine",
    )(pos1, pos2, pos1, pos2, ys, h, route, norm_g)


def _sparse_moe(xn, route, route_t, counts, h, w1, w3, w2, norm_g):
    t = h.shape[0]
    tile = EXPERT_ROWS
    n_tiles = (2 * t) // tile + N_EXPERTS
    expert = jnp.arange(N_EXPERTS, dtype=jnp.int32)
    counts = counts[:N_EXPERTS, 0].astype(jnp.int32)
    group_tiles = (counts + tile - 1) // tile
    end_tile = jnp.sum(jnp.where(expert[None, :] <= expert[:, None], group_tiles[None, :], 0), axis=1)
    first_tile = end_tile - group_tiles
    n_real = end_tile[-1:]
    last_tile = jnp.where(group_tiles > 0, end_tile - 1, -1)

    def positions(e_row, r_row):
        e = route_t[:, e_row, :].astype(jnp.int32)
        start = jnp.zeros_like(e)
        for k in range(N_EXPERTS):
            start = jnp.where(e == k, first_tile[k] * tile, start)
        return start + route_t[:, r_row, :].astype(jnp.int32)

    pos1 = positions(ROUTE_E1, ROUTE_R1)
    pos2 = positions(ROUTE_E2, ROUTE_R2)
    per_step = lambda pos, rows: pos.reshape(t // rows, 1, rows)
    xs = _dispatch_call(last_tile, n_real, per_step(pos1, DISPATCH_ROWS),
                        per_step(pos2, DISPATCH_ROWS), xn, n_tiles * tile)
    ys = _expert_call(first_tile, end_tile, n_real, xs, w1, w3, w2)
    return _combine_call(per_step(pos1, COMBINE_ROWS), per_step(pos2, COMBINE_ROWS), ys, h,
                         route, norm_g)


def _layer(h, norm_mix_g, w_in, b_gates, gmlp_ln_g, gmlp_ln_b, w_spatial, b_spatial, bias_tiles,
           w_out, norm_ffn_g, w_group_router, b_group_router, w_expert_router, b_expert_router,
           w1, w3, w2, norm_out_g):
    b, s, d = h.shape
    t = b * s
    nb = s // MOBA_BLOCK
    x2 = h.reshape(t, d)
    row = lambda v: v.reshape(1, -1)

    ya, qT, k, vT, gb, kmean = _proj_call(
        x2, b, row(norm_mix_g), w_in.astype(BF16), row(b_gates), row(gmlp_ln_g), row(gmlp_ln_b),
        w_spatial, b_spatial[:, :, None])

    km = jnp.transpose(kmean.reshape(b, nb, ATT_HEADS, HEAD_DIM), (0, 2, 1, 3))
    km_hi = km.astype(BF16)
    km_lo = (km - km_hi.astype(F32)).astype(BF16)
    yb = _attn_call(qT, k, vT, km_hi, km_lo, bias_tiles)

    w_router = jnp.concatenate(
        [jnp.transpose(w_expert_router, (0, 2, 1)).reshape(N_EXPERTS, d), w_group_router.T,
         jnp.zeros((ROUTER_UNITS - N_EXPERTS - N_GROUPS, d), F32)], axis=0)
    b_router = jnp.concatenate(
        [b_expert_router.reshape(-1), b_group_router,
         jnp.zeros((ROUTER_UNITS - N_EXPERTS - N_GROUPS,), F32)]).reshape(ROUTER_UNITS, 1)
    wr_hi = w_router.astype(BF16)
    wr_lo = (w_router - wr_hi.astype(F32)).astype(BF16)
    h2, xn, route, route_t, counts = _merge_call(
        x2, ya, gb, yb, w_out.astype(BF16), row(norm_ffn_g), wr_hi, wr_lo, b_router)

    out = _sparse_moe(xn, route, route_t, counts, h2, w1, w3, w2, row(norm_out_g))
    return out.reshape(b, s, d)


def kernel(x, norm_mix_g, w_in, b_gates, gmlp_ln_g, gmlp_ln_b, w_spatial, b_spatial, rel_bias, w_out, norm_ffn_g, w_group_router, b_group_router, w_expert_router, b_expert_router, w1, w3, w2, norm_final_g):
    depth = w_in.shape[0]
    assert depth == 1, "the final rmsnorm is fused into the last layer's combine kernel"
    bias_tiles = _bias_call(rel_bias)
    return _layer(x, norm_mix_g[0], w_in[0], b_gates[0], gmlp_ln_g[0], gmlp_ln_b[0], w_spatial[0],
                  b_spatial[0], bias_tiles, w_out[0], norm_ffn_g[0], w_group_router[0],
                  b_group_router[0], w_expert_router[0], b_expert_router[0], w1[0], w3[0], w2[0],
                  norm_final_g)
```

```python
import functools
import math

import numpy as np
import jax
import jax.numpy as jnp
from jax import lax
from jax.experimental import pallas as pl
from jax.experimental.pallas import tpu as pltpu

F32 = jnp.float32
BF16 = jnp.bfloat16

D_MODEL = 1024
NORM_EPS = 1e-6
GMLP_GROUPS = 8
GMLP_CHUNK = 128
ATT_HEADS = 8
HEAD_DIM = 128
MOBA_BLOCK = 256
MOBA_TOPK = 3
REL_BUCKETS = 32
REL_MAX_DIST = 128
N_GROUPS = 4
EXPERTS_PER_GROUP = 8
N_EXPERTS = N_GROUPS * EXPERTS_PER_GROUP
D_EXPERT = 256
N_SEGMENTS = 7

LANES = 128
TOKEN_SUBLANES = 8
ROUTE_ROWS = 8
VMEM_LIMIT_BYTES = 56 * 1024 * 1024

SQRT_HALF = math.sqrt(0.5)
LOG2E = math.log2(math.e)
SCORE_SCALE2 = (HEAD_DIM ** -0.5) * LOG2E
MASK_NEG = -(2.0 ** 100)
BF16_SUBLANES = 16
V_ROWS = HEAD_DIM + BF16_SUBLANES
ROUTER_LANES = LANES
ROUTER_UNITS = -(-(N_EXPERTS + N_GROUPS) // BF16_SUBLANES) * BF16_SUBLANES
GROUP_UNIT0 = N_EXPERTS

PROJ_ROWS = 512
MERGE_ROWS = 1024
EXPERT_ROWS = 256
DISPATCH_ROWS = 2048
COMBINE_ROWS = 256
DMA_UNROLL = 8


def _rmsnorm(x, g):
    return x * lax.rsqrt(jnp.mean(x * x, axis=-1, keepdims=True) + NORM_EPS) * g


def _gelu(a):
    return 0.5 * a * (1.0 + lax.erf(a * SQRT_HALF))


def _sigmoid(a):
    return 1.0 / (1.0 + jnp.exp(-a))


def _dot(a, b):
    return jnp.dot(a, b, preferred_element_type=F32)


def _proj_kernel(x_ref, ng_ref, w_ref, bg_ref, lng_ref, lnb_ref, ws_ref, bs_ref,
                 ya_ref, qT_ref, k_ref, vT_ref, gb_ref, kmean_ref,
                 xn_scr, vln_scr, mix_scr):
    rows = x_ref.shape[0]
    d = D_MODEL
    xn_scr[...] = _rmsnorm(x_ref[...], ng_ref[...]).astype(BF16)

    def seg(i):
        return _dot(xn_scr[...], w_ref[:, i * d:(i + 1) * d])

    hd = HEAD_DIM

    v = _gelu(seg(1))
    mu = jnp.mean(v, axis=-1, keepdims=True)
    vc = v - mu
    var = jnp.mean(vc * vc, axis=-1, keepdims=True)
    vln_scr[...] = (vc * lax.rsqrt(var + NORM_EPS) * lng_ref[...] + lnb_ref[...]).astype(BF16)

    mix_scr[...] = _gelu(seg(0)) * _sigmoid(seg(5) + bg_ref[:, :d])

    q = seg(2) * SCORE_SCALE2
    for h in range(ATT_HEADS):
        qT_ref[h] = q[:, h * hd:(h + 1) * hd].T.astype(BF16)

    t_idx = lax.broadcasted_iota(jnp.int32, (GMLP_CHUNK, GMLP_CHUNK), 0)
    s_idx = lax.broadcasted_iota(jnp.int32, (GMLP_CHUNK, GMLP_CHUNK), 1)
    causal = t_idx >= s_idx
    gd = d // GMLP_GROUPS
    n_chunks = rows // GMLP_CHUNK
    for g in range(GMLP_GROUPS):
        ws = jnp.where(causal, ws_ref[g], 0.0).astype(BF16)
        bias = bs_ref[g]
        vg = jnp.concatenate(
            [vln_scr[c * GMLP_CHUNK:(c + 1) * GMLP_CHUNK, g * gd:(g + 1) * gd]
             for c in range(n_chunks)], axis=1)
        mixed = _dot(ws, vg)
        for c in range(n_chunks):
            blk_rows = slice(c * GMLP_CHUNK, (c + 1) * GMLP_CHUNK)
            blk_cols = slice(g * gd, (g + 1) * gd)
            ya_ref[blk_rows, blk_cols] = (
                mix_scr[blk_rows, blk_cols] * (mixed[:, c * gd:(c + 1) * gd] + bias)).astype(BF16)

    v = seg(4)
    for blk in range(rows // MOBA_BLOCK):
        r0 = blk * MOBA_BLOCK
        for h in range(ATT_HEADS):
            vT_ref[h, blk, :hd, :] = v[r0:r0 + MOBA_BLOCK, h * hd:(h + 1) * hd].T.astype(BF16)
            vT_ref[h, blk, hd:, :] = jnp.ones((V_ROWS - hd, MOBA_BLOCK), BF16)

    gb_ref[...] = _sigmoid(seg(6) + bg_ref[:, d:]).astype(BF16)

    k = seg(3)
    k_ref[...] = k.astype(BF16)
    for blk in range(rows // MOBA_BLOCK):
        r0 = blk * MOBA_BLOCK
        kmean_ref[0, blk:blk + 1, :] = jnp.mean(k[r0:r0 + MOBA_BLOCK, :], axis=0, keepdims=True)


def _proj_call(x2, batch, norm_g, w_in, b_gates, ln_g, ln_b, w_spatial, b_spatial):
    t, d = x2.shape
    rows = PROJ_ROWS
    seq = t // batch
    assert seq % rows == 0 and rows % MOBA_BLOCK == 0 and rows % GMLP_CHUNK == 0
    n_tiles = t // rows
    tiles_per_seq = seq // rows
    blocks_per_tile = rows // MOBA_BLOCK
    nb = seq // MOBA_BLOCK
    row_spec = pl.BlockSpec((rows, d), lambda i: (i, 0))
    const2 = lambda i: (0, 0)
    const3 = lambda i: (0, 0, 0)
    act = jax.ShapeDtypeStruct((t, d), BF16)
    qT_spec = pl.BlockSpec((None, ATT_HEADS, HEAD_DIM, rows),
                           lambda i: (i // tiles_per_seq, 0, 0, i % tiles_per_seq))
    vT_spec = pl.BlockSpec((None, ATT_HEADS, blocks_per_tile, V_ROWS, MOBA_BLOCK),
                           lambda i: (i // tiles_per_seq, 0, i % tiles_per_seq, 0, 0))
    return pl.pallas_call(
        _proj_kernel,
        grid=(n_tiles,),
        in_specs=[
            row_spec,
            pl.BlockSpec((1, d), const2),
            pl.BlockSpec((d, N_SEGMENTS * d), const2, pipeline_mode=pl.Buffered(1)),
            pl.BlockSpec((1, 2 * d), const2),
            pl.BlockSpec((1, d), const2),
            pl.BlockSpec((1, d), const2),
            pl.BlockSpec((GMLP_GROUPS, GMLP_CHUNK, GMLP_CHUNK), const3),
            pl.BlockSpec((GMLP_GROUPS, GMLP_CHUNK, 1), const3),
        ],
        out_specs=[row_spec, qT_spec, row_spec, vT_spec, row_spec,
                   pl.BlockSpec((1, blocks_per_tile, d), lambda i: (i, 0, 0))],
        out_shape=[act,
                   jax.ShapeDtypeStruct((batch, ATT_HEADS, HEAD_DIM, seq), BF16),
                   act,
                   jax.ShapeDtypeStruct((batch, ATT_HEADS, nb, V_ROWS, MOBA_BLOCK), BF16),
                   act,
                   jax.ShapeDtypeStruct((n_tiles, blocks_per_tile, d), F32)],
        scratch_shapes=[pltpu.VMEM((rows, d), BF16), pltpu.VMEM((rows, d), BF16),
                        pltpu.VMEM((rows, d), F32)],
        compiler_params=pltpu.CompilerParams(
            dimension_semantics=("arbitrary",), vmem_limit_bytes=VMEM_LIMIT_BYTES),
        name="proj_gmlp",
    )(x2, norm_g, w_in, b_gates, ln_g, ln_b, w_spatial, b_spatial)


def _t5_bucket_np(n):
    n = np.maximum(n, 0)
    max_exact = REL_BUCKETS // 2
    nf = np.maximum(n, max_exact).astype(np.float32)
    large = max_exact + (np.log(nf / max_exact) / math.log(REL_MAX_DIST / max_exact)
                         * (REL_BUCKETS - max_exact)).astype(np.int32)
    large = np.minimum(large, REL_BUCKETS - 1)
    return np.where(n < max_exact, n, large).astype(np.int32)


def _bucket_tiles():
    kpos = np.arange(MOBA_BLOCK, dtype=np.int32)[:, None]
    qpos = np.arange(MOBA_BLOCK, dtype=np.int32)[None, :]
    rel = qpos - kpos
    return np.stack([_t5_bucket_np(rel), _t5_bucket_np(rel + MOBA_BLOCK)])


def _bias_kernel(relb_ref, bucket_ref, out_ref):
    h = pl.program_id(0)
    far = relb_ref[REL_BUCKETS - 1, h]
    k_idx = lax.broadcasted_iota(jnp.int32, (MOBA_BLOCK, MOBA_BLOCK), 0)
    q_idx = lax.broadcasted_iota(jnp.int32, (MOBA_BLOCK, MOBA_BLOCK), 1)
    for tile in range(2):
        bucket = bucket_ref[tile]
        bias = jnp.zeros((MOBA_BLOCK, MOBA_BLOCK), F32)
        for b in range(REL_BUCKETS):
            bias = jnp.where(bucket == b, relb_ref[b, h], bias)
        bias2 = (bias - far) * LOG2E
        if tile == BIAS_OWN:
            bias2 = jnp.where(q_idx >= k_idx, bias2, MASK_NEG)
        out_ref[0, tile] = bias2


def _bias_call(rel_bias):
    buckets = _bucket_tiles()
    return pl.pallas_call(
        _bias_kernel,
        grid=(ATT_HEADS,),
        in_specs=[pl.BlockSpec(memory_space=pltpu.SMEM),
                  pl.BlockSpec((2, MOBA_BLOCK, MOBA_BLOCK), lambda h: (0, 0, 0))],
        out_specs=pl.BlockSpec((1, N_BIAS_TILES, MOBA_BLOCK, MOBA_BLOCK), lambda h: (h, 0, 0, 0)),
        out_shape=jax.ShapeDtypeStruct((ATT_HEADS, N_BIAS_TILES, MOBA_BLOCK, MOBA_BLOCK), F32),
        compiler_params=pltpu.CompilerParams(dimension_semantics=("arbitrary",)),
        name="t5_bias_tiles",
    )(rel_bias, jnp.asarray(buckets))


BIAS_OWN, BIAS_PREV = range(2)
N_BIAS_TILES = 2


def _attn_kernel(qT_ref, k_ref, vT_ref, kmh_ref, kml_ref, oh_ref, bias_ref, o_ref,
                 qp_scr, m_scr, acc_scr):
    i = pl.program_id(1)
    nh, nb = kmh_ref.shape[0], kmh_ref.shape[1]
    blk = MOBA_BLOCK
    hd = HEAD_DIM

    n_idx = lax.broadcasted_iota(jnp.int32, (nb, blk), 0)
    past = n_idx < i
    for h in range(nh):
        qT = qT_ref[h]
        gate = _dot(kmh_ref[h], qT) + _dot(kml_ref[h], qT)
        gate = jnp.where(past, gate, -jnp.inf)
        rank = jnp.zeros((nb, blk), F32)
        for m in range(nb):
            row = gate[m:m + 1, :]
            beats = (row > gate) | ((row == gate) & (m < n_idx))
            rank = rank + jnp.where(beats, 1.0, 0.0)
        keep = (past & (rank < MOBA_TOPK)) | (n_idx == i)
        sel = jnp.where(keep, 0.0, MASK_NEG)
        sel = jnp.concatenate([sel, jnp.zeros((hd - nb, blk), F32)], axis=0)
        qp_scr[h] = jnp.concatenate([qT, sel.astype(BF16)], axis=0)

    def scores(h, j):
        r0 = pl.multiple_of(j * blk, blk)
        kp = jnp.concatenate([k_ref[pl.ds(r0, blk), h * hd:(h + 1) * hd], oh_ref[j]], axis=1)
        return _dot(kp, qp_scr[h])

    def fold(blocks, bias_tiles, first):
        ss = []
        for h in range(nh):
            parts = []
            for j, tile in zip(blocks, bias_tiles):
                s = scores(h, j)
                parts.append((s if tile is None else s + bias_ref[h, tile]).astype(BF16))
            ss.append(parts)
        ps, alphas = [], []
        for h in range(nh):
            m_new = functools.reduce(
                jnp.maximum, [jnp.max(s, axis=0, keepdims=True) for s in ss[h]]).astype(F32)
            if not first:
                m_old = m_scr[h]
                m_new = jnp.maximum(m_old, m_new)
                alphas.append(jnp.exp2(m_old - m_new))
            m_scr[h] = m_new
            ps.append([jnp.exp2(s - m_new.astype(BF16)) for s in ss[h]])
        for h in range(nh):
            pv = functools.reduce(
                lambda a, b: a + b, [_dot(vT_ref[h, j], x) for j, x in zip(blocks, ps[h])])
            acc_scr[h] = pv if first else alphas[h] * acc_scr[h] + pv

    @pl.when(i == 0)
    def _():
        fold([i], [BIAS_OWN], first=True)

    @pl.when(i >= 1)
    def _():
        fold([i, i - 1], [BIAS_OWN, BIAS_PREV], first=True)

    n_far = i - 1

    def far_pair(p, carry):
        fold([2 * p, 2 * p + 1], [None, None], first=False)
        return carry

    lax.fori_loop(0, n_far // 2, far_pair, 0)

    @pl.when((n_far >= 1) & (n_far % 2 == 1))
    def _():
        fold([n_far - 1], [None], first=False)

    for h in range(nh):
        y = acc_scr[h, :hd, :] * (1.0 / acc_scr[h, hd:hd + 1, :])
        o_ref[:, h * hd:(h + 1) * hd] = y.T.astype(o_ref.dtype)


def _attn_call(qT, k2, vT, km_hi, km_lo, bias_tiles):
    b, h, nb, v_rows, blk = vT.shape
    hd = HEAD_DIM
    s = nb * blk
    onehot = np.zeros((nb, blk, LANES), np.float32)
    for j in range(nb):
        onehot[j, :, j] = 1.0
    once = pl.Buffered(1)
    return pl.pallas_call(
        _attn_kernel,
        grid=(b, nb),
        in_specs=[
            pl.BlockSpec((None, h, hd, blk), lambda bi, i: (bi, 0, 0, i)),
            pl.BlockSpec((s, h * hd), lambda bi, i: (bi, 0)),
            pl.BlockSpec((None, h, nb, v_rows, blk), lambda bi, i: (bi, 0, 0, 0, 0)),
            pl.BlockSpec((None, h, nb, hd), lambda bi, i: (bi, 0, 0, 0)),
            pl.BlockSpec((None, h, nb, hd), lambda bi, i: (bi, 0, 0, 0)),
            pl.BlockSpec((nb, blk, LANES), lambda bi, i: (0, 0, 0), pipeline_mode=once),
            pl.BlockSpec((h, N_BIAS_TILES, blk, blk), lambda bi, i: (0, 0, 0, 0),
                         pipeline_mode=once),
        ],
        out_specs=pl.BlockSpec((blk, h * hd), lambda bi, i: (bi * nb + i, 0)),
        out_shape=jax.ShapeDtypeStruct((b * s, h * hd), BF16),
        scratch_shapes=[pltpu.VMEM((h, 2 * hd, blk), BF16), pltpu.VMEM((h, 1, blk), F32),
                        pltpu.VMEM((h, v_rows, blk), F32)],
        compiler_params=pltpu.CompilerParams(
            dimension_semantics=("arbitrary", "arbitrary"),
            vmem_limit_bytes=VMEM_LIMIT_BYTES),
        name="moba_attention",
    )(qT, k2, vT, km_hi, km_lo, jnp.asarray(onehot, BF16), bias_tiles)


ROUTE_E1, ROUTE_E2, ROUTE_W1, ROUTE_W2, ROUTE_R1, ROUTE_R2 = range(6)


def _dot_nt(a, b):
    return lax.dot_general(a, b, (((1,), (1,)), ((), ())), preferred_element_type=F32)


def _store_token_major(ref, x):
    rows = x.shape[0]
    for s in range(TOKEN_SUBLANES):
        ref[pl.ds(s, rows, stride=TOKEN_SUBLANES), :] = x[:, s * LANES:(s + 1) * LANES]


def _load_token_major(ref, rows):
    return jnp.concatenate(
        [ref[pl.ds(s, rows, stride=TOKEN_SUBLANES), :] for s in range(TOKEN_SUBLANES)], axis=1)


def _merge_kernel(x_ref, ya_ref, gb_ref, yb_ref, wo_ref, ng_ref, wrh_ref, wrl_ref, br_ref,
                  h_ref, xn_ref, route_ref, route_t_ref, counts_ref, run_scr):
    @pl.when(pl.program_id(0) == 0)
    def _():
        run_scr[...] = jnp.zeros_like(run_scr)

    f = lambda r: r[...].astype(F32)
    mix = (f(ya_ref) + f(gb_ref) * f(yb_ref)).astype(BF16)
    h = x_ref[...] + _dot(mix, wo_ref[...])
    h_ref[...] = h
    xn = _rmsnorm(h, ng_ref[...])
    _store_token_major(xn_ref, xn)
    rows = xn.shape[0]

    x_hi = xn.astype(BF16)
    x_lo = (xn - x_hi.astype(F32)).astype(BF16)
    logits = (_dot_nt(wrh_ref[...], x_hi) + _dot_nt(wrh_ref[...], x_lo)
              + _dot_nt(wrl_ref[...], x_hi) + br_ref[...])
    unit = lax.broadcasted_iota(jnp.int32, logits.shape, 0).astype(F32)
    big = float(ROUTER_UNITS)
    neg_inf = -jnp.inf

    gl = jnp.where((unit >= GROUP_UNIT0) & (unit < GROUP_UNIT0 + N_GROUPS), logits, neg_inf)
    gmax = jnp.max(gl, axis=0, keepdims=True)
    g_w = 1.0 / jnp.sum(jnp.exp(gl - gmax), axis=0, keepdims=True)
    g_idx = jnp.min(jnp.where(gl == gmax, unit, big), axis=0, keepdims=True) - GROUP_UNIT0

    e0 = g_idx * EXPERTS_PER_GROUP
    el = jnp.where((unit >= e0) & (unit < e0 + EXPERTS_PER_GROUP), logits, neg_inf)
    m1 = jnp.max(el, axis=0, keepdims=True)
    i1 = jnp.min(jnp.where(el == m1, unit, big), axis=0, keepdims=True)
    el2 = jnp.where(unit == i1, neg_inf, el)
    m2 = jnp.max(el2, axis=0, keepdims=True)
    i2 = jnp.min(jnp.where(el2 == m2, unit, big), axis=0, keepdims=True)
    e2 = jnp.exp(m2 - m1)
    den = 1.0 + e2
    w1 = (1.0 / den) * g_w
    w2 = (e2 / den) * g_w

    hit1 = unit == i1
    hit2 = unit == i2
    onehot = jnp.where(hit1, 1.0, jnp.where(hit2, 1.0, 0.0))
    c_idx = lax.broadcasted_iota(jnp.int32, (rows, rows), 0)
    r_idx = lax.broadcasted_iota(jnp.int32, (rows, rows), 1)
    earlier = jnp.where(c_idx < r_idx, 1.0, 0.0).astype(BF16)
    prefix = run_scr[...] + _dot(onehot.astype(BF16), earlier)
    rank1 = jnp.sum(jnp.where(hit1, prefix, 0.0), axis=0, keepdims=True)
    rank2 = jnp.sum(jnp.where(hit2, prefix, 0.0), axis=0, keepdims=True)
    run_scr[...] = run_scr[...] + jnp.sum(onehot, axis=1, keepdims=True)
    counts_ref[...] = run_scr[...]

    route_t = jnp.concatenate(
        [i1, i2, w1, w2, rank1, rank2, jnp.zeros((ROUTER_LANES - 6, rows), F32)], axis=0)
    route_t_ref[0] = route_t[:ROUTE_ROWS]
    route_ref[...] = route_t.T


def _merge_call(x2, ya, gb, yb, w_out, norm_g, wr_hi, wr_lo, b_router):
    t, d = x2.shape
    rows = MERGE_ROWS
    assert t % rows == 0 and d == TOKEN_SUBLANES * LANES
    n_tiles = t // rows
    row_spec = pl.BlockSpec((rows, d), lambda i: (i, 0))
    const2 = lambda i: (0, 0)
    return pl.pallas_call(
        _merge_kernel,
        grid=(n_tiles,),
        in_specs=[row_spec, row_spec, row_spec, row_spec,
                  pl.BlockSpec((d, d), const2),
                  pl.BlockSpec((1, d), const2),
                  pl.BlockSpec((ROUTER_UNITS, d), const2),
                  pl.BlockSpec((ROUTER_UNITS, d), const2),
                  pl.BlockSpec((ROUTER_UNITS, 1), const2)],
        out_specs=[row_spec,
                   pl.BlockSpec((rows * TOKEN_SUBLANES, LANES), lambda i: (i, 0)),
                   pl.BlockSpec((rows, ROUTER_LANES), lambda i: (i, 0)),
                   pl.BlockSpec((1, ROUTE_ROWS, rows), lambda i: (i, 0, 0)),
                   pl.BlockSpec((ROUTER_UNITS, 1), const2)],
        out_shape=[jax.ShapeDtypeStruct((t, d), F32),
                   jax.ShapeDtypeStruct((t * TOKEN_SUBLANES, LANES), F32),
                   jax.ShapeDtypeStruct((t, ROUTER_LANES), F32),
                   jax.ShapeDtypeStruct((n_tiles, ROUTE_ROWS, rows), F32),
                   jax.ShapeDtypeStruct((ROUTER_UNITS, 1), F32)],
        scratch_shapes=[pltpu.VMEM((ROUTER_UNITS, 1), F32)],
        compiler_params=pltpu.CompilerParams(
            dimension_semantics=("arbitrary",), vmem_limit_bytes=VMEM_LIMIT_BYTES),
        name="merge_outproj_router",
    )(x2, ya, gb, yb, w_out, norm_g, wr_hi, wr_lo, b_router)


def _token_rows(ref, token):
    return ref.at[pl.ds(pl.multiple_of(token * TOKEN_SUBLANES, TOKEN_SUBLANES), TOKEN_SUBLANES)]


def _dispatch_kernel(last_ref, nreal_ref, pos1_ref, pos2_ref, xn_ref, xs_hbm, zero_scr, sem):
    rows = pos1_ref.shape[2]
    tile = EXPERT_ROWS
    n_tiles = xs_hbm.shape[0] // (tile * TOKEN_SUBLANES)

    def zero_tile(j):
        start = pl.multiple_of(j * (tile * TOKEN_SUBLANES), tile * TOKEN_SUBLANES)
        return pltpu.make_async_copy(
            zero_scr, xs_hbm.at[pl.ds(start, tile * TOKEN_SUBLANES)], sem)

    @pl.when(pl.program_id(0) == 0)
    def _():
        zero_scr[...] = jnp.zeros_like(zero_scr)
        for e in range(N_EXPERTS):
            @pl.when(last_ref[e] >= 0)
            def _():
                zero_tile(last_ref[e]).start()

        def tail_start(j, carry):
            zero_tile(j).start()
            return carry

        lax.fori_loop(nreal_ref[0], n_tiles, tail_start, 0)

        for e in range(N_EXPERTS):
            @pl.when(last_ref[e] >= 0)
            def _():
                zero_tile(0).wait()

        def tail_wait(j, carry):
            zero_tile(0).wait()
            return carry

        lax.fori_loop(nreal_ref[0], n_tiles, tail_wait, 0)

    def issue(g, carry):
        for u in range(DMA_UNROLL):
            r = g * DMA_UNROLL + u
            src = _token_rows(xn_ref, r)
            pltpu.make_async_copy(src, _token_rows(xs_hbm, pos1_ref[0, 0, r]), sem).start(0)
            pltpu.make_async_copy(src, _token_rows(xs_hbm, pos2_ref[0, 0, r]), sem).start(1)
        return carry

    lax.fori_loop(0, rows // DMA_UNROLL, issue, 0)

    for _ in range(2):
        pltpu.make_async_copy(xn_ref, xs_hbm.at[pl.ds(0, rows * TOKEN_SUBLANES)], sem).wait()


def _dispatch_call(last_tile, n_real, pos1, pos2, xn, n_sorted_rows):
    n_steps, _, rows = pos1.shape
    smem_row = pl.BlockSpec((1, 1, rows), lambda i, lt, nr: (i, 0, 0), memory_space=pltpu.SMEM)
    return pl.pallas_call(
        _dispatch_kernel,
        grid_spec=pltpu.PrefetchScalarGridSpec(
            num_scalar_prefetch=2,
            grid=(n_steps,),
            in_specs=[smem_row, smem_row,
                      pl.BlockSpec((rows * TOKEN_SUBLANES, LANES), lambda i, lt, nr: (i, 0))],
            out_specs=pl.BlockSpec(memory_space=pl.ANY),
            scratch_shapes=[pltpu.VMEM((EXPERT_ROWS * TOKEN_SUBLANES, LANES), F32),
                            pltpu.SemaphoreType.DMA(())],
        ),
        out_shape=jax.ShapeDtypeStruct((n_sorted_rows * TOKEN_SUBLANES, LANES), F32),
        compiler_params=pltpu.CompilerParams(dimension_semantics=("arbitrary",)),
        name="moe_dispatch",
    )(last_tile, n_real, pos1, pos2, xn)


EXPERT_IN_SLOTS = 6
EXPERT_OUT_SLOTS = 4


def _expert_kernel(first_ref, end_ref, nreal_ref, xs_hbm, w1_ref, w3_ref, w2_ref, ys_hbm,
                   w1_scr, w3_scr, w2_scr, xbuf, ybuf, in_sems, out_sems):
    e = pl.program_id(0)
    tile_rows = EXPERT_ROWS * TOKEN_SUBLANES
    n_real = nreal_ref[0]
    n_tiles = xs_hbm.shape[0] // tile_rows

    def tile_of(ref, t):
        return ref.at[pl.ds(pl.multiple_of(t * tile_rows, tile_rows), tile_rows)]

    def in_copy(t):
        slot = t % EXPERT_IN_SLOTS
        return pltpu.make_async_copy(tile_of(xs_hbm, t), xbuf.at[slot], in_sems.at[slot])

    def out_copy(t):
        slot = t % EXPERT_OUT_SLOTS
        return pltpu.make_async_copy(ybuf.at[slot], tile_of(ys_hbm, t), out_sems.at[slot])

    @pl.when(e == 0)
    def _():
        for t in range(EXPERT_IN_SLOTS - 1):
            @pl.when(t < n_real)
            def _():
                in_copy(t).start()

    w1_scr[...] = w1_ref[...].astype(BF16)
    w3_scr[...] = w3_ref[...].astype(BF16)
    w2_scr[...] = w2_ref[...].astype(BF16)

    def tile(t, carry):
        ahead = t + EXPERT_IN_SLOTS - 1

        @pl.when(ahead < n_real)
        def _():
            in_copy(ahead).start()

        in_copy(t).wait()

        @pl.when(t >= EXPERT_OUT_SLOTS)
        def _():
            out_copy(t - EXPERT_OUT_SLOTS).wait()

        x = _load_token_major(xbuf.at[t % EXPERT_IN_SLOTS], EXPERT_ROWS).astype(BF16)
        a = _dot(x, w1_scr[...])
        b = _dot(x, w3_scr[...])
        hid = (a * _sigmoid(a)) * b
        _store_token_major(ybuf.at[t % EXPERT_OUT_SLOTS], _dot(hid.astype(BF16), w2_scr[...]))
        out_copy(t).start()
        return carry

    lax.fori_loop(first_ref[e], end_ref[e], tile, 0)

    @pl.when(e == pl.num_programs(0) - 1)
    def _():
        for back in range(EXPERT_OUT_SLOTS, 0, -1):
            @pl.when(n_real - back >= 0)
            def _():
                out_copy(n_real - back).wait()
        ybuf[0] = jnp.zeros(ybuf.shape[1:], F32)

        def tail_start(t, carry):
            pltpu.make_async_copy(ybuf.at[0], tile_of(ys_hbm, t), out_sems.at[0]).start()
            return carry

        def tail_wait(t, carry):
            pltpu.make_async_copy(ybuf.at[0], tile_of(ys_hbm, t), out_sems.at[0]).wait()
            return carry

        lax.fori_loop(n_real, n_tiles, tail_start, 0)
        lax.fori_loop(n_real, n_tiles, tail_wait, 0)


def _expert_call(first_tile, end_tile, n_real, xs, w1, w3, w2):
    n_experts, d, d_expert = w1.shape
    tile_rows = EXPERT_ROWS * TOKEN_SUBLANES
    per_expert = lambda e, f, n, nr: (e, 0, 0)
    return pl.pallas_call(
        _expert_kernel,
        grid_spec=pltpu.PrefetchScalarGridSpec(
            num_scalar_prefetch=3,
            grid=(n_experts,),
            in_specs=[pl.BlockSpec(memory_space=pl.ANY),
                      pl.BlockSpec((None, d, d_expert), per_expert),
                      pl.BlockSpec((None, d, d_expert), per_expert),
                      pl.BlockSpec((None, d_expert, d), per_expert)],
            out_specs=pl.BlockSpec(memory_space=pl.ANY),
            scratch_shapes=[pltpu.VMEM((d, d_expert), BF16), pltpu.VMEM((d, d_expert), BF16),
                            pltpu.VMEM((d_expert, d), BF16),
                            pltpu.VMEM((EXPERT_IN_SLOTS, tile_rows, LANES), F32),
                            pltpu.VMEM((EXPERT_OUT_SLOTS, tile_rows, LANES), F32),
                            pltpu.SemaphoreType.DMA((EXPERT_IN_SLOTS,)),
                            pltpu.SemaphoreType.DMA((EXPERT_OUT_SLOTS,))],
        ),
        out_shape=jax.ShapeDtypeStruct(xs.shape, F32),
        compiler_params=pltpu.CompilerParams(
            dimension_semantics=("arbitrary",), vmem_limit_bytes=VMEM_LIMIT_BYTES),
        name="moe_experts",
    )(first_tile, end_tile, n_real, xs, w1, w3, w2)


def _combine_kernel(p1_first, p2_first, p1_next, p2_next, ys_hbm, h_ref, route_ref, ng_ref,
                    out_ref, buf, sems):
    i = pl.program_id(0)
    n = pl.num_programs(0)
    rows = h_ref.shape[0]

    def issue(p1_ref, p2_ref, slot):
        def body(g, carry):
            for u in range(DMA_UNROLL):
                r = g * DMA_UNROLL + u
                for which, p_ref in ((0, p1_ref), (1, p2_ref)):
                    pltpu.make_async_copy(_token_rows(ys_hbm, p_ref[0, 0, r]),
                                          _token_rows(buf.at[2 * slot + which], r),
                                          sems.at[slot]).start(which)
            return carry
        lax.fori_loop(0, rows // DMA_UNROLL, body, 0)

    @pl.when(i == 0)
    def _():
        issue(p1_first, p2_first, 0)

    @pl.when(i + 1 < n)
    def _():
        issue(p1_next, p2_next, (i + 1) % 2)

    slot = i % 2
    for which in range(2):
        pltpu.make_async_copy(ys_hbm.at[pl.ds(0, rows * TOKEN_SUBLANES)],
                              buf.at[2 * slot + which], sems.at[slot]).wait()

    route = route_ref[...]
    w1 = route[:, ROUTE_W1:ROUTE_W1 + 1]
    w2 = route[:, ROUTE_W2:ROUTE_W2 + 1]
    y = (h_ref[...] + w1 * _load_token_major(buf.at[2 * slot], rows)
         + w2 * _load_token_major(buf.at[2 * slot + 1], rows))
    out_ref[...] = _rmsnorm(y, ng_ref[...])


def _combine_call(pos1, pos2, ys, h, route, norm_g):
    t, d = h.shape
    n_steps, _, rows = pos1.shape
    row_spec = pl.BlockSpec((rows, d), lambda i: (i, 0))
    first = pl.BlockSpec((1, 1, rows), lambda i: (0, 0, 0), memory_space=pltpu.SMEM)
    nxt = pl.BlockSpec((1, 1, rows), lambda i: (jnp.minimum(i + 1, n_steps - 1), 0, 0),
                       memory_space=pltpu.SMEM)
    return pl.pallas_call(
        _combine_kernel,
        grid=(n_steps,),
        in_specs=[first, first, nxt, nxt,
                  pl.BlockSpec(memory_space=pl.ANY),
                  row_spec,
                  pl.BlockSpec((rows, ROUTER_LANES), lambda i: (i, 0)),
                  pl.BlockSpec((1, d), lambda i: (0, 0))],
        out_specs=row_spec,
        out_shape=jax.ShapeDtypeStruct((t, d), F32),
        scratch_shapes=[pltpu.VMEM((4, rows * TOKEN_SUBLANES, LANES), F32),
                        pltpu.SemaphoreType.DMA((2,))],
        compiler_params=pltpu.CompilerParams(
            dimension_semantics=("arbitrary",), vmem_limit_bytes=VMEM_LIMIT_BYTES),
        name="moe_combine",
    )(pos1, pos2, pos1, pos2, ys, h, route, norm_g)


def _sparse_moe(xn, route, route_t, counts, h, w1, w3, w2, norm_g):
    t = h.shape[0]
    tile = EXPERT_ROWS
    n_tiles = (2 * t) // tile + N_EXPERTS
    expert = jnp.arange(N_EXPERTS, dtype=jnp.int32)
    counts = counts[:N_EXPERTS, 0].astype(jnp.int32)
    group_tiles = (counts + tile - 1) // tile
    end_tile = jnp.sum(jnp.where(expert[None, :] <= expert[:, None], group_tiles[None, :], 0), axis=1)
    first_tile = end_tile - group_tiles
    n_real = end_tile[-1:]
    last_tile = jnp.where(group_tiles > 0, end_tile - 1, -1)

    def positions(e_row, r_row):
        e = route_t[:, e_row, :].astype(jnp.int32)
        start = jnp.zeros_like(e)
        for k in range(N_EXPERTS):
            start = jnp.where(e == k, first_tile[k] * tile, start)
        return start + route_t[:, r_row, :].astype(jnp.int32)

    pos1 = positions(ROUTE_E1, ROUTE_R1)
    pos2 = positions(ROUTE_E2, ROUTE_R2)
    per_step = lambda pos, rows: pos.reshape(t // rows, 1, rows)
    xs = _dispatch_call(last_tile, n_real, per_step(pos1, DISPATCH_ROWS),
                        per_step(pos2, DISPATCH_ROWS), xn, n_tiles * tile)
    ys = _expert_call(first_tile, end_tile, n_real, xs, w1, w3, w2)
    return _combine_call(per_step(pos1, COMBINE_ROWS), per_step(pos2, COMBINE_ROWS), ys, h,
                         route, norm_g)


def _layer(h, norm_mix_g, w_in, b_gates, gmlp_ln_g, gmlp_ln_b, w_spatial, b_spatial, bias_tiles,
           w_out, norm_ffn_g, w_group_router, b_group_router, w_expert_router, b_expert_router,
           w1, w3, w2, norm_out_g):
    b, s, d = h.shape
    t = b * s
    nb = s // MOBA_BLOCK
    x2 = h.reshape(t, d)
    row = lambda v: v.reshape(1, -1)

    ya, qT, k, vT, gb, kmean = _proj_call(
        x2, b, row(norm_mix_g), w_in.astype(BF16), row(b_gates), row(gmlp_ln_g), row(gmlp_ln_b),
        w_spatial, b_spatial[:, :, None])

    km = jnp.transpose(kmean.reshape(b, nb, ATT_HEADS, HEAD_DIM), (0, 2, 1, 3))
    km_hi = km.astype(BF16)
    km_lo = (km - km_hi.astype(F32)).astype(BF16)
    yb = _attn_call(qT, k, vT, km_hi, km_lo, bias_tiles)

    w_router = jnp.concatenate(
        [jnp.transpose(w_expert_router, (0, 2, 1)).reshape(N_EXPERTS, d), w_group_router.T,
         jnp.zeros((ROUTER_UNITS - N_EXPERTS - N_GROUPS, d), F32)], axis=0)
    b_router = jnp.concatenate(
        [b_expert_router.reshape(-1), b_group_router,
         jnp.zeros((ROUTER_UNITS - N_EXPERTS - N_GROUPS,), F32)]).reshape(ROUTER_UNITS, 1)
    wr_hi = w_router.astype(BF16)
    wr_lo = (w_router - wr_hi.astype(F32)).astype(BF16)
    h2, xn, route, route_t, counts = _merge_call(
        x2, ya, gb, yb, w_out.astype(BF16), row(norm_ffn_g), wr_hi, wr_lo, b_router)

    out = _sparse_moe(xn, route, route_t, counts, h2, w1, w3, w2, row(norm_out_g))
    return out.reshape(b, s, d)


def kernel(x, norm_mix_g, w_in, b_gates, gmlp_ln_g, gmlp_ln_b, w_spatial, b_spatial, rel_bias, w_out, norm_ffn_g, w_group_router, b_group_router, w_expert_router, b_expert_router, w1, w3, w2, norm_final_g):
    depth = w_in.shape[0]
    assert depth == 1, "the final rmsnorm is fused into the last layer's combine kernel"
    bias_tiles = _bias_call(rel_bias)
    return _layer(x, norm_mix_g[0], w_in[0], b_gates[0], gmlp_ln_g[0], gmlp_ln_b[0], w_spatial[0],
                  b_spatial[0], bias_tiles, w_out[0], norm_ffn_g[0], w_group_router[0],
                  b_group_router[0], w_expert_router[0], b_expert_router[0], w1[0], w3[0], w2[0],
                  norm_final_g)
```

```python
import functools
import math

import numpy as np
import jax
import jax.numpy as jnp
from jax import lax
from jax.experimental import pallas as pl
from jax.experimental.pallas import tpu as pltpu

F32 = jnp.float32
BF16 = jnp.bfloat16

D_MODEL = 1024
NORM_EPS = 1e-6
GMLP_GROUPS = 8
GMLP_CHUNK = 128
ATT_HEADS = 8
HEAD_DIM = 128
MOBA_BLOCK = 256
MOBA_TOPK = 3
REL_BUCKETS = 32
REL_MAX_DIST = 128
N_GROUPS = 4
EXPERTS_PER_GROUP = 8
N_EXPERTS = N_GROUPS * EXPERTS_PER_GROUP
N_SEGMENTS = 7

LANES = 128
TOKEN_SUBLANES = 8
ROUTE_ROWS = 8
VMEM_LIMIT_BYTES = 56 * 1024 * 1024

SQRT_HALF = math.sqrt(0.5)
LOG2E = math.log2(math.e)
SCORE_SCALE2 = (HEAD_DIM ** -0.5) * LOG2E
MASK_NEG = -(2.0 ** 100)
BF16_SUBLANES = 16
V_ROWS = HEAD_DIM + BF16_SUBLANES
ROUTER_LANES = LANES
ROUTER_UNITS = -(-(N_EXPERTS + N_GROUPS) // BF16_SUBLANES) * BF16_SUBLANES
GROUP_UNIT0 = N_EXPERTS

PROJ_ROWS = 512
W_STAGE_ROWS = 64
MERGE_ROWS = 1024
EXPERT_ROWS = 256
DISPATCH_ROWS = 2048
COMBINE_ROWS = 256
DMA_UNROLL = 8


def _rmsnorm(x, g):
    return x * lax.rsqrt(jnp.mean(x * x, axis=-1, keepdims=True) + NORM_EPS) * g


def _gelu(a):
    return 0.5 * a * (1.0 + lax.erf(a * SQRT_HALF))


def _sigmoid(a):
    return 1.0 / (1.0 + jnp.exp(-a))


def _dot(a, b):
    return jnp.dot(a, b, preferred_element_type=F32)


def _stage_weight_bf16(w_hbm, w_scr, stage, sems):
    chunk = stage.shape[1]
    n_chunks = w_scr.shape[0] // chunk

    def copy(c, slot):
        return pltpu.make_async_copy(
            w_hbm.at[pl.ds(c * chunk, chunk), :], stage.at[slot], sems.at[slot])

    copy(0, 0).start()

    def body(c, carry):
        slot = c % 2

        @pl.when(c + 1 < n_chunks)
        def _():
            copy(c + 1, 1 - slot).start()

        copy(c, slot).wait()
        w_scr[pl.ds(pl.multiple_of(c * chunk, chunk), chunk), :] = stage[slot].astype(BF16)
        return carry

    lax.fori_loop(0, n_chunks, body, 0)


def _proj_kernel(x_ref, ng_ref, w_hbm, bg_ref, lng_ref, lnb_ref, ws_ref, bs_ref,
                 ya_ref, qT_ref, k_ref, vT_ref, gb_ref, kmean_ref,
                 w_ref, w_stage, w_sems, xn_scr, vln_scr, mix_scr):
    rows = x_ref.shape[0]
    d = D_MODEL

    @pl.when(pl.program_id(0) == 0)
    def _():
        _stage_weight_bf16(w_hbm, w_ref, w_stage, w_sems)

    xn_scr[...] = _rmsnorm(x_ref[...], ng_ref[...]).astype(BF16)

    def seg(i):
        return _dot(xn_scr[...], w_ref[:, i * d:(i + 1) * d])

    hd = HEAD_DIM

    v = _gelu(seg(1))
    mu = jnp.mean(v, axis=-1, keepdims=True)
    vc = v - mu
    var = jnp.mean(vc * vc, axis=-1, keepdims=True)
    vln_scr[...] = (vc * lax.rsqrt(var + NORM_EPS) * lng_ref[...] + lnb_ref[...]).astype(BF16)

    mix_scr[...] = _gelu(seg(0)) * _sigmoid(seg(5) + bg_ref[:, :d])

    q = seg(2) * SCORE_SCALE2
    for h in range(ATT_HEADS):
        qT_ref[h] = q[:, h * hd:(h + 1) * hd].T.astype(BF16)

    t_idx = lax.broadcasted_iota(jnp.int32, (GMLP_CHUNK, GMLP_CHUNK), 0)
    s_idx = lax.broadcasted_iota(jnp.int32, (GMLP_CHUNK, GMLP_CHUNK), 1)
    causal = t_idx >= s_idx
    gd = d // GMLP_GROUPS
    n_chunks = rows // GMLP_CHUNK
    for g in range(GMLP_GROUPS):
        ws = jnp.where(causal, ws_ref[g], 0.0).astype(BF16)
        bias = bs_ref[g]
        vg = jnp.concatenate(
            [vln_scr[c * GMLP_CHUNK:(c + 1) * GMLP_CHUNK, g * gd:(g + 1) * gd]
             for c in range(n_chunks)], axis=1)
        mixed = _dot(ws, vg)
        for c in range(n_chunks):
            blk_rows = slice(c * GMLP_CHUNK, (c + 1) * GMLP_CHUNK)
            blk_cols = slice(g * gd, (g + 1) * gd)
            ya_ref[blk_rows, blk_cols] = (
                mix_scr[blk_rows, blk_cols] * (mixed[:, c * gd:(c + 1) * gd] + bias)).astype(BF16)

    v = seg(4)
    for blk in range(rows // MOBA_BLOCK):
        r0 = blk * MOBA_BLOCK
        for h in range(ATT_HEADS):
            vT_ref[h, blk, :hd, :] = v[r0:r0 + MOBA_BLOCK, h * hd:(h + 1) * hd].T.astype(BF16)
            vT_ref[h, blk, hd:, :] = jnp.ones((V_ROWS - hd, MOBA_BLOCK), BF16)

    gb_ref[...] = _sigmoid(seg(6) + bg_ref[:, d:]).astype(BF16)

    k = seg(3)
    k_ref[...] = k.astype(BF16)
    for blk in range(rows // MOBA_BLOCK):
        r0 = blk * MOBA_BLOCK
        kmean_ref[0, blk:blk + 1, :] = jnp.mean(k[r0:r0 + MOBA_BLOCK, :], axis=0, keepdims=True)


def _proj_call(x2, batch, norm_g, w_in, b_gates, ln_g, ln_b, w_spatial, b_spatial):
    t, d = x2.shape
    rows = PROJ_ROWS
    seq = t // batch
    assert seq % rows == 0 and rows % MOBA_BLOCK == 0 and rows % GMLP_CHUNK == 0
    n_tiles = t // rows
    tiles_per_seq = seq // rows
    blocks_per_tile = rows // MOBA_BLOCK
    nb = seq // MOBA_BLOCK
    row_spec = pl.BlockSpec((rows, d), lambda i: (i, 0))
    const2 = lambda i: (0, 0)
    const3 = lambda i: (0, 0, 0)
    act = jax.ShapeDtypeStruct((t, d), BF16)
    qT_spec = pl.BlockSpec((None, ATT_HEADS, HEAD_DIM, rows),
                           lambda i: (i // tiles_per_seq, 0, 0, i % tiles_per_seq))
    vT_spec = pl.BlockSpec((None, ATT_HEADS, blocks_per_tile, V_ROWS, MOBA_BLOCK),
                           lambda i: (i // tiles_per_seq, 0, i % tiles_per_seq, 0, 0))
    return pl.pallas_call(
        _proj_kernel,
        grid=(n_tiles,),
        in_specs=[
            row_spec,
            pl.BlockSpec((1, d), const2),
            pl.BlockSpec(memory_space=pl.ANY),
            pl.BlockSpec((1, 2 * d), const2),
            pl.BlockSpec((1, d), const2),
            pl.BlockSpec((1, d), const2),
            pl.BlockSpec((GMLP_GROUPS, GMLP_CHUNK, GMLP_CHUNK), const3),
            pl.BlockSpec((GMLP_GROUPS, GMLP_CHUNK, 1), const3),
        ],
        out_specs=[row_spec, qT_spec, row_spec, vT_spec, row_spec,
                   pl.BlockSpec((1, blocks_per_tile, d), lambda i: (i, 0, 0))],
        out_shape=[act,
                   jax.ShapeDtypeStruct((batch, ATT_HEADS, HEAD_DIM, seq), BF16),
                   act,
                   jax.ShapeDtypeStruct((batch, ATT_HEADS, nb, V_ROWS, MOBA_BLOCK), BF16),
                   act,
                   jax.ShapeDtypeStruct((n_tiles, blocks_per_tile, d), F32)],
        scratch_shapes=[pltpu.VMEM((d, N_SEGMENTS * d), BF16),
                        pltpu.VMEM((2, W_STAGE_ROWS, N_SEGMENTS * d), F32),
                        pltpu.SemaphoreType.DMA((2,)),
                        pltpu.VMEM((rows, d), BF16), pltpu.VMEM((rows, d), BF16),
                        pltpu.VMEM((rows, d), F32)],
        compiler_params=pltpu.CompilerParams(
            dimension_semantics=("arbitrary",), vmem_limit_bytes=VMEM_LIMIT_BYTES),
        name="proj_gmlp",
    )(x2, norm_g, w_in, b_gates, ln_g, ln_b, w_spatial, b_spatial)


def _t5_bucket_np(n):
    n = np.maximum(n, 0)
    max_exact = REL_BUCKETS // 2
    nf = np.maximum(n, max_exact).astype(np.float32)
    large = max_exact + (np.log(nf / max_exact) / math.log(REL_MAX_DIST / max_exact)
                         * (REL_BUCKETS - max_exact)).astype(np.int32)
    large = np.minimum(large, REL_BUCKETS - 1)
    return np.where(n < max_exact, n, large).astype(np.int32)


def _bucket_tiles():
    kpos = np.arange(MOBA_BLOCK, dtype=np.int32)[:, None]
    qpos = np.arange(MOBA_BLOCK, dtype=np.int32)[None, :]
    rel = qpos - kpos
    return np.stack([_t5_bucket_np(rel), _t5_bucket_np(rel + MOBA_BLOCK)])


def _bias_kernel(relb_ref, bucket_ref, out_ref):
    h = pl.program_id(0)
    far = relb_ref[REL_BUCKETS - 1, h]
    k_idx = lax.broadcasted_iota(jnp.int32, (MOBA_BLOCK, MOBA_BLOCK), 0)
    q_idx = lax.broadcasted_iota(jnp.int32, (MOBA_BLOCK, MOBA_BLOCK), 1)
    for tile in range(2):
        bucket = bucket_ref[tile]
        bias = jnp.zeros((MOBA_BLOCK, MOBA_BLOCK), F32)
        for b in range(REL_BUCKETS):
            bias = jnp.where(bucket == b, relb_ref[b, h], bias)
        bias2 = (bias - far) * LOG2E
        if tile == BIAS_OWN:
            bias2 = jnp.where(q_idx >= k_idx, bias2, MASK_NEG)
        out_ref[0, tile] = bias2


def _bias_call(rel_bias):
    buckets = _bucket_tiles()
    return pl.pallas_call(
        _bias_kernel,
        grid=(ATT_HEADS,),
        in_specs=[pl.BlockSpec(memory_space=pltpu.SMEM),
                  pl.BlockSpec((2, MOBA_BLOCK, MOBA_BLOCK), lambda h: (0, 0, 0))],
        out_specs=pl.BlockSpec((1, N_BIAS_TILES, MOBA_BLOCK, MOBA_BLOCK), lambda h: (h, 0, 0, 0)),
        out_shape=jax.ShapeDtypeStruct((ATT_HEADS, N_BIAS_TILES, MOBA_BLOCK, MOBA_BLOCK), F32),
        compiler_params=pltpu.CompilerParams(dimension_semantics=("arbitrary",)),
        name="t5_bias_tiles",
    )(rel_bias, jnp.asarray(buckets))


BIAS_OWN, BIAS_PREV = range(2)
N_BIAS_TILES = 2


def _attn_kernel(qT_ref, k_ref, vT_ref, kmh_ref, kml_ref, oh_ref, bias_ref, o_ref,
                 qp_scr, m_scr, acc_scr):
    i = pl.program_id(1)
    nh, nb = kmh_ref.shape[0], kmh_ref.shape[1]
    blk = MOBA_BLOCK
    hd = HEAD_DIM

    n_idx = lax.broadcasted_iota(jnp.int32, (nb, blk), 0)
    past = n_idx < i
    for h in range(nh):
        qT = qT_ref[h]
        gate = _dot(kmh_ref[h], qT) + _dot(kml_ref[h], qT)
        gate = jnp.where(past, gate, -jnp.inf)
        rank = jnp.zeros((nb, blk), F32)
        for m in range(nb):
            row = gate[m:m + 1, :]
            beats = (row > gate) | ((row == gate) & (m < n_idx))
            rank = rank + jnp.where(beats, 1.0, 0.0)
        keep = (past & (rank < MOBA_TOPK)) | (n_idx == i)
        sel = jnp.where(keep, 0.0, MASK_NEG)
        sel = jnp.concatenate([sel, jnp.zeros((hd - nb, blk), F32)], axis=0)
        qp_scr[h] = jnp.concatenate([qT, sel.astype(BF16)], axis=0)

    def scores(h, j):
        r0 = pl.multiple_of(j * blk, blk)
        kp = jnp.concatenate([k_ref[pl.ds(r0, blk), h * hd:(h + 1) * hd], oh_ref[j]], axis=1)
        return _dot(kp, qp_scr[h])

    def fold(blocks, bias_tiles, first):
        ss = []
        for h in range(nh):
            parts = []
            for j, tile in zip(blocks, bias_tiles):
                s = scores(h, j)
                parts.append((s if tile is None else s + bias_ref[h, tile]).astype(BF16))
            ss.append(parts)
        ps, alphas = [], []
        for h in range(nh):
            m_new = functools.reduce(
                jnp.maximum, [jnp.max(s, axis=0, keepdims=True) for s in ss[h]]).astype(F32)
            if not first:
                m_old = m_scr[h]
                m_new = jnp.maximum(m_old, m_new)
                alphas.append(jnp.exp2(m_old - m_new))
            m_scr[h] = m_new
            ps.append([jnp.exp2(s - m_new.astype(BF16)) for s in ss[h]])
        for h in range(nh):
            pv = functools.reduce(
                lambda a, b: a + b, [_dot(vT_ref[h, j], x) for j, x in zip(blocks, ps[h])])
            acc_scr[h] = pv if first else alphas[h] * acc_scr[h] + pv

    @pl.when(i == 0)
    def _():
        fold([i], [BIAS_OWN], first=True)

    @pl.when(i >= 1)
    def _():
        fold([i, i - 1], [BIAS_OWN, BIAS_PREV], first=True)

    n_far = i - 1

    def far_pair(p, carry):
        fold([2 * p, 2 * p + 1], [None, None], first=False)
        return carry

    lax.fori_loop(0, n_far // 2, far_pair, 0)

    @pl.when((n_far >= 1) & (n_far % 2 == 1))
    def _():
        fold([n_far - 1], [None], first=False)

    for h in range(nh):
        y = acc_scr[h, :hd, :] * (1.0 / acc_scr[h, hd:hd + 1, :])
        o_ref[:, h * hd:(h + 1) * hd] = y.T.astype(o_ref.dtype)


def _attn_call(qT, k2, vT, km_hi, km_lo, bias_tiles):
    b, h, nb, v_rows, blk = vT.shape
    hd = HEAD_DIM
    s = nb * blk
    onehot = np.zeros((nb, blk, LANES), np.float32)
    for j in range(nb):
        onehot[j, :, j] = 1.0
    once = pl.Buffered(1)
    return pl.pallas_call(
        _attn_kernel,
        grid=(b, nb),
        in_specs=[
            pl.BlockSpec((None, h, hd, blk), lambda bi, i: (bi, 0, 0, i)),
            pl.BlockSpec((s, h * hd), lambda bi, i: (bi, 0)),
            pl.BlockSpec((None, h, nb, v_rows, blk), lambda bi, i: (bi, 0, 0, 0, 0)),
            pl.BlockSpec((None, h, nb, hd), lambda bi, i: (bi, 0, 0, 0)),
            pl.BlockSpec((None, h, nb, hd), lambda bi, i: (bi, 0, 0, 0)),
            pl.BlockSpec((nb, blk, LANES), lambda bi, i: (0, 0, 0), pipeline_mode=once),
            pl.BlockSpec((h, N_BIAS_TILES, blk, blk), lambda bi, i: (0, 0, 0, 0),
                         pipeline_mode=once),
        ],
        out_specs=pl.BlockSpec((blk, h * hd), lambda bi, i: (bi * nb + i, 0)),
        out_shape=jax.ShapeDtypeStruct((b * s, h * hd), BF16),
        scratch_shapes=[pltpu.VMEM((h, 2 * hd, blk), BF16), pltpu.VMEM((h, 1, blk), F32),
                        pltpu.VMEM((h, v_rows, blk), F32)],
        compiler_params=pltpu.CompilerParams(
            dimension_semantics=("arbitrary", "arbitrary"),
            vmem_limit_bytes=VMEM_LIMIT_BYTES),
        name="moba_attention",
    )(qT, k2, vT, km_hi, km_lo, jnp.asarray(onehot, BF16), bias_tiles)


ROUTE_E1, ROUTE_E2, ROUTE_W1, ROUTE_W2, ROUTE_R1, ROUTE_R2 = range(6)


def _dot_nt(a, b):
    return lax.dot_general(a, b, (((1,), (1,)), ((), ())), preferred_element_type=F32)


def _store_token_major(ref, x):
    rows = x.shape[0]
    for s in range(TOKEN_SUBLANES):
        ref[pl.ds(s, rows, stride=TOKEN_SUBLANES), :] = x[:, s * LANES:(s + 1) * LANES]


def _load_token_major(ref, rows):
    return jnp.concatenate(
        [ref[pl.ds(s, rows, stride=TOKEN_SUBLANES), :] for s in range(TOKEN_SUBLANES)], axis=1)


def _merge_kernel(x_ref, ya_ref, gb_ref, yb_ref, wo_ref, ng_ref, wrh_ref, wrl_ref, br_ref,
                  h_ref, xn_ref, route_ref, route_t_ref, counts_ref, run_scr, wo_scr):
    @pl.when(pl.program_id(0) == 0)
    def _():
        run_scr[...] = jnp.zeros_like(run_scr)
        wo_scr[...] = wo_ref[...].astype(BF16)

    f = lambda r: r[...].astype(F32)
    mix = (f(ya_ref) + f(gb_ref) * f(yb_ref)).astype(BF16)
    h = x_ref[...] + _dot(mix, wo_scr[...])
    h_ref[...] = h
    xn = _rmsnorm(h, ng_ref[...])
    _store_token_major(xn_ref, xn)
    rows = xn.shape[0]

    x_hi = xn.astype(BF16)
    x_lo = (xn - x_hi.astype(F32)).astype(BF16)
    logits = (_dot_nt(wrh_ref[...], x_hi) + _dot_nt(wrh_ref[...], x_lo)
              + _dot_nt(wrl_ref[...], x_hi) + br_ref[...])
    unit = lax.broadcasted_iota(jnp.int32, logits.shape, 0).astype(F32)
    big = float(ROUTER_UNITS)
    neg_inf = -jnp.inf

    gl = jnp.where((unit >= GROUP_UNIT0) & (unit < GROUP_UNIT0 + N_GROUPS), logits, neg_inf)
    gmax = jnp.max(gl, axis=0, keepdims=True)
    g_w = 1.0 / jnp.sum(jnp.exp(gl - gmax), axis=0, keepdims=True)
    g_idx = jnp.min(jnp.where(gl == gmax, unit, big), axis=0, keepdims=True) - GROUP_UNIT0

    e0 = g_idx * EXPERTS_PER_GROUP
    el = jnp.where((unit >= e0) & (unit < e0 + EXPERTS_PER_GROUP), logits, neg_inf)
    m1 = jnp.max(el, axis=0, keepdims=True)
    i1 = jnp.min(jnp.where(el == m1, unit, big), axis=0, keepdims=True)
    el2 = jnp.where(unit == i1, neg_inf, el)
    m2 = jnp.max(el2, axis=0, keepdims=True)
    i2 = jnp.min(jnp.where(el2 == m2, unit, big), axis=0, keepdims=True)
    e2 = jnp.exp(m2 - m1)
    den = 1.0 + e2
    w1 = (1.0 / den) * g_w
    w2 = (e2 / den) * g_w

    hit1 = unit == i1
    hit2 = unit == i2
    onehot = jnp.where(hit1, 1.0, jnp.where(hit2, 1.0, 0.0))
    c_idx = lax.broadcasted_iota(jnp.int32, (rows, rows), 0)
    r_idx = lax.broadcasted_iota(jnp.int32, (rows, rows), 1)
    earlier = jnp.where(c_idx < r_idx, 1.0, 0.0).astype(BF16)
    prefix = run_scr[...] + _dot(onehot.astype(BF16), earlier)
    rank1 = jnp.sum(jnp.where(hit1, prefix, 0.0), axis=0, keepdims=True)
    rank2 = jnp.sum(jnp.where(hit2, prefix, 0.0), axis=0, keepdims=True)
    run_scr[...] = run_scr[...] + jnp.sum(onehot, axis=1, keepdims=True)
    counts_ref[...] = run_scr[...]

    route_t = jnp.concatenate(
        [i1, i2, w1, w2, rank1, rank2, jnp.zeros((ROUTER_LANES - 6, rows), F32)], axis=0)
    route_t_ref[0] = route_t[:ROUTE_ROWS]
    route_ref[...] = route_t.T


def _merge_call(x2, ya, gb, yb, w_out, norm_g, wr_hi, wr_lo, b_router):
    t, d = x2.shape
    rows = MERGE_ROWS
    assert t % rows == 0 and d == TOKEN_SUBLANES * LANES
    n_tiles = t // rows
    row_spec = pl.BlockSpec((rows, d), lambda i: (i, 0))
    const2 = lambda i: (0, 0)
    return pl.pallas_call(
        _merge_kernel,
        grid=(n_tiles,),
        in_specs=[row_spec, row_spec, row_spec, row_spec,
                  pl.BlockSpec((d, d), const2, pipeline_mode=pl.Buffered(1)),
                  pl.BlockSpec((1, d), const2),
                  pl.BlockSpec((ROUTER_UNITS, d), const2),
                  pl.BlockSpec((ROUTER_UNITS, d), const2),
                  pl.BlockSpec((ROUTER_UNITS, 1), const2)],
        out_specs=[row_spec,
                   pl.BlockSpec((rows * TOKEN_SUBLANES, LANES), lambda i: (i, 0)),
                   pl.BlockSpec((rows, ROUTER_LANES), lambda i: (i, 0)),
                   pl.BlockSpec((1, ROUTE_ROWS, rows), lambda i: (i, 0, 0)),
                   pl.BlockSpec((ROUTER_UNITS, 1), const2)],
        out_shape=[jax.ShapeDtypeStruct((t, d), F32),
                   jax.ShapeDtypeStruct((t * TOKEN_SUBLANES, LANES), F32),
                   jax.ShapeDtypeStruct((t, ROUTER_LANES), F32),
                   jax.ShapeDtypeStruct((n_tiles, ROUTE_ROWS, rows), F32),
                   jax.ShapeDtypeStruct((ROUTER_UNITS, 1), F32)],
        scratch_shapes=[pltpu.VMEM((ROUTER_UNITS, 1), F32), pltpu.VMEM((d, d), BF16)],
        compiler_params=pltpu.CompilerParams(
            dimension_semantics=("arbitrary",), vmem_limit_bytes=VMEM_LIMIT_BYTES),
        name="merge_outproj_router",
    )(x2, ya, gb, yb, w_out, norm_g, wr_hi, wr_lo, b_router)


def _token_rows(ref, token):
    return ref.at[pl.ds(pl.multiple_of(token * TOKEN_SUBLANES, TOKEN_SUBLANES), TOKEN_SUBLANES)]


def _dispatch_kernel(last_ref, nreal_ref, pos1_ref, pos2_ref, xn_ref, xs_hbm, zero_scr, sem):
    rows = pos1_ref.shape[2]
    tile = EXPERT_ROWS
    n_tiles = xs_hbm.shape[0] // (tile * TOKEN_SUBLANES)

    def zero_tile(j):
        start = pl.multiple_of(j * (tile * TOKEN_SUBLANES), tile * TOKEN_SUBLANES)
        return pltpu.make_async_copy(
            zero_scr, xs_hbm.at[pl.ds(start, tile * TOKEN_SUBLANES)], sem)

    @pl.when(pl.program_id(0) == 0)
    def _():
        zero_scr[...] = jnp.zeros_like(zero_scr)
        for e in range(N_EXPERTS):
            @pl.when(last_ref[e] >= 0)
            def _():
                zero_tile(last_ref[e]).start()

        def tail_start(j, carry):
            zero_tile(j).start()
            return carry

        lax.fori_loop(nreal_ref[0], n_tiles, tail_start, 0)

        for e in range(N_EXPERTS):
            @pl.when(last_ref[e] >= 0)
            def _():
                zero_tile(0).wait()

        def tail_wait(j, carry):
            zero_tile(0).wait()
            return carry

        lax.fori_loop(nreal_ref[0], n_tiles, tail_wait, 0)

    def issue(g, carry):
        for u in range(DMA_UNROLL):
            r = g * DMA_UNROLL + u
            src = _token_rows(xn_ref, r)
            pltpu.make_async_copy(src, _token_rows(xs_hbm, pos1_ref[0, 0, r]), sem).start(0)
            pltpu.make_async_copy(src, _token_rows(xs_hbm, pos2_ref[0, 0, r]), sem).start(1)
        return carry

    lax.fori_loop(0, rows // DMA_UNROLL, issue, 0)

    for _ in range(2):
        pltpu.make_async_copy(xn_ref, xs_hbm.at[pl.ds(0, rows * TOKEN_SUBLANES)], sem).wait()


def _dispatch_call(last_tile, n_real, pos1, pos2, xn, n_sorted_rows):
    n_steps, _, rows = pos1.shape
    smem_row = pl.BlockSpec((1, 1, rows), lambda i, lt, nr: (i, 0, 0), memory_space=pltpu.SMEM)
    return pl.pallas_call(
        _dispatch_kernel,
        grid_spec=pltpu.PrefetchScalarGridSpec(
            num_scalar_prefetch=2,
            grid=(n_steps,),
            in_specs=[smem_row, smem_row,
                      pl.BlockSpec((rows * TOKEN_SUBLANES, LANES), lambda i, lt, nr: (i, 0))],
            out_specs=pl.BlockSpec(memory_space=pl.ANY),
            scratch_shapes=[pltpu.VMEM((EXPERT_ROWS * TOKEN_SUBLANES, LANES), F32),
                            pltpu.SemaphoreType.DMA(())],
        ),
        out_shape=jax.ShapeDtypeStruct((n_sorted_rows * TOKEN_SUBLANES, LANES), F32),
        compiler_params=pltpu.CompilerParams(dimension_semantics=("arbitrary",)),
        name="moe_dispatch",
    )(last_tile, n_real, pos1, pos2, xn)


EXPERT_IN_SLOTS = 6
EXPERT_OUT_SLOTS = 4


def _expert_kernel(first_ref, end_ref, nreal_ref, xs_hbm, w1_ref, w3_ref, w2_ref, ys_hbm,
                   w1_scr, w3_scr, w2_scr, xbuf, ybuf, in_sems, out_sems):
    e = pl.program_id(0)
    tile_rows = EXPERT_ROWS * TOKEN_SUBLANES
    n_real = nreal_ref[0]
    n_tiles = xs_hbm.shape[0] // tile_rows

    def tile_of(ref, t):
        return ref.at[pl.ds(pl.multiple_of(t * tile_rows, tile_rows), tile_rows)]

    def in_copy(t):
        slot = t % EXPERT_IN_SLOTS
        return pltpu.make_async_copy(tile_of(xs_hbm, t), xbuf.at[slot], in_sems.at[slot])

    def out_copy(t):
        slot = t % EXPERT_OUT_SLOTS
        return pltpu.make_async_copy(ybuf.at[slot], tile_of(ys_hbm, t), out_sems.at[slot])

    @pl.when(e == 0)
    def _():
        for t in range(EXPERT_IN_SLOTS - 1):
            @pl.when(t < n_real)
            def _():
                in_copy(t).start()

    w1_scr[...] = w1_ref[...].astype(BF16)
    w3_scr[...] = w3_ref[...].astype(BF16)
    w2_scr[...] = w2_ref[...].astype(BF16)

    def tile(t, carry):
        ahead = t + EXPERT_IN_SLOTS - 1

        @pl.when(ahead < n_real)
        def _():
            in_copy(ahead).start()

        in_copy(t).wait()

        @pl.when(t >= EXPERT_OUT_SLOTS)
        def _():
            out_copy(t - EXPERT_OUT_SLOTS).wait()

        x = _load_token_major(xbuf.at[t % EXPERT_IN_SLOTS], EXPERT_ROWS).astype(BF16)
        a = _dot(x, w1_scr[...])
        b = _dot(x, w3_scr[...])
        hid = (a * _sigmoid(a)) * b
        _store_token_major(ybuf.at[t % EXPERT_OUT_SLOTS], _dot(hid.astype(BF16), w2_scr[...]))
        out_copy(t).start()
        return carry

    lax.fori_loop(first_ref[e], end_ref[e], tile, 0)

    @pl.when(e == pl.num_programs(0) - 1)
    def _():
        for back in range(EXPERT_OUT_SLOTS, 0, -1):
            @pl.when(n_real - back >= 0)
            def _():
                out_copy(n_real - back).wait()
        ybuf[0] = jnp.zeros(ybuf.shape[1:], F32)

        def tail_start(t, carry):
            pltpu.make_async_copy(ybuf.at[0], tile_of(ys_hbm, t), out_sems.at[0]).start()
            return carry

        def tail_wait(t, carry):
            pltpu.make_async_copy(ybuf.at[0], tile_of(ys_hbm, t), out_sems.at[0]).wait()
            return carry

        lax.fori_loop(n_real, n_tiles, tail_start, 0)
        lax.fori_loop(n_real, n_tiles, tail_wait, 0)


def _expert_call(first_tile, end_tile, n_real, xs, w1, w3, w2):
    n_experts, d, d_expert = w1.shape
    tile_rows = EXPERT_ROWS * TOKEN_SUBLANES
    per_expert = lambda e, f, n, nr: (e, 0, 0)
    return pl.pallas_call(
        _expert_kernel,
        grid_spec=pltpu.PrefetchScalarGridSpec(
            num_scalar_prefetch=3,
            grid=(n_experts,),
            in_specs=[pl.BlockSpec(memory_space=pl.ANY),
                      pl.BlockSpec((None, d, d_expert), per_expert),
                      pl.BlockSpec((None, d, d_expert), per_expert),
                      pl.BlockSpec((None, d_expert, d), per_expert)],
            out_specs=pl.BlockSpec(memory_space=pl.ANY),
            scratch_shapes=[pltpu.VMEM((d, d_expert), BF16), pltpu.VMEM((d, d_expert), BF16),
                            pltpu.VMEM((d_expert, d), BF16),
                            pltpu.VMEM((EXPERT_IN_SLOTS, tile_rows, LANES), F32),
                            pltpu.VMEM((EXPERT_OUT_SLOTS, tile_rows, LANES), F32),
                            pltpu.SemaphoreType.DMA((EXPERT_IN_SLOTS,)),
                            pltpu.SemaphoreType.DMA((EXPERT_OUT_SLOTS,))],
        ),
        out_shape=jax.ShapeDtypeStruct(xs.shape, F32),
        compiler_params=pltpu.CompilerParams(
            dimension_semantics=("arbitrary",), vmem_limit_bytes=VMEM_LIMIT_BYTES),
        name="moe_experts",
    )(first_tile, end_tile, n_real, xs, w1, w3, w2)


def _combine_kernel(p1_first, p2_first, p1_next, p2_next, ys_hbm, h_ref, route_ref, ng_ref,
                    out_ref, buf, sems):
    i = pl.program_id(0)
    n = pl.num_programs(0)
    rows = h_ref.shape[0]

    def issue(p1_ref, p2_ref, slot):
        def body(g, carry):
            for u in range(DMA_UNROLL):
                r = g * DMA_UNROLL + u
                for which, p_ref in ((0, p1_ref), (1, p2_ref)):
                    pltpu.make_async_copy(_token_rows(ys_hbm, p_ref[0, 0, r]),
                                          _token_rows(buf.at[2 * slot + which], r),
                                          sems.at[slot]).start(which)
            return carry
        lax.fori_loop(0, rows // DMA_UNROLL, body, 0)

    @pl.when(i == 0)
    def _():
        issue(p1_first, p2_first, 0)

    @pl.when(i + 1 < n)
    def _():
        issue(p1_next, p2_next, (i + 1) % 2)

    slot = i % 2
    for which in range(2):
        pltpu.make_async_copy(ys_hbm.at[pl.ds(0, rows * TOKEN_SUBLANES)],
                              buf.at[2 * slot + which], sems.at[slot]).wait()

    route = route_ref[...]
    w1 = route[:, ROUTE_W1:ROUTE_W1 + 1]
    w2 = route[:, ROUTE_W2:ROUTE_W2 + 1]
    y = (h_ref[...] + w1 * _load_token_major(buf.at[2 * slot], rows)
         + w2 * _load_token_major(buf.at[2 * slot + 1], rows))
    out_ref[...] = _rmsnorm(y, ng_ref[...])


def _combine_call(pos1, pos2, ys, h, route, norm_g):
    t, d = h.shape
    n_steps, _, rows = pos1.shape
    row_spec = pl.BlockSpec((rows, d), lambda i: (i, 0))
    first = pl.BlockSpec((1, 1, rows), lambda i: (0, 0, 0), memory_space=pltpu.SMEM)
    nxt = pl.BlockSpec((1, 1, rows), lambda i: (jnp.minimum(i + 1, n_steps - 1), 0, 0),
                       memory_space=pltpu.SMEM)
    return pl.pallas_call(
        _combine_kernel,
        grid=(n_steps,),
        in_specs=[first, first, nxt, nxt,
                  pl.BlockSpec(memory_space=pl.ANY),
                  row_spec,
                  pl.BlockSpec((rows, ROUTER_LANES), lambda i: (i, 0)),
                  pl.BlockSpec((1, d), lambda i: (0, 0))],
        out_specs=row_spec,
        out_shape=jax.ShapeDtypeStruct((t, d), F32),
        scratch_shapes=[pltpu.VMEM((4, rows * TOKEN_SUBLANES, LANES), F32),
                        pltpu.SemaphoreType.DMA((2,))],
        compiler_params=pltpu.CompilerParams(
            dimension_semantics=("arbitrary",), vmem_limit_bytes=VMEM_LIMIT_BYTES),
        name="moe_combine",
    )(pos1, pos2, pos1, pos2, ys, h, route, norm_g)


def _sparse_moe(xn, route, route_t, counts, h, w1, w3, w2, norm_g):
    t = h.shape[0]
    tile = EXPERT_ROWS
    n_tiles = (2 * t) // tile + N_EXPERTS
    expert = jnp.arange(N_EXPERTS, dtype=jnp.int32)
    counts = counts[:N_EXPERTS, 0].astype(jnp.int32)
    group_tiles = (counts + tile - 1) // tile
    end_tile = jnp.sum(jnp.where(expert[None, :] <= expert[:, None], group_tiles[None, :], 0), axis=1)
    first_tile = end_tile - group_tiles
    n_real = end_tile[-1:]
    last_tile = jnp.where(group_tiles > 0, end_tile - 1, -1)

    def positions(e_row, r_row):
        e = route_t[:, e_row, :].astype(jnp.int32)
        start = jnp.zeros_like(e)
        for k in range(N_EXPERTS):
            start = jnp.where(e == k, first_tile[k] * tile, start)
        return start + route_t[:, r_row, :].astype(jnp.int32)

    pos1 = positions(ROUTE_E1, ROUTE_R1)
    pos2 = positions(ROUTE_E2, ROUTE_R2)
    per_step = lambda pos, rows: pos.reshape(t // rows, 1, rows)
    xs = _dispatch_call(last_tile, n_real, per_step(pos1, DISPATCH_ROWS),
                        per_step(pos2, DISPATCH_ROWS), xn, n_tiles * tile)
    ys = _expert_call(first_tile, end_tile, n_real, xs, w1, w3, w2)
    return _combine_call(per_step(pos1, COMBINE_ROWS), per_step(pos2, COMBINE_ROWS), ys, h,
                         route, norm_g)


def _layer(h, norm_mix_g, w_in, b_gates, gmlp_ln_g, gmlp_ln_b, w_spatial, b_spatial, bias_tiles,
           w_out, norm_ffn_g, w_group_router, b_group_router, w_expert_router, b_expert_router,
           w1, w3, w2, norm_out_g):
    b, s, d = h.shape
    t = b * s
    nb = s // MOBA_BLOCK
    x2 = h.reshape(t, d)
    row = lambda v: v.reshape(1, -1)

    ya, qT, k, vT, gb, kmean = _proj_call(
        x2, b, row(norm_mix_g), w_in, row(b_gates), row(gmlp_ln_g), row(gmlp_ln_b),
        w_spatial, b_spatial[:, :, None])

    km = jnp.transpose(kmean.reshape(b, nb, ATT_HEADS, HEAD_DIM), (0, 2, 1, 3))
    km_hi = km.astype(BF16)
    km_lo = (km - km_hi.astype(F32)).astype(BF16)
    yb = _attn_call(qT, k, vT, km_hi, km_lo, bias_tiles)

    w_router = jnp.concatenate(
        [jnp.transpose(w_expert_router, (0, 2, 1)).reshape(N_EXPERTS, d), w_group_router.T,
         jnp.zeros((ROUTER_UNITS - N_EXPERTS - N_GROUPS, d), F32)], axis=0)
    b_router = jnp.concatenate(
        [b_expert_router.reshape(-1), b_group_router,
         jnp.zeros((ROUTER_UNITS - N_EXPERTS - N_GROUPS,), F32)]).reshape(ROUTER_UNITS, 1)
    wr_hi = w_router.astype(BF16)
    wr_lo = (w_router - wr_hi.astype(F32)).astype(BF16)
    h2, xn, route, route_t, counts = _merge_call(
        x2, ya, gb, yb, w_out, row(norm_ffn_g), wr_hi, wr_lo, b_router)

    out = _sparse_moe(xn, route, route_t, counts, h2, w1, w3, w2, row(norm_out_g))
    return out.reshape(b, s, d)


def kernel(x, norm_mix_g, w_in, b_gates, gmlp_ln_g, gmlp_ln_b, w_spatial, b_spatial, rel_bias, w_out, norm_ffn_g, w_group_router, b_group_router, w_expert_router, b_expert_router, w1, w3, w2, norm_final_g):
    depth = w_in.shape[0]
    assert depth == 1, "the final rmsnorm is fused into the last layer's combine kernel"
    bias_tiles = _bias_call(rel_bias)
    return _layer(x, norm_mix_g[0], w_in[0], b_gates[0], gmlp_ln_g[0], gmlp_ln_b[0], w_spatial[0],
                  b_spatial[0], bias_tiles, w_out[0], norm_ffn_g[0], w_group_router[0],
                  b_group_router[0], w_expert_router[0], b_expert_router[0], w1[0], w3[0], w2[0],
                  norm_final_g)
```

```python
import functools
import math

import numpy as np
import jax
import jax.numpy as jnp
from jax import lax
from jax.experimental import pallas as pl
from jax.experimental.pallas import tpu as pltpu

F32 = jnp.float32
BF16 = jnp.bfloat16

D_MODEL = 1024
NORM_EPS = 1e-6
GMLP_GROUPS = 8
GMLP_CHUNK = 128
ATT_HEADS = 8
HEAD_DIM = 128
MOBA_BLOCK = 256
MOBA_TOPK = 3
REL_BUCKETS = 32
REL_MAX_DIST = 128
N_GROUPS = 4
EXPERTS_PER_GROUP = 8
N_EXPERTS = N_GROUPS * EXPERTS_PER_GROUP
N_SEGMENTS = 7

LANES = 128
TOKEN_SUBLANES = 8
ROUTE_ROWS = 8
VMEM_LIMIT_BYTES = 56 * 1024 * 1024

SQRT_HALF = math.sqrt(0.5)
LOG2E = math.log2(math.e)
SCORE_SCALE2 = (HEAD_DIM ** -0.5) * LOG2E
MASK_NEG = -(2.0 ** 100)
BF16_SUBLANES = 16
V_ROWS = HEAD_DIM + BF16_SUBLANES
ROUTER_LANES = LANES
ROUTER_UNITS = -(-(N_EXPERTS + N_GROUPS) // BF16_SUBLANES) * BF16_SUBLANES
GROUP_UNIT0 = N_EXPERTS

PROJ_ROWS = 512
W_STAGE_ROWS = 64
MERGE_ROWS = 1024
EXPERT_ROWS = 256
DISPATCH_ROWS = 2048
COMBINE_ROWS = 256
DMA_UNROLL = 8


def _rmsnorm(x, g):
    return x * lax.rsqrt(jnp.mean(x * x, axis=-1, keepdims=True) + NORM_EPS) * g


def _gelu(a):
    return 0.5 * a * (1.0 + lax.erf(a * SQRT_HALF))


def _sigmoid(a):
    return 1.0 / (1.0 + jnp.exp(-a))


def _dot(a, b):
    return jnp.dot(a, b, preferred_element_type=F32)


def _stage_weight_bf16(w_hbm, w_scr, stage, sems):
    chunk = stage.shape[1]
    n_chunks = w_scr.shape[0] // chunk

    def copy(c, slot):
        return pltpu.make_async_copy(
            w_hbm.at[pl.ds(c * chunk, chunk), :], stage.at[slot], sems.at[slot])

    copy(0, 0).start()

    def body(c, carry):
        slot = c % 2

        @pl.when(c + 1 < n_chunks)
        def _():
            copy(c + 1, 1 - slot).start()

        copy(c, slot).wait()
        w_scr[pl.ds(pl.multiple_of(c * chunk, chunk), chunk), :] = stage[slot].astype(BF16)
        return carry

    lax.fori_loop(0, n_chunks, body, 0)


def _proj_kernel(x_ref, ng_ref, w_hbm, bg_ref, lng_ref, lnb_ref, ws_ref, bs_ref,
                 ya_ref, qT_ref, k_ref, vT_ref, gb_ref, kmean_ref,
                 w_ref, w_stage, w_sems, xn_scr, vln_scr, mix_scr):
    rows = x_ref.shape[0]
    d = D_MODEL

    @pl.when(pl.program_id(0) == 0)
    def _():
        _stage_weight_bf16(w_hbm, w_ref, w_stage, w_sems)

    xn_scr[...] = _rmsnorm(x_ref[...], ng_ref[...]).astype(BF16)

    def seg(i):
        return _dot(xn_scr[...], w_ref[:, i * d:(i + 1) * d])

    hd = HEAD_DIM

    v = _gelu(seg(1))
    mu = jnp.mean(v, axis=-1, keepdims=True)
    vc = v - mu
    var = jnp.mean(vc * vc, axis=-1, keepdims=True)
    vln_scr[...] = (vc * lax.rsqrt(var + NORM_EPS) * lng_ref[...] + lnb_ref[...]).astype(BF16)

    mix_scr[...] = _gelu(seg(0)) * _sigmoid(seg(5) + bg_ref[:, :d])

    q = seg(2) * SCORE_SCALE2
    for h in range(ATT_HEADS):
        qT_ref[h] = q[:, h * hd:(h + 1) * hd].T.astype(BF16)

    t_idx = lax.broadcasted_iota(jnp.int32, (GMLP_CHUNK, GMLP_CHUNK), 0)
    s_idx = lax.broadcasted_iota(jnp.int32, (GMLP_CHUNK, GMLP_CHUNK), 1)
    causal = t_idx >= s_idx
    gd = d // GMLP_GROUPS
    n_chunks = rows // GMLP_CHUNK
    for g in range(GMLP_GROUPS):
        ws = jnp.where(causal, ws_ref[g], 0.0).astype(BF16)
        bias = bs_ref[g]
        vg = jnp.concatenate(
            [vln_scr[c * GMLP_CHUNK:(c + 1) * GMLP_CHUNK, g * gd:(g + 1) * gd]
             for c in range(n_chunks)], axis=1)
        mixed = _dot(ws, vg)
        for c in range(n_chunks):
            blk_rows = slice(c * GMLP_CHUNK, (c + 1) * GMLP_CHUNK)
            blk_cols = slice(g * gd, (g + 1) * gd)
            ya_ref[blk_rows, blk_cols] = (
                mix_scr[blk_rows, blk_cols] * (mixed[:, c * gd:(c + 1) * gd] + bias)).astype(BF16)

    v = seg(4)
    for blk in range(rows // MOBA_BLOCK):
        r0 = blk * MOBA_BLOCK
        for h in range(ATT_HEADS):
            vT_ref[h, blk, :hd, :] = v[r0:r0 + MOBA_BLOCK, h * hd:(h + 1) * hd].T.astype(BF16)
            vT_ref[h, blk, hd:, :] = jnp.ones((V_ROWS - hd, MOBA_BLOCK), BF16)

    gb_ref[...] = _sigmoid(seg(6) + bg_ref[:, d:]).astype(BF16)

    k = seg(3)
    k_ref[...] = k.astype(BF16)
    for blk in range(rows // MOBA_BLOCK):
        r0 = blk * MOBA_BLOCK
        kmean_ref[0, blk:blk + 1, :] = jnp.mean(k[r0:r0 + MOBA_BLOCK, :], axis=0, keepdims=True)


def _proj_call(x2, batch, norm_g, w_in, b_gates, ln_g, ln_b, w_spatial, b_spatial):
    t, d = x2.shape
    rows = PROJ_ROWS
    seq = t // batch
    assert seq % rows == 0 and rows % MOBA_BLOCK == 0 and rows % GMLP_CHUNK == 0
    n_tiles = t // rows
    tiles_per_seq = seq // rows
    blocks_per_tile = rows // MOBA_BLOCK
    nb = seq // MOBA_BLOCK
    row_spec = pl.BlockSpec((rows, d), lambda i: (i, 0))
    const2 = lambda i: (0, 0)
    const3 = lambda i: (0, 0, 0)
    act = jax.ShapeDtypeStruct((t, d), BF16)
    qT_spec = pl.BlockSpec((None, ATT_HEADS, HEAD_DIM, rows),
                           lambda i: (i // tiles_per_seq, 0, 0, i % tiles_per_seq))
    vT_spec = pl.BlockSpec((None, ATT_HEADS, blocks_per_tile, V_ROWS, MOBA_BLOCK),
                           lambda i: (i // tiles_per_seq, 0, i % tiles_per_seq, 0, 0))
    return pl.pallas_call(
        _proj_kernel,
        grid=(n_tiles,),
        in_specs=[
            row_spec,
            pl.BlockSpec((1, d), const2),
            pl.BlockSpec(memory_space=pl.ANY),
            pl.BlockSpec((1, 2 * d), const2),
            pl.BlockSpec((1, d), const2),
            pl.BlockSpec((1, d), const2),
            pl.BlockSpec((GMLP_GROUPS, GMLP_CHUNK, GMLP_CHUNK), const3),
            pl.BlockSpec((GMLP_GROUPS, GMLP_CHUNK, 1), const3),
        ],
        out_specs=[row_spec, qT_spec, row_spec, vT_spec, row_spec,
                   pl.BlockSpec((1, blocks_per_tile, d), lambda i: (i, 0, 0))],
        out_shape=[act,
                   jax.ShapeDtypeStruct((batch, ATT_HEADS, HEAD_DIM, seq), BF16),
                   act,
                   jax.ShapeDtypeStruct((batch, ATT_HEADS, nb, V_ROWS, MOBA_BLOCK), BF16),
                   act,
                   jax.ShapeDtypeStruct((n_tiles, blocks_per_tile, d), F32)],
        scratch_shapes=[pltpu.VMEM((d, N_SEGMENTS * d), BF16),
                        pltpu.VMEM((2, W_STAGE_ROWS, N_SEGMENTS * d), F32),
                        pltpu.SemaphoreType.DMA((2,)),
                        pltpu.VMEM((rows, d), BF16), pltpu.VMEM((rows, d), BF16),
                        pltpu.VMEM((rows, d), F32)],
        compiler_params=pltpu.CompilerParams(
            dimension_semantics=("arbitrary",), vmem_limit_bytes=VMEM_LIMIT_BYTES),
        name="proj_gmlp",
    )(x2, norm_g, w_in, b_gates, ln_g, ln_b, w_spatial, b_spatial)


def _t5_bucket_np(n):
    n = np.maximum(n, 0)
    max_exact = REL_BUCKETS // 2
    nf = np.maximum(n, max_exact).astype(np.float32)
    large = max_exact + (np.log(nf / max_exact) / math.log(REL_MAX_DIST / max_exact)
                         * (REL_BUCKETS - max_exact)).astype(np.int32)
    large = np.minimum(large, REL_BUCKETS - 1)
    return np.where(n < max_exact, n, large).astype(np.int32)


def _distance_buckets():
    dist = np.arange(N_BIAS_TILES * MOBA_BLOCK, dtype=np.int32)
    return np.tile(_t5_bucket_np(dist)[None, :], (TOKEN_SUBLANES, 1))


def _bias_kernel(relb_ref, bucket_ref, out_ref):
    k_idx = lax.broadcasted_iota(jnp.int32, (MOBA_BLOCK, MOBA_BLOCK), 0)
    q_idx = lax.broadcasted_iota(jnp.int32, (MOBA_BLOCK, MOBA_BLOCK), 1)
    bucket = bucket_ref[...]
    for h in range(ATT_HEADS):
        far = relb_ref[REL_BUCKETS - 1, h]
        by_dist = jnp.zeros(bucket.shape, F32)
        for b in range(REL_BUCKETS):
            by_dist = jnp.where(bucket == b, relb_ref[b, h], by_dist)
        by_dist = (by_dist - far) * LOG2E
        rows = jnp.broadcast_to(by_dist[:1], (MOBA_BLOCK, N_BIAS_TILES * MOBA_BLOCK))
        toeplitz = pltpu.roll(rows, 0, 1, stride=1, stride_axis=0)
        out_ref[h, BIAS_OWN] = jnp.where(q_idx >= k_idx, toeplitz[:, :MOBA_BLOCK], MASK_NEG)
        out_ref[h, BIAS_PREV] = toeplitz[:, MOBA_BLOCK:]


def _bias_call(rel_bias):
    return pl.pallas_call(
        _bias_kernel,
        in_specs=[pl.BlockSpec(memory_space=pltpu.SMEM),
                  pl.BlockSpec(memory_space=pltpu.VMEM)],
        out_specs=pl.BlockSpec(memory_space=pltpu.VMEM),
        out_shape=jax.ShapeDtypeStruct((ATT_HEADS, N_BIAS_TILES, MOBA_BLOCK, MOBA_BLOCK), F32),
        name="t5_bias_tiles",
    )(rel_bias, jnp.asarray(_distance_buckets()))


BIAS_OWN, BIAS_PREV = range(2)
N_BIAS_TILES = 2


def _attn_kernel(qT_ref, k_ref, vT_ref, kmh_ref, kml_ref, oh_ref, bias_ref, o_ref,
                 qp_scr, m_scr, acc_scr):
    i = pl.program_id(1)
    nh, nb = kmh_ref.shape[0], kmh_ref.shape[1]
    blk = MOBA_BLOCK
    hd = HEAD_DIM

    n_idx = lax.broadcasted_iota(jnp.int32, (nb, blk), 0)
    past = n_idx < i
    for h in range(nh):
        qT = qT_ref[h]
        gate = _dot(kmh_ref[h], qT) + _dot(kml_ref[h], qT)
        gate = jnp.where(past, gate, -jnp.inf)
        rank = jnp.zeros((nb, blk), F32)
        for m in range(nb):
            row = gate[m:m + 1, :]
            beats = (row > gate) | ((row == gate) & (m < n_idx))
            rank = rank + jnp.where(beats, 1.0, 0.0)
        keep = (past & (rank < MOBA_TOPK)) | (n_idx == i)
        sel = jnp.where(keep, 0.0, MASK_NEG)
        sel = jnp.concatenate([sel, jnp.zeros((hd - nb, blk), F32)], axis=0)
        qp_scr[h] = jnp.concatenate([qT, sel.astype(BF16)], axis=0)

    def scores(h, j):
        r0 = pl.multiple_of(j * blk, blk)
        kp = jnp.concatenate([k_ref[pl.ds(r0, blk), h * hd:(h + 1) * hd], oh_ref[j]], axis=1)
        return _dot(kp, qp_scr[h])

    def fold(blocks, bias_tiles, first):
        ss = []
        for h in range(nh):
            parts = []
            for j, tile in zip(blocks, bias_tiles):
                s = scores(h, j)
                parts.append((s if tile is None else s + bias_ref[h, tile]).astype(BF16))
            ss.append(parts)
        ps, alphas = [], []
        for h in range(nh):
            m_new = functools.reduce(
                jnp.maximum, [jnp.max(s, axis=0, keepdims=True) for s in ss[h]]).astype(F32)
            if not first:
                m_old = m_scr[h]
                m_new = jnp.maximum(m_old, m_new)
                alphas.append(jnp.exp2(m_old - m_new))
            m_scr[h] = m_new
            ps.append([jnp.exp2(s - m_new.astype(BF16)) for s in ss[h]])
        for h in range(nh):
            pv = functools.reduce(
                lambda a, b: a + b, [_dot(vT_ref[h, j], x) for j, x in zip(blocks, ps[h])])
            acc_scr[h] = pv if first else alphas[h] * acc_scr[h] + pv

    @pl.when(i == 0)
    def _():
        fold([i], [BIAS_OWN], first=True)

    @pl.when(i >= 1)
    def _():
        fold([i, i - 1], [BIAS_OWN, BIAS_PREV], first=True)

    n_far = i - 1

    def far_pair(p, carry):
        fold([2 * p, 2 * p + 1], [None, None], first=False)
        return carry

    lax.fori_loop(0, n_far // 2, far_pair, 0)

    @pl.when((n_far >= 1) & (n_far % 2 == 1))
    def _():
        fold([n_far - 1], [None], first=False)

    for h in range(nh):
        y = acc_scr[h, :hd, :] * (1.0 / acc_scr[h, hd:hd + 1, :])
        o_ref[:, h * hd:(h + 1) * hd] = y.T.astype(o_ref.dtype)


def _attn_call(qT, k2, vT, km_hi, km_lo, bias_tiles):
    b, h, nb, v_rows, blk = vT.shape
    hd = HEAD_DIM
    s = nb * blk
    onehot = np.zeros((nb, blk, LANES), np.float32)
    for j in range(nb):
        onehot[j, :, j] = 1.0
    once = pl.Buffered(1)
    return pl.pallas_call(
        _attn_kernel,
        grid=(b, nb),
        in_specs=[
            pl.BlockSpec((None, h, hd, blk), lambda bi, i: (bi, 0, 0, i)),
            pl.BlockSpec((s, h * hd), lambda bi, i: (bi, 0)),
            pl.BlockSpec((None, h, nb, v_rows, blk), lambda bi, i: (bi, 0, 0, 0, 0)),
            pl.BlockSpec((None, h, nb, hd), lambda bi, i: (bi, 0, 0, 0)),
            pl.BlockSpec((None, h, nb, hd), lambda bi, i: (bi, 0, 0, 0)),
            pl.BlockSpec((nb, blk, LANES), lambda bi, i: (0, 0, 0), pipeline_mode=once),
            pl.BlockSpec((h, N_BIAS_TILES, blk, blk), lambda bi, i: (0, 0, 0, 0),
                         pipeline_mode=once),
        ],
        out_specs=pl.BlockSpec((blk, h * hd), lambda bi, i: (bi * nb + i, 0)),
        out_shape=jax.ShapeDtypeStruct((b * s, h * hd), BF16),
        scratch_shapes=[pltpu.VMEM((h, 2 * hd, blk), BF16), pltpu.VMEM((h, 1, blk), F32),
                        pltpu.VMEM((h, v_rows, blk), F32)],
        compiler_params=pltpu.CompilerParams(
            dimension_semantics=("arbitrary", "arbitrary"),
            vmem_limit_bytes=VMEM_LIMIT_BYTES),
        name="moba_attention",
    )(qT, k2, vT, km_hi, km_lo, jnp.asarray(onehot, BF16), bias_tiles)


ROUTE_E1, ROUTE_E2, ROUTE_W1, ROUTE_W2, ROUTE_R1, ROUTE_R2 = range(6)


def _dot_nt(a, b):
    return lax.dot_general(a, b, (((1,), (1,)), ((), ())), preferred_element_type=F32)


def _store_token_major(ref, x):
    rows = x.shape[0]
    for s in range(TOKEN_SUBLANES):
        ref[pl.ds(s, rows, stride=TOKEN_SUBLANES), :] = x[:, s * LANES:(s + 1) * LANES]


def _load_token_major(ref, rows):
    return jnp.concatenate(
        [ref[pl.ds(s, rows, stride=TOKEN_SUBLANES), :] for s in range(TOKEN_SUBLANES)], axis=1)


def _merge_kernel(x_ref, ya_ref, gb_ref, yb_ref, wo_ref, ng_ref, wrh_ref, wrl_ref, br_ref,
                  h_ref, xn_ref, route_ref, route_t_ref, counts_ref, run_scr, wo_scr):
    @pl.when(pl.program_id(0) == 0)
    def _():
        run_scr[...] = jnp.zeros_like(run_scr)
        wo_scr[...] = wo_ref[...].astype(BF16)

    f = lambda r: r[...].astype(F32)
    mix = (f(ya_ref) + f(gb_ref) * f(yb_ref)).astype(BF16)
    h = x_ref[...] + _dot(mix, wo_scr[...])
    h_ref[...] = h
    xn = _rmsnorm(h, ng_ref[...])
    _store_token_major(xn_ref, xn)
    rows = xn.shape[0]

    x_hi = xn.astype(BF16)
    x_lo = (xn - x_hi.astype(F32)).astype(BF16)
    logits = (_dot_nt(wrh_ref[...], x_hi) + _dot_nt(wrh_ref[...], x_lo)
              + _dot_nt(wrl_ref[...], x_hi) + br_ref[...])
    unit = lax.broadcasted_iota(jnp.int32, logits.shape, 0).astype(F32)
    big = float(ROUTER_UNITS)
    neg_inf = -jnp.inf

    gl = jnp.where((unit >= GROUP_UNIT0) & (unit < GROUP_UNIT0 + N_GROUPS), logits, neg_inf)
    gmax = jnp.max(gl, axis=0, keepdims=True)
    g_w = 1.0 / jnp.sum(jnp.exp(gl - gmax), axis=0, keepdims=True)
    g_idx = jnp.min(jnp.where(gl == gmax, unit, big), axis=0, keepdims=True) - GROUP_UNIT0

    e0 = g_idx * EXPERTS_PER_GROUP
    el = jnp.where((unit >= e0) & (unit < e0 + EXPERTS_PER_GROUP), logits, neg_inf)
    m1 = jnp.max(el, axis=0, keepdims=True)
    i1 = jnp.min(jnp.where(el == m1, unit, big), axis=0, keepdims=True)
    el2 = jnp.where(unit == i1, neg_inf, el)
    m2 = jnp.max(el2, axis=0, keepdims=True)
    i2 = jnp.min(jnp.where(el2 == m2, unit, big), axis=0, keepdims=True)
    e2 = jnp.exp(m2 - m1)
    den = 1.0 + e2
    w1 = (1.0 / den) * g_w
    w2 = (e2 / den) * g_w

    hit1 = unit == i1
    hit2 = unit == i2
    onehot = jnp.where(hit1, 1.0, jnp.where(hit2, 1.0, 0.0))
    c_idx = lax.broadcasted_iota(jnp.int32, (rows, rows), 0)
    r_idx = lax.broadcasted_iota(jnp.int32, (rows, rows), 1)
    earlier = jnp.where(c_idx < r_idx, 1.0, 0.0).astype(BF16)
    prefix = run_scr[...] + _dot(onehot.astype(BF16), earlier)
    rank1 = jnp.sum(jnp.where(hit1, prefix, 0.0), axis=0, keepdims=True)
    rank2 = jnp.sum(jnp.where(hit2, prefix, 0.0), axis=0, keepdims=True)
    run_scr[...] = run_scr[...] + jnp.sum(onehot, axis=1, keepdims=True)
    counts_ref[...] = run_scr[...]

    route_t = jnp.concatenate(
        [i1, i2, w1, w2, rank1, rank2, jnp.zeros((ROUTER_LANES - 6, rows), F32)], axis=0)
    route_t_ref[0] = route_t[:ROUTE_ROWS]
    route_ref[...] = route_t.T


def _merge_call(x2, ya, gb, yb, w_out, norm_g, wr_hi, wr_lo, b_router):
    t, d = x2.shape
    rows = MERGE_ROWS
    assert t % rows == 0 and d == TOKEN_SUBLANES * LANES
    n_tiles = t // rows
    row_spec = pl.BlockSpec((rows, d), lambda i: (i, 0))
    const2 = lambda i: (0, 0)
    return pl.pallas_call(
        _merge_kernel,
        grid=(n_tiles,),
        in_specs=[row_spec, row_spec, row_spec, row_spec,
                  pl.BlockSpec((d, d), const2, pipeline_mode=pl.Buffered(1)),
                  pl.BlockSpec((1, d), const2),
                  pl.BlockSpec((ROUTER_UNITS, d), const2),
                  pl.BlockSpec((ROUTER_UNITS, d), const2),
                  pl.BlockSpec((ROUTER_UNITS, 1), const2)],
        out_specs=[row_spec,
                   pl.BlockSpec((rows * TOKEN_SUBLANES, LANES), lambda i: (i, 0)),
                   pl.BlockSpec((rows, ROUTER_LANES), lambda i: (i, 0)),
                   pl.BlockSpec((1, ROUTE_ROWS, rows), lambda i: (i, 0, 0)),
                   pl.BlockSpec((ROUTER_UNITS, 1), const2)],
        out_shape=[jax.ShapeDtypeStruct((t, d), F32),
                   jax.ShapeDtypeStruct((t * TOKEN_SUBLANES, LANES), F32),
                   jax.ShapeDtypeStruct((t, ROUTER_LANES), F32),
                   jax.ShapeDtypeStruct((n_tiles, ROUTE_ROWS, rows), F32),
                   jax.ShapeDtypeStruct((ROUTER_UNITS, 1), F32)],
        scratch_shapes=[pltpu.VMEM((ROUTER_UNITS, 1), F32), pltpu.VMEM((d, d), BF16)],
        compiler_params=pltpu.CompilerParams(
            dimension_semantics=("arbitrary",), vmem_limit_bytes=VMEM_LIMIT_BYTES),
        name="merge_outproj_router",
    )(x2, ya, gb, yb, w_out, norm_g, wr_hi, wr_lo, b_router)


def _token_rows(ref, token):
    return ref.at[pl.ds(pl.multiple_of(token * TOKEN_SUBLANES, TOKEN_SUBLANES), TOKEN_SUBLANES)]


def _dispatch_kernel(last_ref, nreal_ref, pos1_ref, pos2_ref, xn_ref, xs_hbm, zero_scr, sem):
    rows = pos1_ref.shape[2]
    tile = EXPERT_ROWS
    n_tiles = xs_hbm.shape[0] // (tile * TOKEN_SUBLANES)

    def zero_tile(j):
        start = pl.multiple_of(j * (tile * TOKEN_SUBLANES), tile * TOKEN_SUBLANES)
        return pltpu.make_async_copy(
            zero_scr, xs_hbm.at[pl.ds(start, tile * TOKEN_SUBLANES)], sem)

    @pl.when(pl.program_id(0) == 0)
    def _():
        zero_scr[...] = jnp.zeros_like(zero_scr)
        for e in range(N_EXPERTS):
            @pl.when(last_ref[e] >= 0)
            def _():
                zero_tile(last_ref[e]).start()

        def tail_start(j, carry):
            zero_tile(j).start()
            return carry

        lax.fori_loop(nreal_ref[0], n_tiles, tail_start, 0)

        for e in range(N_EXPERTS):
            @pl.when(last_ref[e] >= 0)
            def _():
                zero_tile(0).wait()

        def tail_wait(j, carry):
            zero_tile(0).wait()
            return carry

        lax.fori_loop(nreal_ref[0], n_tiles, tail_wait, 0)

    def issue(g, carry):
        for u in range(DMA_UNROLL):
            r = g * DMA_UNROLL + u
            src = _token_rows(xn_ref, r)
            pltpu.make_async_copy(src, _token_rows(xs_hbm, pos1_ref[0, 0, r]), sem).start(0)
            pltpu.make_async_copy(src, _token_rows(xs_hbm, pos2_ref[0, 0, r]), sem).start(1)
        return carry

    lax.fori_loop(0, rows // DMA_UNROLL, issue, 0)

    for _ in range(2):
        pltpu.make_async_copy(xn_ref, xs_hbm.at[pl.ds(0, rows * TOKEN_SUBLANES)], sem).wait()


def _dispatch_call(last_tile, n_real, pos1, pos2, xn, n_sorted_rows):
    n_steps, _, rows = pos1.shape
    smem_row = pl.BlockSpec((1, 1, rows), lambda i, lt, nr: (i, 0, 0), memory_space=pltpu.SMEM)
    return pl.pallas_call(
        _dispatch_kernel,
        grid_spec=pltpu.PrefetchScalarGridSpec(
            num_scalar_prefetch=2,
            grid=(n_steps,),
            in_specs=[smem_row, smem_row,
                      pl.BlockSpec((rows * TOKEN_SUBLANES, LANES), lambda i, lt, nr: (i, 0))],
            out_specs=pl.BlockSpec(memory_space=pl.ANY),
            scratch_shapes=[pltpu.VMEM((EXPERT_ROWS * TOKEN_SUBLANES, LANES), F32),
                            pltpu.SemaphoreType.DMA(())],
        ),
        out_shape=jax.ShapeDtypeStruct((n_sorted_rows * TOKEN_SUBLANES, LANES), F32),
        compiler_params=pltpu.CompilerParams(dimension_semantics=("arbitrary",)),
        name="moe_dispatch",
    )(last_tile, n_real, pos1, pos2, xn)


EXPERT_IN_SLOTS = 6
EXPERT_OUT_SLOTS = 4


def _expert_kernel(first_ref, end_ref, nreal_ref, xs_hbm, w1_ref, w3_ref, w2_ref, ys_hbm,
                   w1_scr, w3_scr, w2_scr, xbuf, ybuf, in_sems, out_sems):
    e = pl.program_id(0)
    tile_rows = EXPERT_ROWS * TOKEN_SUBLANES
    n_real = nreal_ref[0]
    n_tiles = xs_hbm.shape[0] // tile_rows

    def tile_of(ref, t):
        return ref.at[pl.ds(pl.multiple_of(t * tile_rows, tile_rows), tile_rows)]

    def in_copy(t):
        slot = t % EXPERT_IN_SLOTS
        return pltpu.make_async_copy(tile_of(xs_hbm, t), xbuf.at[slot], in_sems.at[slot])

    def out_copy(t):
        slot = t % EXPERT_OUT_SLOTS
        return pltpu.make_async_copy(ybuf.at[slot], tile_of(ys_hbm, t), out_sems.at[slot])

    @pl.when(e == 0)
    def _():
        for t in range(EXPERT_IN_SLOTS - 1):
            @pl.when(t < n_real)
            def _():
                in_copy(t).start()

    w1_scr[...] = w1_ref[...].astype(BF16)
    w3_scr[...] = w3_ref[...].astype(BF16)
    w2_scr[...] = w2_ref[...].astype(BF16)

    def tile(t, carry):
        ahead = t + EXPERT_IN_SLOTS - 1

        @pl.when(ahead < n_real)
        def _():
            in_copy(ahead).start()

        in_copy(t).wait()

        @pl.when(t >= EXPERT_OUT_SLOTS)
        def _():
            out_copy(t - EXPERT_OUT_SLOTS).wait()

        x = _load_token_major(xbuf.at[t % EXPERT_IN_SLOTS], EXPERT_ROWS).astype(BF16)
        a = _dot(x, w1_scr[...])
        b = _dot(x, w3_scr[...])
        hid = (a * _sigmoid(a)) * b
        _store_token_major(ybuf.at[t % EXPERT_OUT_SLOTS], _dot(hid.astype(BF16), w2_scr[...]))
        out_copy(t).start()
        return carry

    lax.fori_loop(first_ref[e], end_ref[e], tile, 0)

    @pl.when(e == pl.num_programs(0) - 1)
    def _():
        for back in range(EXPERT_OUT_SLOTS, 0, -1):
            @pl.when(n_real - back >= 0)
            def _():
                out_copy(n_real - back).wait()
        ybuf[0] = jnp.zeros(ybuf.shape[1:], F32)

        def tail_start(t, carry):
            pltpu.make_async_copy(ybuf.at[0], tile_of(ys_hbm, t), out_sems.at[0]).start()
            return carry

        def tail_wait(t, carry):
            pltpu.make_async_copy(ybuf.at[0], tile_of(ys_hbm, t), out_sems.at[0]).wait()
            return carry

        lax.fori_loop(n_real, n_tiles, tail_start, 0)
        lax.fori_loop(n_real, n_tiles, tail_wait, 0)


def _expert_call(first_tile, end_tile, n_real, xs, w1, w3, w2):
    n_experts, d, d_expert = w1.shape
    tile_rows = EXPERT_ROWS * TOKEN_SUBLANES
    per_expert = lambda e, f, n, nr: (e, 0, 0)
    return pl.pallas_call(
        _expert_kernel,
        grid_spec=pltpu.PrefetchScalarGridSpec(
            num_scalar_prefetch=3,
            grid=(n_experts,),
            in_specs=[pl.BlockSpec(memory_space=pl.ANY),
                      pl.BlockSpec((None, d, d_expert), per_expert),
                      pl.BlockSpec((None, d, d_expert), per_expert),
                      pl.BlockSpec((None, d_expert, d), per_expert)],
            out_specs=pl.BlockSpec(memory_space=pl.ANY),
            scratch_shapes=[pltpu.VMEM((d, d_expert), BF16), pltpu.VMEM((d, d_expert), BF16),
                            pltpu.VMEM((d_expert, d), BF16),
                            pltpu.VMEM((EXPERT_IN_SLOTS, tile_rows, LANES), F32),
                            pltpu.VMEM((EXPERT_OUT_SLOTS, tile_rows, LANES), F32),
                            pltpu.SemaphoreType.DMA((EXPERT_IN_SLOTS,)),
                            pltpu.SemaphoreType.DMA((EXPERT_OUT_SLOTS,))],
        ),
        out_shape=jax.ShapeDtypeStruct(xs.shape, F32),
        compiler_params=pltpu.CompilerParams(
            dimension_semantics=("arbitrary",), vmem_limit_bytes=VMEM_LIMIT_BYTES),
        name="moe_experts",
    )(first_tile, end_tile, n_real, xs, w1, w3, w2)


def _combine_kernel(p1_first, p2_first, p1_next, p2_next, ys_hbm, h_ref, route_ref, ng_ref,
                    out_ref, buf, sems):
    i = pl.program_id(0)
    n = pl.num_programs(0)
    rows = h_ref.shape[0]

    def issue(p1_ref, p2_ref, slot):
        def body(g, carry):
            for u in range(DMA_UNROLL):
                r = g * DMA_UNROLL + u
                for which, p_ref in ((0, p1_ref), (1, p2_ref)):
                    pltpu.make_async_copy(_token_rows(ys_hbm, p_ref[0, 0, r]),
                                          _token_rows(buf.at[2 * slot + which], r),
                                          sems.at[slot]).start(which)
            return carry
        lax.fori_loop(0, rows // DMA_UNROLL, body, 0)

    @pl.when(i == 0)
    def _():
        issue(p1_first, p2_first, 0)

    @pl.when(i + 1 < n)
    def _():
        issue(p1_next, p2_next, (i + 1) % 2)

    slot = i % 2
    for which in range(2):
        pltpu.make_async_copy(ys_hbm.at[pl.ds(0, rows * TOKEN_SUBLANES)],
                              buf.at[2 * slot + which], sems.at[slot]).wait()

    route = route_ref[...]
    w1 = route[:, ROUTE_W1:ROUTE_W1 + 1]
    w2 = route[:, ROUTE_W2:ROUTE_W2 + 1]
    y = (h_ref[...] + w1 * _load_token_major(buf.at[2 * slot], rows)
         + w2 * _load_token_major(buf.at[2 * slot + 1], rows))
    out_ref[...] = _rmsnorm(y, ng_ref[...])


def _combine_call(pos1, pos2, ys, h, route, norm_g):
    t, d = h.shape
    n_steps, _, rows = pos1.shape
    row_spec = pl.BlockSpec((rows, d), lambda i: (i, 0))
    first = pl.BlockSpec((1, 1, rows), lambda i: (0, 0, 0), memory_space=pltpu.SMEM)
    nxt = pl.BlockSpec((1, 1, rows), lambda i: (jnp.minimum(i + 1, n_steps - 1), 0, 0),
                       memory_space=pltpu.SMEM)
    return pl.pallas_call(
        _combine_kernel,
        grid=(n_steps,),
        in_specs=[first, first, nxt, nxt,
                  pl.BlockSpec(memory_space=pl.ANY),
                  row_spec,
                  pl.BlockSpec((rows, ROUTER_LANES), lambda i: (i, 0)),
                  pl.BlockSpec((1, d), lambda i: (0, 0))],
        out_specs=row_spec,
        out_shape=jax.ShapeDtypeStruct((t, d), F32),
        scratch_shapes=[pltpu.VMEM((4, rows * TOKEN_SUBLANES, LANES), F32),
                        pltpu.SemaphoreType.DMA((2,))],
        compiler_params=pltpu.CompilerParams(
            dimension_semantics=("arbitrary",), vmem_limit_bytes=VMEM_LIMIT_BYTES),
        name="moe_combine",
    )(pos1, pos2, pos1, pos2, ys, h, route, norm_g)


def _sparse_moe(xn, route, route_t, counts, h, w1, w3, w2, norm_g):
    t = h.shape[0]
    tile = EXPERT_ROWS
    n_tiles = (2 * t) // tile + N_EXPERTS
    expert = jnp.arange(N_EXPERTS, dtype=jnp.int32)
    counts = counts[:N_EXPERTS, 0].astype(jnp.int32)
    group_tiles = (counts + tile - 1) // tile
    end_tile = jnp.sum(jnp.where(expert[None, :] <= expert[:, None], group_tiles[None, :], 0), axis=1)
    first_tile = end_tile - group_tiles
    n_real = end_tile[-1:]
    last_tile = jnp.where(group_tiles > 0, end_tile - 1, -1)

    def positions(e_row, r_row):
        e = route_t[:, e_row, :].astype(jnp.int32)
        start = jnp.zeros_like(e)
        for k in range(N_EXPERTS):
            start = jnp.where(e == k, first_tile[k] * tile, start)
        return start + route_t[:, r_row, :].astype(jnp.int32)

    pos1 = positions(ROUTE_E1, ROUTE_R1)
    pos2 = positions(ROUTE_E2, ROUTE_R2)
    per_step = lambda pos, rows: pos.reshape(t // rows, 1, rows)
    xs = _dispatch_call(last_tile, n_real, per_step(pos1, DISPATCH_ROWS),
                        per_step(pos2, DISPATCH_ROWS), xn, n_tiles * tile)
    ys = _expert_call(first_tile, end_tile, n_real, xs, w1, w3, w2)
    return _combine_call(per_step(pos1, COMBINE_ROWS), per_step(pos2, COMBINE_ROWS), ys, h,
                         route, norm_g)


def _layer(h, norm_mix_g, w_in, b_gates, gmlp_ln_g, gmlp_ln_b, w_spatial, b_spatial, bias_tiles,
           w_out, norm_ffn_g, w_group_router, b_group_router, w_expert_router, b_expert_router,
           w1, w3, w2, norm_out_g):
    b, s, d = h.shape
    t = b * s
    nb = s // MOBA_BLOCK
    x2 = h.reshape(t, d)
    row = lambda v: v.reshape(1, -1)

    ya, qT, k, vT, gb, kmean = _proj_call(
        x2, b, row(norm_mix_g), w_in, row(b_gates), row(gmlp_ln_g), row(gmlp_ln_b),
        w_spatial, b_spatial[:, :, None])

    km = jnp.transpose(kmean.reshape(b, nb, ATT_HEADS, HEAD_DIM), (0, 2, 1, 3))
    km_hi = km.astype(BF16)
    km_lo = (km - km_hi.astype(F32)).astype(BF16)
    yb = _attn_call(qT, k, vT, km_hi, km_lo, bias_tiles)

    w_router = jnp.concatenate(
        [jnp.transpose(w_expert_router, (0, 2, 1)).reshape(N_EXPERTS, d), w_group_router.T,
         jnp.zeros((ROUTER_UNITS - N_EXPERTS - N_GROUPS, d), F32)], axis=0)
    b_router = jnp.concatenate(
        [b_expert_router.reshape(-1), b_group_router,
         jnp.zeros((ROUTER_UNITS - N_EXPERTS - N_GROUPS,), F32)]).reshape(ROUTER_UNITS, 1)
    wr_hi = w_router.astype(BF16)
    wr_lo = (w_router - wr_hi.astype(F32)).astype(BF16)
    h2, xn, route, route_t, counts = _merge_call(
        x2, ya, gb, yb, w_out, row(norm_ffn_g), wr_hi, wr_lo, b_router)

    out = _sparse_moe(xn, route, route_t, counts, h2, w1, w3, w2, row(norm_out_g))
    return out.reshape(b, s, d)


def kernel(x, norm_mix_g, w_in, b_gates, gmlp_ln_g, gmlp_ln_b, w_spatial, b_spatial, rel_bias, w_out, norm_ffn_g, w_group_router, b_group_router, w_expert_router, b_expert_router, w1, w3, w2, norm_final_g):
    depth = w_in.shape[0]
    assert depth == 1, "the final rmsnorm is fused into the last layer's combine kernel"
    bias_tiles = _bias_call(rel_bias)
    return _layer(x, norm_mix_g[0], w_in[0], b_gates[0], gmlp_ln_g[0], gmlp_ln_b[0], w_spatial[0],
                  b_spatial[0], bias_tiles, w_out[0], norm_ffn_g[0], w_group_router[0],
                  b_group_router[0], w_expert_router[0], b_expert_router[0], w1[0], w3[0], w2[0],
                  norm_final_g)
```

```python
import functools
import math

import numpy as np
import jax
import jax.numpy as jnp
from jax import lax
from jax.experimental import pallas as pl
from jax.experimental.pallas import tpu as pltpu

F32 = jnp.float32
BF16 = jnp.bfloat16

D_MODEL = 1024
NORM_EPS = 1e-6
GMLP_GROUPS = 8
GMLP_CHUNK = 128
ATT_HEADS = 8
HEAD_DIM = 128
MOBA_BLOCK = 256
MOBA_TOPK = 3
REL_BUCKETS = 32
REL_MAX_DIST = 128
N_GROUPS = 4
EXPERTS_PER_GROUP = 8
N_EXPERTS = N_GROUPS * EXPERTS_PER_GROUP
N_SEGMENTS = 7

LANES = 128
TOKEN_SUBLANES = 8
ROUTE_ROWS = 8
VMEM_LIMIT_BYTES = 56 * 1024 * 1024

SQRT_HALF = math.sqrt(0.5)
LOG2E = math.log2(math.e)
SCORE_SCALE2 = (HEAD_DIM ** -0.5) * LOG2E
MASK_NEG = -(2.0 ** 100)
BF16_SUBLANES = 16
V_ROWS = HEAD_DIM + BF16_SUBLANES
ROUTER_LANES = LANES
ROUTER_UNITS = -(-(N_EXPERTS + N_GROUPS) // BF16_SUBLANES) * BF16_SUBLANES
GROUP_UNIT0 = N_EXPERTS

PROJ_ROWS = 512
W_STAGE_COLS = 512
SEGMENT_ORDER = (1, 0, 5, 2, 4, 6, 3)
MERGE_ROWS = 1024
EXPERT_ROWS = 256
DISPATCH_ROWS = 2048
COMBINE_ROWS = 256
DMA_UNROLL = 8


def _rmsnorm(x, g):
    return x * lax.rsqrt(jnp.mean(x * x, axis=-1, keepdims=True) + NORM_EPS) * g


def _gelu(a):
    return 0.5 * a * (1.0 + lax.erf(a * SQRT_HALF))


def _sigmoid(a):
    return 1.0 / (1.0 + jnp.exp(-a))


def _dot(a, b):
    return jnp.dot(a, b, preferred_element_type=F32)


def _proj_kernel(*refs):
    first = pl.program_id(0) == 0
    pl.when(first)(functools.partial(_proj_body, True, *refs))
    pl.when(jnp.logical_not(first))(functools.partial(_proj_body, False, *refs))


def _proj_body(load_weight, x_ref, ng_ref, w_hbm, bg_ref, lng_ref, lnb_ref, ws_ref, bs_ref,
               ya_ref, qT_ref, k_ref, vT_ref, gb_ref, kmean_ref,
               w_ref, w_stage, w_sems, xn_scr, vln_scr, mix_scr):
    rows = x_ref.shape[0]
    d = D_MODEL

    chunk_cols = w_stage.shape[2]
    chunks = [s * d + c * chunk_cols for s in SEGMENT_ORDER for c in range(d // chunk_cols)]

    def chunk_copy(n):
        slot = n % 2
        return pltpu.make_async_copy(
            w_hbm.at[:, pl.ds(chunks[n], chunk_cols)], w_stage.at[slot], w_sems.at[slot])

    if load_weight:
        chunk_copy(0).start()
        chunk_copy(1).start()

    xn_scr[...] = _rmsnorm(x_ref[...], ng_ref[...]).astype(BF16)
    used = []

    def seg(i):
        used.append(i)
        assert tuple(used) == SEGMENT_ORDER[:len(used)]
        if load_weight:
            per_seg = d // chunk_cols
            for n in range((len(used) - 1) * per_seg, len(used) * per_seg):
                chunk_copy(n).wait()
                w_ref[:, chunks[n]:chunks[n] + chunk_cols] = w_stage[n % 2].astype(BF16)
                if n + 2 < len(chunks):
                    chunk_copy(n + 2).start()
        return _dot(xn_scr[...], w_ref[:, i * d:(i + 1) * d])

    hd = HEAD_DIM

    v = _gelu(seg(1))
    mu = jnp.mean(v, axis=-1, keepdims=True)
    vc = v - mu
    var = jnp.mean(vc * vc, axis=-1, keepdims=True)
    vln_scr[...] = (vc * lax.rsqrt(var + NORM_EPS) * lng_ref[...] + lnb_ref[...]).astype(BF16)

    mix_scr[...] = _gelu(seg(0)) * _sigmoid(seg(5) + bg_ref[:, :d])

    q = seg(2) * SCORE_SCALE2
    for h in range(ATT_HEADS):
        qT_ref[h] = q[:, h * hd:(h + 1) * hd].T.astype(BF16)

    t_idx = lax.broadcasted_iota(jnp.int32, (GMLP_CHUNK, GMLP_CHUNK), 0)
    s_idx = lax.broadcasted_iota(jnp.int32, (GMLP_CHUNK, GMLP_CHUNK), 1)
    causal = t_idx >= s_idx
    gd = d // GMLP_GROUPS
    n_chunks = rows // GMLP_CHUNK
    for g in range(GMLP_GROUPS):
        ws = jnp.where(causal, ws_ref[g], 0.0).astype(BF16)
        bias = bs_ref[g]
        vg = jnp.concatenate(
            [vln_scr[c * GMLP_CHUNK:(c + 1) * GMLP_CHUNK, g * gd:(g + 1) * gd]
             for c in range(n_chunks)], axis=1)
        mixed = _dot(ws, vg)
        for c in range(n_chunks):
            blk_rows = slice(c * GMLP_CHUNK, (c + 1) * GMLP_CHUNK)
            blk_cols = slice(g * gd, (g + 1) * gd)
            ya_ref[blk_rows, blk_cols] = (
                mix_scr[blk_rows, blk_cols] * (mixed[:, c * gd:(c + 1) * gd] + bias)).astype(BF16)

    v = seg(4)
    for blk in range(rows // MOBA_BLOCK):
        r0 = blk * MOBA_BLOCK
        for h in range(ATT_HEADS):
            vT_ref[h, blk, :hd, :] = v[r0:r0 + MOBA_BLOCK, h * hd:(h + 1) * hd].T.astype(BF16)
            vT_ref[h, blk, hd:, :] = jnp.ones((V_ROWS - hd, MOBA_BLOCK), BF16)

    gb_ref[...] = _sigmoid(seg(6) + bg_ref[:, d:]).astype(BF16)

    k = seg(3)
    k_ref[...] = k.astype(BF16)
    for blk in range(rows // MOBA_BLOCK):
        r0 = blk * MOBA_BLOCK
        kmean_ref[0, blk:blk + 1, :] = jnp.mean(k[r0:r0 + MOBA_BLOCK, :], axis=0, keepdims=True)


def _proj_call(x2, batch, norm_g, w_in, b_gates, ln_g, ln_b, w_spatial, b_spatial):
    t, d = x2.shape
    rows = PROJ_ROWS
    seq = t // batch
    assert seq % rows == 0 and rows % MOBA_BLOCK == 0 and rows % GMLP_CHUNK == 0
    n_tiles = t // rows
    tiles_per_seq = seq // rows
    blocks_per_tile = rows // MOBA_BLOCK
    nb = seq // MOBA_BLOCK
    row_spec = pl.BlockSpec((rows, d), lambda i: (i, 0))
    const2 = lambda i: (0, 0)
    const3 = lambda i: (0, 0, 0)
    act = jax.ShapeDtypeStruct((t, d), BF16)
    qT_spec = pl.BlockSpec((None, ATT_HEADS, HEAD_DIM, rows),
                           lambda i: (i // tiles_per_seq, 0, 0, i % tiles_per_seq))
    vT_spec = pl.BlockSpec((None, ATT_HEADS, blocks_per_tile, V_ROWS, MOBA_BLOCK),
                           lambda i: (i // tiles_per_seq, 0, i % tiles_per_seq, 0, 0))
    return pl.pallas_call(
        _proj_kernel,
        grid=(n_tiles,),
        in_specs=[
            row_spec,
            pl.BlockSpec((1, d), const2),
            pl.BlockSpec(memory_space=pl.ANY),
            pl.BlockSpec((1, 2 * d), const2),
            pl.BlockSpec((1, d), const2),
            pl.BlockSpec((1, d), const2),
            pl.BlockSpec((GMLP_GROUPS, GMLP_CHUNK, GMLP_CHUNK), const3),
            pl.BlockSpec((GMLP_GROUPS, GMLP_CHUNK, 1), const3),
        ],
        out_specs=[row_spec, qT_spec, row_spec, vT_spec, row_spec,
                   pl.BlockSpec((1, blocks_per_tile, d), lambda i: (i, 0, 0))],
        out_shape=[act,
                   jax.ShapeDtypeStruct((batch, ATT_HEADS, HEAD_DIM, seq), BF16),
                   act,
                   jax.ShapeDtypeStruct((batch, ATT_HEADS, nb, V_ROWS, MOBA_BLOCK), BF16),
                   act,
                   jax.ShapeDtypeStruct((n_tiles, blocks_per_tile, d), F32)],
        scratch_shapes=[pltpu.VMEM((d, N_SEGMENTS * d), BF16),
                        pltpu.VMEM((2, d, W_STAGE_COLS), F32),
                        pltpu.SemaphoreType.DMA((2,)),
                        pltpu.VMEM((rows, d), BF16), pltpu.VMEM((rows, d), BF16),
                        pltpu.VMEM((rows, d), F32)],
        compiler_params=pltpu.CompilerParams(
            dimension_semantics=("arbitrary",), vmem_limit_bytes=VMEM_LIMIT_BYTES),
        name="proj_gmlp",
    )(x2, norm_g, w_in, b_gates, ln_g, ln_b, w_spatial, b_spatial)


def _t5_bucket_np(n):
    n = np.maximum(n, 0)
    max_exact = REL_BUCKETS // 2
    nf = np.maximum(n, max_exact).astype(np.float32)
    large = max_exact + (np.log(nf / max_exact) / math.log(REL_MAX_DIST / max_exact)
                         * (REL_BUCKETS - max_exact)).astype(np.int32)
    large = np.minimum(large, REL_BUCKETS - 1)
    return np.where(n < max_exact, n, large).astype(np.int32)


def _distance_buckets():
    dist = np.arange(N_BIAS_TILES * MOBA_BLOCK, dtype=np.int32)
    return np.tile(_t5_bucket_np(dist)[None, :], (TOKEN_SUBLANES, 1))


def _bias_kernel(relb_ref, bucket_ref, out_ref):
    k_idx = lax.broadcasted_iota(jnp.int32, (MOBA_BLOCK, MOBA_BLOCK), 0)
    q_idx = lax.broadcasted_iota(jnp.int32, (MOBA_BLOCK, MOBA_BLOCK), 1)
    bucket = bucket_ref[...]
    for h in range(ATT_HEADS):
        far = relb_ref[REL_BUCKETS - 1, h]
        by_dist = jnp.zeros(bucket.shape, F32)
        for b in range(REL_BUCKETS):
            by_dist = jnp.where(bucket == b, relb_ref[b, h], by_dist)
        by_dist = (by_dist - far) * LOG2E
        rows = jnp.broadcast_to(by_dist[:1], (MOBA_BLOCK, N_BIAS_TILES * MOBA_BLOCK))
        toeplitz = pltpu.roll(rows, 0, 1, stride=1, stride_axis=0)
        out_ref[h, BIAS_OWN] = jnp.where(q_idx >= k_idx, toeplitz[:, :MOBA_BLOCK], MASK_NEG)
        out_ref[h, BIAS_PREV] = toeplitz[:, MOBA_BLOCK:]


def _bias_call(rel_bias):
    return pl.pallas_call(
        _bias_kernel,
        in_specs=[pl.BlockSpec(memory_space=pltpu.SMEM),
                  pl.BlockSpec(memory_space=pltpu.VMEM)],
        out_specs=pl.BlockSpec(memory_space=pltpu.VMEM),
        out_shape=jax.ShapeDtypeStruct((ATT_HEADS, N_BIAS_TILES, MOBA_BLOCK, MOBA_BLOCK), F32),
        name="t5_bias_tiles",
    )(rel_bias, jnp.asarray(_distance_buckets()))


BIAS_OWN, BIAS_PREV = range(2)
N_BIAS_TILES = 2


def _attn_kernel(qT_ref, k_ref, vT_ref, kmh_ref, kml_ref, oh_ref, bias_ref, o_ref,
                 qp_scr, m_scr, acc_scr):
    i = pl.program_id(1)
    nh, nb = kmh_ref.shape[0], kmh_ref.shape[1]
    blk = MOBA_BLOCK
    hd = HEAD_DIM

    n_idx = lax.broadcasted_iota(jnp.int32, (nb, blk), 0)
    past = n_idx < i
    for h in range(nh):
        qT = qT_ref[h]
        gate = _dot(kmh_ref[h], qT) + _dot(kml_ref[h], qT)
        gate = jnp.where(past, gate, -jnp.inf)
        rank = jnp.zeros((nb, blk), F32)
        for m in range(nb):
            row = gate[m:m + 1, :]
            beats = (row > gate) | ((row == gate) & (m < n_idx))
            rank = rank + jnp.where(beats, 1.0, 0.0)
        keep = (past & (rank < MOBA_TOPK)) | (n_idx == i)
        sel = jnp.where(keep, 0.0, MASK_NEG)
        sel = jnp.concatenate([sel, jnp.zeros((hd - nb, blk), F32)], axis=0)
        qp_scr[h] = jnp.concatenate([qT, sel.astype(BF16)], axis=0)

    def scores(h, j):
        r0 = pl.multiple_of(j * blk, blk)
        kp = jnp.concatenate([k_ref[pl.ds(r0, blk), h * hd:(h + 1) * hd], oh_ref[j]], axis=1)
        return _dot(kp, qp_scr[h])

    def fold(blocks, bias_tiles, first):
        ss = []
        for h in range(nh):
            parts = []
            for j, tile in zip(blocks, bias_tiles):
                s = scores(h, j)
                parts.append((s if tile is None else s + bias_ref[h, tile]).astype(BF16))
            ss.append(parts)
        ps, alphas = [], []
        for h in range(nh):
            m_new = functools.reduce(
                jnp.maximum, [jnp.max(s, axis=0, keepdims=True) for s in ss[h]]).astype(F32)
            if not first:
                m_old = m_scr[h]
                m_new = jnp.maximum(m_old, m_new)
                alphas.append(jnp.exp2(m_old - m_new))
            m_scr[h] = m_new
            ps.append([jnp.exp2(s - m_new.astype(BF16)) for s in ss[h]])
        for h in range(nh):
            pv = functools.reduce(
                lambda a, b: a + b, [_dot(vT_ref[h, j], x) for j, x in zip(blocks, ps[h])])
            acc_scr[h] = pv if first else alphas[h] * acc_scr[h] + pv

    @pl.when(i == 0)
    def _():
        fold([i], [BIAS_OWN], first=True)

    @pl.when(i >= 1)
    def _():
        fold([i, i - 1], [BIAS_OWN, BIAS_PREV], first=True)

    n_far = i - 1

    def far_pair(p, carry):
        fold([2 * p, 2 * p + 1], [None, None], first=False)
        return carry

    lax.fori_loop(0, n_far // 2, far_pair, 0)

    @pl.when((n_far >= 1) & (n_far % 2 == 1))
    def _():
        fold([n_far - 1], [None], first=False)

    for h in range(nh):
        y = acc_scr[h, :hd, :] * (1.0 / acc_scr[h, hd:hd + 1, :])
        o_ref[:, h * hd:(h + 1) * hd] = y.T.astype(o_ref.dtype)


def _attn_call(qT, k2, vT, km_hi, km_lo, bias_tiles):
    b, h, nb, v_rows, blk = vT.shape
    hd = HEAD_DIM
    s = nb * blk
    onehot = np.zeros((nb, blk, LANES), np.float32)
    for j in range(nb):
        onehot[j, :, j] = 1.0
    once = pl.Buffered(1)
    return pl.pallas_call(
        _attn_kernel,
        grid=(b, nb),
        in_specs=[
            pl.BlockSpec((None, h, hd, blk), lambda bi, i: (bi, 0, 0, i)),
            pl.BlockSpec((s, h * hd), lambda bi, i: (bi, 0)),
            pl.BlockSpec((None, h, nb, v_rows, blk), lambda bi, i: (bi, 0, 0, 0, 0)),
            pl.BlockSpec((None, h, nb, hd), lambda bi, i: (bi, 0, 0, 0)),
            pl.BlockSpec((None, h, nb, hd), lambda bi, i: (bi, 0, 0, 0)),
            pl.BlockSpec((nb, blk, LANES), lambda bi, i: (0, 0, 0), pipeline_mode=once),
            pl.BlockSpec((h, N_BIAS_TILES, blk, blk), lambda bi, i: (0, 0, 0, 0),
                         pipeline_mode=once),
        ],
        out_specs=pl.BlockSpec((blk, h * hd), lambda bi, i: (bi * nb + i, 0)),
        out_shape=jax.ShapeDtypeStruct((b * s, h * hd), BF16),
        scratch_shapes=[pltpu.VMEM((h, 2 * hd, blk), BF16), pltpu.VMEM((h, 1, blk), F32),
                        pltpu.VMEM((h, v_rows, blk), F32)],
        compiler_params=pltpu.CompilerParams(
            dimension_semantics=("arbitrary", "arbitrary"),
            vmem_limit_bytes=VMEM_LIMIT_BYTES),
        name="moba_attention",
    )(qT, k2, vT, km_hi, km_lo, jnp.asarray(onehot, BF16), bias_tiles)


ROUTE_E1, ROUTE_E2, ROUTE_W1, ROUTE_W2, ROUTE_R1, ROUTE_R2 = range(6)


def _dot_nt(a, b):
    return lax.dot_general(a, b, (((1,), (1,)), ((), ())), preferred_element_type=F32)


def _store_token_major(ref, x):
    rows = x.shape[0]
    for s in range(TOKEN_SUBLANES):
        ref[pl.ds(s, rows, stride=TOKEN_SUBLANES), :] = x[:, s * LANES:(s + 1) * LANES]


def _load_token_major(ref, rows):
    return jnp.concatenate(
        [ref[pl.ds(s, rows, stride=TOKEN_SUBLANES), :] for s in range(TOKEN_SUBLANES)], axis=1)


def _merge_kernel(x_ref, ya_ref, gb_ref, yb_ref, wo_ref, ng_ref, wrh_ref, wrl_ref, br_ref,
                  h_ref, xn_ref, route_ref, route_t_ref, counts_ref, run_scr, wo_scr):
    @pl.when(pl.program_id(0) == 0)
    def _():
        run_scr[...] = jnp.zeros_like(run_scr)
        wo_scr[...] = wo_ref[...].astype(BF16)

    f = lambda r: r[...].astype(F32)
    mix = (f(ya_ref) + f(gb_ref) * f(yb_ref)).astype(BF16)
    h = x_ref[...] + _dot(mix, wo_scr[...])
    h_ref[...] = h
    xn = _rmsnorm(h, ng_ref[...])
    _store_token_major(xn_ref, xn)
    rows = xn.shape[0]

    x_hi = xn.astype(BF16)
    x_lo = (xn - x_hi.astype(F32)).astype(BF16)
    logits = (_dot_nt(wrh_ref[...], x_hi) + _dot_nt(wrh_ref[...], x_lo)
              + _dot_nt(wrl_ref[...], x_hi) + br_ref[...])
    unit = lax.broadcasted_iota(jnp.int32, logits.shape, 0).astype(F32)
    big = float(ROUTER_UNITS)
    neg_inf = -jnp.inf

    gl = jnp.where((unit >= GROUP_UNIT0) & (unit < GROUP_UNIT0 + N_GROUPS), logits, neg_inf)
    gmax = jnp.max(gl, axis=0, keepdims=True)
    g_w = 1.0 / jnp.sum(jnp.exp(gl - gmax), axis=0, keepdims=True)
    g_idx = jnp.min(jnp.where(gl == gmax, unit, big), axis=0, keepdims=True) - GROUP_UNIT0

    e0 = g_idx * EXPERTS_PER_GROUP
    el = jnp.where((unit >= e0) & (unit < e0 + EXPERTS_PER_GROUP), logits, neg_inf)
    m1 = jnp.max(el, axis=0, keepdims=True)
    i1 = jnp.min(jnp.where(el == m1, unit, big), axis=0, keepdims=True)
    el2 = jnp.where(unit == i1, neg_inf, el)
    m2 = jnp.max(el2, axis=0, keepdims=True)
    i2 = jnp.min(jnp.where(el2 == m2, unit, big), axis=0, keepdims=True)
    e2 = jnp.exp(m2 - m1)
    den = 1.0 + e2
    w1 = (1.0 / den) * g_w
    w2 = (e2 / den) * g_w

    hit1 = unit == i1
    hit2 = unit == i2
    onehot = jnp.where(hit1, 1.0, jnp.where(hit2, 1.0, 0.0))
    c_idx = lax.broadcasted_iota(jnp.int32, (rows, rows), 0)
    r_idx = lax.broadcasted_iota(jnp.int32, (rows, rows), 1)
    earlier = jnp.where(c_idx < r_idx, 1.0, 0.0).astype(BF16)
    prefix = run_scr[...] + _dot(onehot.astype(BF16), earlier)
    rank1 = jnp.sum(jnp.where(hit1, prefix, 0.0), axis=0, keepdims=True)
    rank2 = jnp.sum(jnp.where(hit2, prefix, 0.0), axis=0, keepdims=True)
    run_scr[...] = run_scr[...] + jnp.sum(onehot, axis=1, keepdims=True)
    counts_ref[...] = run_scr[...]

    route_t = jnp.concatenate(
        [i1, i2, w1, w2, rank1, rank2, jnp.zeros((ROUTER_LANES - 6, rows), F32)], axis=0)
    route_t_ref[0] = route_t[:ROUTE_ROWS]
    route_ref[...] = route_t.T


def _merge_call(x2, ya, gb, yb, w_out, norm_g, wr_hi, wr_lo, b_router):
    t, d = x2.shape
    rows = MERGE_ROWS
    assert t % rows == 0 and d == TOKEN_SUBLANES * LANES
    n_tiles = t // rows
    row_spec = pl.BlockSpec((rows, d), lambda i: (i, 0))
    const2 = lambda i: (0, 0)
    return pl.pallas_call(
        _merge_kernel,
        grid=(n_tiles,),
        in_specs=[row_spec, row_spec, row_spec, row_spec,
                  pl.BlockSpec((d, d), const2, pipeline_mode=pl.Buffered(1)),
                  pl.BlockSpec((1, d), const2),
                  pl.BlockSpec((ROUTER_UNITS, d), const2),
                  pl.BlockSpec((ROUTER_UNITS, d), const2),
                  pl.BlockSpec((ROUTER_UNITS, 1), const2)],
        out_specs=[row_spec,
                   pl.BlockSpec((rows * TOKEN_SUBLANES, LANES), lambda i: (i, 0)),
                   pl.BlockSpec((rows, ROUTER_LANES), lambda i: (i, 0)),
                   pl.BlockSpec((1, ROUTE_ROWS, rows), lambda i: (i, 0, 0)),
                   pl.BlockSpec((ROUTER_UNITS, 1), const2)],
        out_shape=[jax.ShapeDtypeStruct((t, d), F32),
                   jax.ShapeDtypeStruct((t * TOKEN_SUBLANES, LANES), F32),
                   jax.ShapeDtypeStruct((t, ROUTER_LANES), F32),
                   jax.ShapeDtypeStruct((n_tiles, ROUTE_ROWS, rows), F32),
                   jax.ShapeDtypeStruct((ROUTER_UNITS, 1), F32)],
        scratch_shapes=[pltpu.VMEM((ROUTER_UNITS, 1), F32), pltpu.VMEM((d, d), BF16)],
        compiler_params=pltpu.CompilerParams(
            dimension_semantics=("arbitrary",), vmem_limit_bytes=VMEM_LIMIT_BYTES),
        name="merge_outproj_router",
    )(x2, ya, gb, yb, w_out, norm_g, wr_hi, wr_lo, b_router)


def _token_rows(ref, token):
    return ref.at[pl.ds(pl.multiple_of(token * TOKEN_SUBLANES, TOKEN_SUBLANES), TOKEN_SUBLANES)]


def _dispatch_kernel(last_ref, nreal_ref, pos1_ref, pos2_ref, xn_ref, xs_hbm, zero_scr, sem):
    rows = pos1_ref.shape[2]
    tile = EXPERT_ROWS
    n_tiles = xs_hbm.shape[0] // (tile * TOKEN_SUBLANES)

    def zero_tile(j):
        start = pl.multiple_of(j * (tile * TOKEN_SUBLANES), tile * TOKEN_SUBLANES)
        return pltpu.make_async_copy(
            zero_scr, xs_hbm.at[pl.ds(start, tile * TOKEN_SUBLANES)], sem)

    @pl.when(pl.program_id(0) == 0)
    def _():
        zero_scr[...] = jnp.zeros_like(zero_scr)
        for e in range(N_EXPERTS):
            @pl.when(last_ref[e] >= 0)
            def _():
                zero_tile(last_ref[e]).start()

        def tail_start(j, carry):
            zero_tile(j).start()
            return carry

        lax.fori_loop(nreal_ref[0], n_tiles, tail_start, 0)

        for e in range(N_EXPERTS):
            @pl.when(last_ref[e] >= 0)
            def _():
                zero_tile(0).wait()

        def tail_wait(j, carry):
            zero_tile(0).wait()
            return carry

        lax.fori_loop(nreal_ref[0], n_tiles, tail_wait, 0)

    def issue(g, carry):
        for u in range(DMA_UNROLL):
            r = g * DMA_UNROLL + u
            src = _token_rows(xn_ref, r)
            pltpu.make_async_copy(src, _token_rows(xs_hbm, pos1_ref[0, 0, r]), sem).start(0)
            pltpu.make_async_copy(src, _token_rows(xs_hbm, pos2_ref[0, 0, r]), sem).start(1)
        return carry

    lax.fori_loop(0, rows // DMA_UNROLL, issue, 0)

    for _ in range(2):
        pltpu.make_async_copy(xn_ref, xs_hbm.at[pl.ds(0, rows * TOKEN_SUBLANES)], sem).wait()


def _dispatch_call(last_tile, n_real, pos1, pos2, xn, n_sorted_rows):
    n_steps, _, rows = pos1.shape
    smem_row = pl.BlockSpec((1, 1, rows), lambda i, lt, nr: (i, 0, 0), memory_space=pltpu.SMEM)
    return pl.pallas_call(
        _dispatch_kernel,
        grid_spec=pltpu.PrefetchScalarGridSpec(
            num_scalar_prefetch=2,
            grid=(n_steps,),
            in_specs=[smem_row, smem_row,
                      pl.BlockSpec((rows * TOKEN_SUBLANES, LANES), lambda i, lt, nr: (i, 0))],
            out_specs=pl.BlockSpec(memory_space=pl.ANY),
            scratch_shapes=[pltpu.VMEM((EXPERT_ROWS * TOKEN_SUBLANES, LANES), F32),
                            pltpu.SemaphoreType.DMA(())],
        ),
        out_shape=jax.ShapeDtypeStruct((n_sorted_rows * TOKEN_SUBLANES, LANES), F32),
        compiler_params=pltpu.CompilerParams(dimension_semantics=("arbitrary",)),
        name="moe_dispatch",
    )(last_tile, n_real, pos1, pos2, xn)


EXPERT_IN_SLOTS = 6
EXPERT_OUT_SLOTS = 4


def _expert_kernel(first_ref, end_ref, nreal_ref, xs_hbm, w1_ref, w3_ref, w2_ref, ys_hbm,
                   w1_scr, w3_scr, w2_scr, xbuf, ybuf, in_sems, out_sems):
    e = pl.program_id(0)
    tile_rows = EXPERT_ROWS * TOKEN_SUBLANES
    n_real = nreal_ref[0]
    n_tiles = xs_hbm.shape[0] // tile_rows

    def tile_of(ref, t):
        return ref.at[pl.ds(pl.multiple_of(t * tile_rows, tile_rows), tile_rows)]

    def in_copy(t):
        slot = t % EXPERT_IN_SLOTS
        return pltpu.make_async_copy(tile_of(xs_hbm, t), xbuf.at[slot], in_sems.at[slot])

    def out_copy(t):
        slot = t % EXPERT_OUT_SLOTS
        return pltpu.make_async_copy(ybuf.at[slot], tile_of(ys_hbm, t), out_sems.at[slot])

    @pl.when(e == 0)
    def _():
        for t in range(EXPERT_IN_SLOTS - 1):
            @pl.when(t < n_real)
            def _():
                in_copy(t).start()

    w1_scr[...] = w1_ref[...].astype(BF16)
    w3_scr[...] = w3_ref[...].astype(BF16)
    w2_scr[...] = w2_ref[...].astype(BF16)

    def tile(t, carry):
        ahead = t + EXPERT_IN_SLOTS - 1

        @pl.when(ahead < n_real)
        def _():
            in_copy(ahead).start()

        in_copy(t).wait()

        @pl.when(t >= EXPERT_OUT_SLOTS)
        def _():
            out_copy(t - EXPERT_OUT_SLOTS).wait()

        x = _load_token_major(xbuf.at[t % EXPERT_IN_SLOTS], EXPERT_ROWS).astype(BF16)
        a = _dot(x, w1_scr[...])
        b = _dot(x, w3_scr[...])
        hid = (a * _sigmoid(a)) * b
        _store_token_major(ybuf.at[t % EXPERT_OUT_SLOTS], _dot(hid.astype(BF16), w2_scr[...]))
        out_copy(t).start()
        return carry

    lax.fori_loop(first_ref[e], end_ref[e], tile, 0)

    @pl.when(e == pl.num_programs(0) - 1)
    def _():
        for back in range(EXPERT_OUT_SLOTS, 0, -1):
            @pl.when(n_real - back >= 0)
            def _():
                out_copy(n_real - back).wait()
        ybuf[0] = jnp.zeros(ybuf.shape[1:], F32)

        def tail_start(t, carry):
            pltpu.make_async_copy(ybuf.at[0], tile_of(ys_hbm, t), out_sems.at[0]).start()
            return carry

        def tail_wait(t, carry):
            pltpu.make_async_copy(ybuf.at[0], tile_of(ys_hbm, t), out_sems.at[0]).wait()
            return carry

        lax.fori_loop(n_real, n_tiles, tail_start, 0)
        lax.fori_loop(n_real, n_tiles, tail_wait, 0)


def _expert_call(first_tile, end_tile, n_real, xs, w1, w3, w2):
    n_experts, d, d_expert = w1.shape
    tile_rows = EXPERT_ROWS * TOKEN_SUBLANES
    per_expert = lambda e, f, n, nr: (e, 0, 0)
    return pl.pallas_call(
        _expert_kernel,
        grid_spec=pltpu.PrefetchScalarGridSpec(
            num_scalar_prefetch=3,
            grid=(n_experts,),
            in_specs=[pl.BlockSpec(memory_space=pl.ANY),
                      pl.BlockSpec((None, d, d_expert), per_expert),
                      pl.BlockSpec((None, d, d_expert), per_expert),
                      pl.BlockSpec((None, d_expert, d), per_expert)],
            out_specs=pl.BlockSpec(memory_space=pl.ANY),
            scratch_shapes=[pltpu.VMEM((d, d_expert), BF16), pltpu.VMEM((d, d_expert), BF16),
                            pltpu.VMEM((d_expert, d), BF16),
                            pltpu.VMEM((EXPERT_IN_SLOTS, tile_rows, LANES), F32),
                            pltpu.VMEM((EXPERT_OUT_SLOTS, tile_rows, LANES), F32),
                            pltpu.SemaphoreType.DMA((EXPERT_IN_SLOTS,)),
                            pltpu.SemaphoreType.DMA((EXPERT_OUT_SLOTS,))],
        ),
        out_shape=jax.ShapeDtypeStruct(xs.shape, F32),
        compiler_params=pltpu.CompilerParams(
            dimension_semantics=("arbitrary",), vmem_limit_bytes=VMEM_LIMIT_BYTES),
        name="moe_experts",
    )(first_tile, end_tile, n_real, xs, w1, w3, w2)


def _combine_kernel(p1_first, p2_first, p1_next, p2_next, ys_hbm, h_ref, route_ref, ng_ref,
                    out_ref, buf, sems):
    i = pl.program_id(0)
    n = pl.num_programs(0)
    rows = h_ref.shape[0]

    def issue(p1_ref, p2_ref, slot):
        def body(g, carry):
            for u in range(DMA_UNROLL):
                r = g * DMA_UNROLL + u
                for which, p_ref in ((0, p1_ref), (1, p2_ref)):
                    pltpu.make_async_copy(_token_rows(ys_hbm, p_ref[0, 0, r]),
                                          _token_rows(buf.at[2 * slot + which], r),
                                          sems.at[slot]).start(which)
            return carry
        lax.fori_loop(0, rows // DMA_UNROLL, body, 0)

    @pl.when(i == 0)
    def _():
        issue(p1_first, p2_first, 0)

    @pl.when(i + 1 < n)
    def _():
        issue(p1_next, p2_next, (i + 1) % 2)

    slot = i % 2
    for which in range(2):
        pltpu.make_async_copy(ys_hbm.at[pl.ds(0, rows * TOKEN_SUBLANES)],
                              buf.at[2 * slot + which], sems.at[slot]).wait()

    route = route_ref[...]
    w1 = route[:, ROUTE_W1:ROUTE_W1 + 1]
    w2 = route[:, ROUTE_W2:ROUTE_W2 + 1]
    y = (h_ref[...] + w1 * _load_token_major(buf.at[2 * slot], rows)
         + w2 * _load_token_major(buf.at[2 * slot + 1], rows))
    out_ref[...] = _rmsnorm(y, ng_ref[...])


def _combine_call(pos1, pos2, ys, h, route, norm_g):
    t, d = h.shape
    n_steps, _, rows = pos1.shape
    row_spec = pl.BlockSpec((rows, d), lambda i: (i, 0))
    first = pl.BlockSpec((1, 1, rows), lambda i: (0, 0, 0), memory_space=pltpu.SMEM)
    nxt = pl.BlockSpec((1, 1, rows), lambda i: (jnp.minimum(i + 1, n_steps - 1), 0, 0),
                       memory_space=pltpu.SMEM)
    return pl.pallas_call(
        _combine_kernel,
        grid=(n_steps,),
        in_specs=[first, first, nxt, nxt,
                  pl.BlockSpec(memory_space=pl.ANY),
                  row_spec,
                  pl.BlockSpec((rows, ROUTER_LANES), lambda i: (i, 0)),
                  pl.BlockSpec((1, d), lambda i: (0, 0))],
        out_specs=row_spec,
        out_shape=jax.ShapeDtypeStruct((t, d), F32),
        scratch_shapes=[pltpu.VMEM((4, rows * TOKEN_SUBLANES, LANES), F32),
                        pltpu.SemaphoreType.DMA((2,))],
        compiler_params=pltpu.CompilerParams(
            dimension_semantics=("arbitrary",), vmem_limit_bytes=VMEM_LIMIT_BYTES),
        name="moe_combine",
    )(pos1, pos2, pos1, pos2, ys, h, route, norm_g)


def _sparse_moe(xn, route, route_t, counts, h, w1, w3, w2, norm_g):
    t = h.shape[0]
    tile = EXPERT_ROWS
    n_tiles = (2 * t) // tile + N_EXPERTS
    expert = jnp.arange(N_EXPERTS, dtype=jnp.int32)
    counts = counts[:N_EXPERTS, 0].astype(jnp.int32)
    group_tiles = (counts + tile - 1) // tile
    end_tile = jnp.sum(jnp.where(expert[None, :] <= expert[:, None], group_tiles[None, :], 0), axis=1)
    first_tile = end_tile - group_tiles
    n_real = end_tile[-1:]
    last_tile = jnp.where(group_tiles > 0, end_tile - 1, -1)

    def positions(e_row, r_row):
        e = route_t[:, e_row, :].astype(jnp.int32)
        start = jnp.zeros_like(e)
        for k in range(N_EXPERTS):
            start = jnp.where(e == k, first_tile[k] * tile, start)
        return start + route_t[:, r_row, :].astype(jnp.int32)

    pos1 = positions(ROUTE_E1, ROUTE_R1)
    pos2 = positions(ROUTE_E2, ROUTE_R2)
    per_step = lambda pos, rows: pos.reshape(t // rows, 1, rows)
    xs = _dispatch_call(last_tile, n_real, per_step(pos1, DISPATCH_ROWS),
                        per_step(pos2, DISPATCH_ROWS), xn, n_tiles * tile)
    ys = _expert_call(first_tile, end_tile, n_real, xs, w1, w3, w2)
    return _combine_call(per_step(pos1, COMBINE_ROWS), per_step(pos2, COMBINE_ROWS), ys, h,
                         route, norm_g)


def _layer(h, norm_mix_g, w_in, b_gates, gmlp_ln_g, gmlp_ln_b, w_spatial, b_spatial, bias_tiles,
           w_out, norm_ffn_g, w_group_router, b_group_router, w_expert_router, b_expert_router,
           w1, w3, w2, norm_out_g):
    b, s, d = h.shape
    t = b * s
    nb = s // MOBA_BLOCK
    x2 = h.reshape(t, d)
    row = lambda v: v.reshape(1, -1)

    ya, qT, k, vT, gb, kmean = _proj_call(
        x2, b, row(norm_mix_g), w_in, row(b_gates), row(gmlp_ln_g), row(gmlp_ln_b),
        w_spatial, b_spatial[:, :, None])

    km = jnp.transpose(kmean.reshape(b, nb, ATT_HEADS, HEAD_DIM), (0, 2, 1, 3))
    km_hi = km.astype(BF16)
    km_lo = (km - km_hi.astype(F32)).astype(BF16)
    yb = _attn_call(qT, k, vT, km_hi, km_lo, bias_tiles)

    w_router = jnp.concatenate(
        [jnp.transpose(w_expert_router, (0, 2, 1)).reshape(N_EXPERTS, d), w_group_router.T,
         jnp.zeros((ROUTER_UNITS - N_EXPERTS - N_GROUPS, d), F32)], axis=0)
    b_router = jnp.concatenate(
        [b_expert_router.reshape(-1), b_group_router,
         jnp.zeros((ROUTER_UNITS - N_EXPERTS - N_GROUPS,), F32)]).reshape(ROUTER_UNITS, 1)
    wr_hi = w_router.astype(BF16)
    wr_lo = (w_router - wr_hi.astype(F32)).astype(BF16)
    h2, xn, route, route_t, counts = _merge_call(
        x2, ya, gb, yb, w_out, row(norm_ffn_g), wr_hi, wr_lo, b_router)

    out = _sparse_moe(xn, route, route_t, counts, h2, w1, w3, w2, row(norm_out_g))
    return out.reshape(b, s, d)


def kernel(x, norm_mix_g, w_in, b_gates, gmlp_ln_g, gmlp_ln_b, w_spatial, b_spatial, rel_bias, w_out, norm_ffn_g, w_group_router, b_group_router, w_expert_router, b_expert_router, w1, w3, w2, norm_final_g):
    depth = w_in.shape[0]
    assert depth == 1, "the final rmsnorm is fused into the last layer's combine kernel"
    bias_tiles = _bias_call(rel_bias)
    return _layer(x, norm_mix_g[0], w_in[0], b_gates[0], gmlp_ln_g[0], gmlp_ln_b[0], w_spatial[0],
                  b_spatial[0], bias_tiles, w_out[0], norm_ffn_g[0], w_group_router[0],
                  b_group_router[0], w_expert_router[0], b_expert_router[0], w1[0], w3[0], w2[0],
                  norm_final_g)
```

```python
import functools
import math

import numpy as np
import jax
import jax.numpy as jnp
from jax import lax
from jax.experimental import pallas as pl
from jax.experimental.pallas import tpu as pltpu

F32 = jnp.float32
BF16 = jnp.bfloat16

D_MODEL = 1024
NORM_EPS = 1e-6
GMLP_GROUPS = 8
GMLP_CHUNK = 128
ATT_HEADS = 8
HEAD_DIM = 128
MOBA_BLOCK = 256
MOBA_TOPK = 3
REL_BUCKETS = 32
REL_MAX_DIST = 128
N_GROUPS = 4
EXPERTS_PER_GROUP = 8
N_EXPERTS = N_GROUPS * EXPERTS_PER_GROUP
N_SEGMENTS = 7

LANES = 128
TOKEN_SUBLANES = 8
ROUTE_ROWS = 8
VMEM_LIMIT_BYTES = 56 * 1024 * 1024

SQRT_HALF = math.sqrt(0.5)
LOG2E = math.log2(math.e)
SCORE_SCALE2 = (HEAD_DIM ** -0.5) * LOG2E
MASK_NEG = -(2.0 ** 100)
BF16_SUBLANES = 16
V_ROWS = HEAD_DIM + BF16_SUBLANES
ROUTER_LANES = LANES
ROUTER_UNITS = -(-(N_EXPERTS + N_GROUPS) // BF16_SUBLANES) * BF16_SUBLANES
GROUP_UNIT0 = N_EXPERTS

PROJ_ROWS = 512
W_STAGE_ROWS = 64
MERGE_ROWS = 1024
EXPERT_ROWS = 256
COMBINE_ROWS = 256
DMA_UNROLL = 8


def _rmsnorm(x, g):
    return x * lax.rsqrt(jnp.mean(x * x, axis=-1, keepdims=True) + NORM_EPS) * g


def _gelu(a):
    return 0.5 * a * (1.0 + lax.erf(a * SQRT_HALF))


def _sigmoid(a):
    return 1.0 / (1.0 + jnp.exp(-a))


def _dot(a, b):
    return jnp.dot(a, b, preferred_element_type=F32)


def _stage_weight_bf16(w_hbm, w_scr, stage, sems):
    chunk = stage.shape[1]
    n_chunks = w_scr.shape[0] // chunk

    def copy(c, slot):
        return pltpu.make_async_copy(
            w_hbm.at[pl.ds(c * chunk, chunk), :], stage.at[slot], sems.at[slot])

    copy(0, 0).start()

    def body(c, carry):
        slot = c % 2

        @pl.when(c + 1 < n_chunks)
        def _():
            copy(c + 1, 1 - slot).start()

        copy(c, slot).wait()
        w_scr[pl.ds(pl.multiple_of(c * chunk, chunk), chunk), :] = stage[slot].astype(BF16)
        return carry

    lax.fori_loop(0, n_chunks, body, 0)


def _proj_kernel(x_ref, ng_ref, w_hbm, bg_ref, lng_ref, lnb_ref, ws_ref, bs_ref,
                 ya_ref, qT_ref, k_ref, vT_ref, gb_ref, kmean_ref,
                 w_ref, w_stage, w_sems, xn_scr, vln_scr, mix_scr):
    rows = x_ref.shape[0]
    d = D_MODEL

    @pl.when(pl.program_id(0) == 0)
    def _():
        _stage_weight_bf16(w_hbm, w_ref, w_stage, w_sems)

    xn_scr[...] = _rmsnorm(x_ref[...], ng_ref[...]).astype(BF16)

    def seg(i):
        return _dot(xn_scr[...], w_ref[:, i * d:(i + 1) * d])

    hd = HEAD_DIM

    v = _gelu(seg(1))
    mu = jnp.mean(v, axis=-1, keepdims=True)
    vc = v - mu
    var = jnp.mean(vc * vc, axis=-1, keepdims=True)
    vln_scr[...] = (vc * lax.rsqrt(var + NORM_EPS) * lng_ref[...] + lnb_ref[...]).astype(BF16)

    mix_scr[...] = _gelu(seg(0)) * _sigmoid(seg(5) + bg_ref[:, :d])

    q = seg(2) * SCORE_SCALE2
    for h in range(ATT_HEADS):
        qT_ref[h] = q[:, h * hd:(h + 1) * hd].T.astype(BF16)

    t_idx = lax.broadcasted_iota(jnp.int32, (GMLP_CHUNK, GMLP_CHUNK), 0)
    s_idx = lax.broadcasted_iota(jnp.int32, (GMLP_CHUNK, GMLP_CHUNK), 1)
    causal = t_idx >= s_idx
    gd = d // GMLP_GROUPS
    n_chunks = rows // GMLP_CHUNK
    for g in range(GMLP_GROUPS):
        ws = jnp.where(causal, ws_ref[g], 0.0).astype(BF16)
        bias = bs_ref[g]
        vg = jnp.concatenate(
            [vln_scr[c * GMLP_CHUNK:(c + 1) * GMLP_CHUNK, g * gd:(g + 1) * gd]
             for c in range(n_chunks)], axis=1)
        mixed = _dot(ws, vg)
        for c in range(n_chunks):
            blk_rows = slice(c * GMLP_CHUNK, (c + 1) * GMLP_CHUNK)
            blk_cols = slice(g * gd, (g + 1) * gd)
            ya_ref[blk_rows, blk_cols] = (
                mix_scr[blk_rows, blk_cols] * (mixed[:, c * gd:(c + 1) * gd] + bias)).astype(BF16)

    v = seg(4)
    for blk in range(rows // MOBA_BLOCK):
        r0 = blk * MOBA_BLOCK
        for h in range(ATT_HEADS):
            vT_ref[h, blk, :hd, :] = v[r0:r0 + MOBA_BLOCK, h * hd:(h + 1) * hd].T.astype(BF16)
            vT_ref[h, blk, hd:, :] = jnp.ones((V_ROWS - hd, MOBA_BLOCK), BF16)

    gb_ref[...] = _sigmoid(seg(6) + bg_ref[:, d:]).astype(BF16)

    k = seg(3)
    k_ref[...] = k.astype(BF16)
    for blk in range(rows // MOBA_BLOCK):
        r0 = blk * MOBA_BLOCK
        kmean_ref[0, blk:blk + 1, :] = jnp.mean(k[r0:r0 + MOBA_BLOCK, :], axis=0, keepdims=True)


def _proj_call(x2, batch, norm_g, w_in, b_gates, ln_g, ln_b, w_spatial, b_spatial):
    t, d = x2.shape
    rows = PROJ_ROWS
    seq = t // batch
    assert seq % rows == 0 and rows % MOBA_BLOCK == 0 and rows % GMLP_CHUNK == 0
    n_tiles = t // rows
    tiles_per_seq = seq // rows
    blocks_per_tile = rows // MOBA_BLOCK
    nb = seq // MOBA_BLOCK
    row_spec = pl.BlockSpec((rows, d), lambda i: (i, 0))
    const2 = lambda i: (0, 0)
    const3 = lambda i: (0, 0, 0)
    act = jax.ShapeDtypeStruct((t, d), BF16)
    qT_spec = pl.BlockSpec((None, ATT_HEADS, HEAD_DIM, rows),
                           lambda i: (i // tiles_per_seq, 0, 0, i % tiles_per_seq))
    vT_spec = pl.BlockSpec((None, ATT_HEADS, blocks_per_tile, V_ROWS, MOBA_BLOCK),
                           lambda i: (i // tiles_per_seq, 0, i % tiles_per_seq, 0, 0))
    return pl.pallas_call(
        _proj_kernel,
        grid=(n_tiles,),
        in_specs=[
            row_spec,
            pl.BlockSpec((1, d), const2),
            pl.BlockSpec(memory_space=pl.ANY),
            pl.BlockSpec((1, 2 * d), const2),
            pl.BlockSpec((1, d), const2),
            pl.BlockSpec((1, d), const2),
            pl.BlockSpec((GMLP_GROUPS, GMLP_CHUNK, GMLP_CHUNK), const3),
            pl.BlockSpec((GMLP_GROUPS, GMLP_CHUNK, 1), const3),
        ],
        out_specs=[row_spec, qT_spec, row_spec, vT_spec, row_spec,
                   pl.BlockSpec((1, blocks_per_tile, d), lambda i: (i, 0, 0))],
        out_shape=[act,
                   jax.ShapeDtypeStruct((batch, ATT_HEADS, HEAD_DIM, seq), BF16),
                   act,
                   jax.ShapeDtypeStruct((batch, ATT_HEADS, nb, V_ROWS, MOBA_BLOCK), BF16),
                   act,
                   jax.ShapeDtypeStruct((n_tiles, blocks_per_tile, d), F32)],
        scratch_shapes=[pltpu.VMEM((d, N_SEGMENTS * d), BF16),
                        pltpu.VMEM((2, W_STAGE_ROWS, N_SEGMENTS * d), F32),
                        pltpu.SemaphoreType.DMA((2,)),
                        pltpu.VMEM((rows, d), BF16), pltpu.VMEM((rows, d), BF16),
                        pltpu.VMEM((rows, d), F32)],
        compiler_params=pltpu.CompilerParams(
            dimension_semantics=("arbitrary",), vmem_limit_bytes=VMEM_LIMIT_BYTES),
        name="proj_gmlp",
    )(x2, norm_g, w_in, b_gates, ln_g, ln_b, w_spatial, b_spatial)


def _t5_bucket_np(n):
    n = np.maximum(n, 0)
    max_exact = REL_BUCKETS // 2
    nf = np.maximum(n, max_exact).astype(np.float32)
    large = max_exact + (np.log(nf / max_exact) / math.log(REL_MAX_DIST / max_exact)
                         * (REL_BUCKETS - max_exact)).astype(np.int32)
    large = np.minimum(large, REL_BUCKETS - 1)
    return np.where(n < max_exact, n, large).astype(np.int32)


def _distance_buckets():
    dist = np.arange(N_BIAS_TILES * MOBA_BLOCK, dtype=np.int32)
    return np.tile(_t5_bucket_np(dist)[None, :], (TOKEN_SUBLANES, 1))


def _bias_kernel(relb_ref, bucket_ref, out_ref):
    k_idx = lax.broadcasted_iota(jnp.int32, (MOBA_BLOCK, MOBA_BLOCK), 0)
    q_idx = lax.broadcasted_iota(jnp.int32, (MOBA_BLOCK, MOBA_BLOCK), 1)
    bucket = bucket_ref[...]
    for h in range(ATT_HEADS):
        far = relb_ref[REL_BUCKETS - 1, h]
        by_dist = jnp.zeros(bucket.shape, F32)
        for b in range(REL_BUCKETS):
            by_dist = jnp.where(bucket == b, relb_ref[b, h], by_dist)
        by_dist = (by_dist - far) * LOG2E
        rows = jnp.broadcast_to(by_dist[:1], (MOBA_BLOCK, N_BIAS_TILES * MOBA_BLOCK))
        toeplitz = pltpu.roll(rows, 0, 1, stride=1, stride_axis=0)
        out_ref[h, BIAS_OWN] = jnp.where(q_idx >= k_idx, toeplitz[:, :MOBA_BLOCK], MASK_NEG)
        out_ref[h, BIAS_PREV] = toeplitz[:, MOBA_BLOCK:]


def _bias_call(rel_bias):
    return pl.pallas_call(
        _bias_kernel,
        in_specs=[pl.BlockSpec(memory_space=pltpu.SMEM),
                  pl.BlockSpec(memory_space=pltpu.VMEM)],
        out_specs=pl.BlockSpec(memory_space=pltpu.VMEM),
        out_shape=jax.ShapeDtypeStruct((ATT_HEADS, N_BIAS_TILES, MOBA_BLOCK, MOBA_BLOCK), F32),
        name="t5_bias_tiles",
    )(rel_bias, jnp.asarray(_distance_buckets()))


BIAS_OWN, BIAS_PREV = range(2)
N_BIAS_TILES = 2


def _attn_kernel(qT_ref, k_ref, vT_ref, kmh_ref, kml_ref, oh_ref, bias_ref, o_ref,
                 qp_scr, m_scr, acc_scr):
    i = pl.program_id(1)
    nh, nb = kmh_ref.shape[0], kmh_ref.shape[1]
    blk = MOBA_BLOCK
    hd = HEAD_DIM

    n_idx = lax.broadcasted_iota(jnp.int32, (nb, blk), 0)
    past = n_idx < i
    for h in range(nh):
        qT = qT_ref[h]
        gate = _dot(kmh_ref[h], qT) + _dot(kml_ref[h], qT)
        gate = jnp.where(past, gate, -jnp.inf)
        rank = jnp.zeros((nb, blk), F32)
        for m in range(nb):
            row = gate[m:m + 1, :]
            beats = (row > gate) | ((row == gate) & (m < n_idx))
            rank = rank + jnp.where(beats, 1.0, 0.0)
        keep = (past & (rank < MOBA_TOPK)) | (n_idx == i)
        sel = jnp.where(keep, 0.0, MASK_NEG)
        sel = jnp.concatenate([sel, jnp.zeros((hd - nb, blk), F32)], axis=0)
        qp_scr[h] = jnp.concatenate([qT, sel.astype(BF16)], axis=0)

    def scores(h, j):
        r0 = pl.multiple_of(j * blk, blk)
        kp = jnp.concatenate([k_ref[pl.ds(r0, blk), h * hd:(h + 1) * hd], oh_ref[j]], axis=1)
        return _dot(kp, qp_scr[h])

    def fold(blocks, bias_tiles, first):
        ss = []
        for h in range(nh):
            parts = []
            for j, tile in zip(blocks, bias_tiles):
                s = scores(h, j)
                parts.append((s if tile is None else s + bias_ref[h, tile]).astype(BF16))
            ss.append(parts)
        ps, alphas = [], []
        for h in range(nh):
            m_new = functools.reduce(
                jnp.maximum, [jnp.max(s, axis=0, keepdims=True) for s in ss[h]]).astype(F32)
            if not first:
                m_old = m_scr[h]
                m_new = jnp.maximum(m_old, m_new)
                alphas.append(jnp.exp2(m_old - m_new))
            m_scr[h] = m_new
            ps.append([jnp.exp2(s - m_new.astype(BF16)) for s in ss[h]])
        for h in range(nh):
            pv = functools.reduce(
                lambda a, b: a + b, [_dot(vT_ref[h, j], x) for j, x in zip(blocks, ps[h])])
            acc_scr[h] = pv if first else alphas[h] * acc_scr[h] + pv

    @pl.when(i == 0)
    def _():
        fold([i], [BIAS_OWN], first=True)

    @pl.when(i >= 1)
    def _():
        fold([i, i - 1], [BIAS_OWN, BIAS_PREV], first=True)

    n_far = i - 1

    def far_pair(p, carry):
        fold([2 * p, 2 * p + 1], [None, None], first=False)
        return carry

    lax.fori_loop(0, n_far // 2, far_pair, 0)

    @pl.when((n_far >= 1) & (n_far % 2 == 1))
    def _():
        fold([n_far - 1], [None], first=False)

    for h in range(nh):
        y = acc_scr[h, :hd, :] * (1.0 / acc_scr[h, hd:hd + 1, :])
        o_ref[:, h * hd:(h + 1) * hd] = y.T.astype(o_ref.dtype)


def _attn_call(qT, k2, vT, km_hi, km_lo, bias_tiles):
    b, h, nb, v_rows, blk = vT.shape
    hd = HEAD_DIM
    s = nb * blk
    onehot = np.zeros((nb, blk, LANES), np.float32)
    for j in range(nb):
        onehot[j, :, j] = 1.0
    once = pl.Buffered(1)
    return pl.pallas_call(
        _attn_kernel,
        grid=(b, nb),
        in_specs=[
            pl.BlockSpec((None, h, hd, blk), lambda bi, i: (bi, 0, 0, i)),
            pl.BlockSpec((s, h * hd), lambda bi, i: (bi, 0)),
            pl.BlockSpec((None, h, nb, v_rows, blk), lambda bi, i: (bi, 0, 0, 0, 0)),
            pl.BlockSpec((None, h, nb, hd), lambda bi, i: (bi, 0, 0, 0)),
            pl.BlockSpec((None, h, nb, hd), lambda bi, i: (bi, 0, 0, 0)),
            pl.BlockSpec((nb, blk, LANES), lambda bi, i: (0, 0, 0), pipeline_mode=once),
            pl.BlockSpec((h, N_BIAS_TILES, blk, blk), lambda bi, i: (0, 0, 0, 0),
                         pipeline_mode=once),
        ],
        out_specs=pl.BlockSpec((blk, h * hd), lambda bi, i: (bi * nb + i, 0)),
        out_shape=jax.ShapeDtypeStruct((b * s, h * hd), BF16),
        scratch_shapes=[pltpu.VMEM((h, 2 * hd, blk), BF16), pltpu.VMEM((h, 1, blk), F32),
                        pltpu.VMEM((h, v_rows, blk), F32)],
        compiler_params=pltpu.CompilerParams(
            dimension_semantics=("arbitrary", "arbitrary"),
            vmem_limit_bytes=VMEM_LIMIT_BYTES),
        name="moba_attention",
    )(qT, k2, vT, km_hi, km_lo, jnp.asarray(onehot, BF16), bias_tiles)


ROUTE_E1, ROUTE_E2, ROUTE_W1, ROUTE_W2, ROUTE_R1, ROUTE_R2 = range(6)


def _dot_nt(a, b):
    return lax.dot_general(a, b, (((1,), (1,)), ((), ())), preferred_element_type=F32)


def _store_token_major(ref, x):
    rows = x.shape[0]
    for s in range(TOKEN_SUBLANES):
        ref[pl.ds(s, rows, stride=TOKEN_SUBLANES), :] = x[:, s * LANES:(s + 1) * LANES]


def _load_token_major(ref, rows):
    return jnp.concatenate(
        [ref[pl.ds(s, rows, stride=TOKEN_SUBLANES), :] for s in range(TOKEN_SUBLANES)], axis=1)


def _merge_kernel(x_ref, ya_ref, gb_ref, yb_ref, wo_ref, ng_ref, wrh_ref, wrl_ref, br_ref,
                  h_ref, xn_ref, route_ref, route_t_ref, counts_ref, run_scr, wo_scr):
    @pl.when(pl.program_id(0) == 0)
    def _():
        run_scr[...] = jnp.zeros_like(run_scr)
        wo_scr[...] = wo_ref[...].astype(BF16)

    f = lambda r: r[...].astype(F32)
    mix = (f(ya_ref) + f(gb_ref) * f(yb_ref)).astype(BF16)
    h = x_ref[...] + _dot(mix, wo_scr[...])
    h_ref[...] = h
    xn = _rmsnorm(h, ng_ref[...])
    _store_token_major(xn_ref, xn)
    rows = xn.shape[0]

    x_hi = xn.astype(BF16)
    x_lo = (xn - x_hi.astype(F32)).astype(BF16)
    logits = (_dot_nt(wrh_ref[...], x_hi) + _dot_nt(wrh_ref[...], x_lo)
              + _dot_nt(wrl_ref[...], x_hi) + br_ref[...])
    unit = lax.broadcasted_iota(jnp.int32, logits.shape, 0).astype(F32)
    big = float(ROUTER_UNITS)
    neg_inf = -jnp.inf

    gl = jnp.where((unit >= GROUP_UNIT0) & (unit < GROUP_UNIT0 + N_GROUPS), logits, neg_inf)
    gmax = jnp.max(gl, axis=0, keepdims=True)
    g_w = 1.0 / jnp.sum(jnp.exp(gl - gmax), axis=0, keepdims=True)
    g_idx = jnp.min(jnp.where(gl == gmax, unit, big), axis=0, keepdims=True) - GROUP_UNIT0

    e0 = g_idx * EXPERTS_PER_GROUP
    el = jnp.where((unit >= e0) & (unit < e0 + EXPERTS_PER_GROUP), logits, neg_inf)
    m1 = jnp.max(el, axis=0, keepdims=True)
    i1 = jnp.min(jnp.where(el == m1, unit, big), axis=0, keepdims=True)
    el2 = jnp.where(unit == i1, neg_inf, el)
    m2 = jnp.max(el2, axis=0, keepdims=True)
    i2 = jnp.min(jnp.where(el2 == m2, unit, big), axis=0, keepdims=True)
    e2 = jnp.exp(m2 - m1)
    den = 1.0 + e2
    w1 = (1.0 / den) * g_w
    w2 = (e2 / den) * g_w

    hit1 = unit == i1
    hit2 = unit == i2
    onehot = jnp.where(hit1, 1.0, jnp.where(hit2, 1.0, 0.0))
    c_idx = lax.broadcasted_iota(jnp.int32, (rows, rows), 0)
    r_idx = lax.broadcasted_iota(jnp.int32, (rows, rows), 1)
    earlier = jnp.where(c_idx < r_idx, 1.0, 0.0).astype(BF16)
    prefix = run_scr[...] + _dot(onehot.astype(BF16), earlier)
    rank1 = jnp.sum(jnp.where(hit1, prefix, 0.0), axis=0, keepdims=True)
    rank2 = jnp.sum(jnp.where(hit2, prefix, 0.0), axis=0, keepdims=True)
    run_scr[...] = run_scr[...] + jnp.sum(onehot, axis=1, keepdims=True)
    counts_ref[...] = run_scr[...]

    route_t = jnp.concatenate(
        [i1, i2, w1, w2, rank1, rank2, jnp.zeros((ROUTER_LANES - 6, rows), F32)], axis=0)
    route_t_ref[0] = route_t[:ROUTE_ROWS]
    route_ref[...] = route_t.T


def _merge_call(x2, ya, gb, yb, w_out, norm_g, wr_hi, wr_lo, b_router):
    t, d = x2.shape
    rows = MERGE_ROWS
    assert t % rows == 0 and d == TOKEN_SUBLANES * LANES
    n_tiles = t // rows
    row_spec = pl.BlockSpec((rows, d), lambda i: (i, 0))
    const2 = lambda i: (0, 0)
    return pl.pallas_call(
        _merge_kernel,
        grid=(n_tiles,),
        in_specs=[row_spec, row_spec, row_spec, row_spec,
                  pl.BlockSpec((d, d), const2, pipeline_mode=pl.Buffered(1)),
                  pl.BlockSpec((1, d), const2),
                  pl.BlockSpec((ROUTER_UNITS, d), const2),
                  pl.BlockSpec((ROUTER_UNITS, d), const2),
                  pl.BlockSpec((ROUTER_UNITS, 1), const2)],
        out_specs=[row_spec,
                   pl.BlockSpec((rows * TOKEN_SUBLANES, LANES), lambda i: (i, 0)),
                   pl.BlockSpec((rows, ROUTER_LANES), lambda i: (i, 0)),
                   pl.BlockSpec((1, ROUTE_ROWS, rows), lambda i: (i, 0, 0)),
                   pl.BlockSpec((ROUTER_UNITS, 1), const2)],
        out_shape=[jax.ShapeDtypeStruct((t, d), F32),
                   jax.ShapeDtypeStruct((t * TOKEN_SUBLANES, LANES), F32),
                   jax.ShapeDtypeStruct((t, ROUTER_LANES), F32),
                   jax.ShapeDtypeStruct((n_tiles, ROUTE_ROWS, rows), F32),
                   jax.ShapeDtypeStruct((ROUTER_UNITS, 1), F32)],
        scratch_shapes=[pltpu.VMEM((ROUTER_UNITS, 1), F32), pltpu.VMEM((d, d), BF16)],
        compiler_params=pltpu.CompilerParams(
            dimension_semantics=("arbitrary",), vmem_limit_bytes=VMEM_LIMIT_BYTES),
        name="merge_outproj_router",
    )(x2, ya, gb, yb, w_out, norm_g, wr_hi, wr_lo, b_router)


def _token_rows(ref, token):
    return ref.at[pl.ds(pl.multiple_of(token * TOKEN_SUBLANES, TOKEN_SUBLANES), TOKEN_SUBLANES)]


def _dispatch_kernel(last_ref, nreal_ref, start_ref, off_ref, cnt_ref, lp1_ref, lp2_ref, xn_ref,
                     xs_hbm, zero_scr, stage, sems):
    rows = lp1_ref.shape[2]
    tile = EXPERT_ROWS
    n_tiles = xs_hbm.shape[0] // (tile * TOKEN_SUBLANES)
    j = pl.program_id(0)
    slot = j % 2
    sem = sems.at[0]

    def wait_step(s):
        for _ in range(2):
            pltpu.make_async_copy(
                xn_ref, xs_hbm.at[pl.ds(0, rows * TOKEN_SUBLANES)], sems.at[s]).wait()

    def zero_tile(j):
        start = pl.multiple_of(j * (tile * TOKEN_SUBLANES), tile * TOKEN_SUBLANES)
        return pltpu.make_async_copy(
            zero_scr, xs_hbm.at[pl.ds(start, tile * TOKEN_SUBLANES)], sem)

    @pl.when(pl.program_id(0) == 0)
    def _():
        zero_scr[...] = jnp.zeros_like(zero_scr)
        for e in range(N_EXPERTS):
            @pl.when(last_ref[e] >= 0)
            def _():
                zero_tile(last_ref[e]).start()

        def tail_start(j, carry):
            zero_tile(j).start()
            return carry

        lax.fori_loop(nreal_ref[0], n_tiles, tail_start, 0)

        for e in range(N_EXPERTS):
            @pl.when(last_ref[e] >= 0)
            def _():
                zero_tile(0).wait()

        def tail_wait(j, carry):
            zero_tile(0).wait()
            return carry

        lax.fori_loop(nreal_ref[0], n_tiles, tail_wait, 0)

    @pl.when(j >= 2)
    def _():
        wait_step(slot)

    buf = stage.at[slot]

    def place(g, carry):
        for u in range(DMA_UNROLL):
            r = g * DMA_UNROLL + u
            tok = _token_rows(xn_ref, r)[...]
            _token_rows(buf, lp1_ref[0, 0, r])[...] = tok
            _token_rows(buf, lp2_ref[0, 0, r])[...] = tok
        return carry

    lax.fori_loop(0, rows // DMA_UNROLL, place, 0)

    def run(e, carry):
        idx = j * N_EXPERTS + e
        n = cnt_ref[idx]
        src0 = off_ref[idx]
        dst0 = start_ref[idx]
        for b in reversed(range(rows.bit_length())):
            size = 1 << b
            done = n & ~(2 * size - 1)

            @pl.when((n & size) != 0)
            def _():
                src = buf.at[pl.ds(pl.multiple_of((src0 + done) * TOKEN_SUBLANES, TOKEN_SUBLANES),
                                   size * TOKEN_SUBLANES)]
                dst = xs_hbm.at[pl.ds(pl.multiple_of((dst0 + done) * TOKEN_SUBLANES,
                                                     TOKEN_SUBLANES), size * TOKEN_SUBLANES)]
                pltpu.make_async_copy(src, dst, sems.at[slot]).start()
        return carry

    lax.fori_loop(0, N_EXPERTS, run, 0)

    @pl.when(j == pl.num_programs(0) - 1)
    def _():
        wait_step(slot)

        @pl.when(j >= 1)
        def _():
            wait_step(1 - slot)


def _dispatch_call(last_tile, n_real, run_start, run_off, run_cnt, lpos1, lpos2, xn,
                   n_sorted_rows):
    n_steps, _, rows = lpos1.shape
    smem_row = pl.BlockSpec((1, 1, rows), lambda i, *_: (i, 0, 0), memory_space=pltpu.SMEM)
    return pl.pallas_call(
        _dispatch_kernel,
        grid_spec=pltpu.PrefetchScalarGridSpec(
            num_scalar_prefetch=5,
            grid=(n_steps,),
            in_specs=[smem_row, smem_row,
                      pl.BlockSpec((rows * TOKEN_SUBLANES, LANES), lambda i, *_: (i, 0))],
            out_specs=pl.BlockSpec(memory_space=pl.ANY),
            scratch_shapes=[pltpu.VMEM((EXPERT_ROWS * TOKEN_SUBLANES, LANES), F32),
                            pltpu.VMEM((2, 2 * rows * TOKEN_SUBLANES, LANES), F32),
                            pltpu.SemaphoreType.DMA((2,))],
        ),
        out_shape=jax.ShapeDtypeStruct((n_sorted_rows * TOKEN_SUBLANES, LANES), F32),
        compiler_params=pltpu.CompilerParams(
            dimension_semantics=("arbitrary",), vmem_limit_bytes=VMEM_LIMIT_BYTES),
        name="moe_dispatch",
    )(last_tile, n_real, run_start, run_off, run_cnt, lpos1, lpos2, xn)


EXPERT_IN_SLOTS = 6
EXPERT_OUT_SLOTS = 4


def _expert_kernel(first_ref, end_ref, nreal_ref, xs_hbm, w1_ref, w3_ref, w2_ref, ys_hbm,
                   w1_scr, w3_scr, w2_scr, xbuf, ybuf, in_sems, out_sems):
    e = pl.program_id(0)
    tile_rows = EXPERT_ROWS * TOKEN_SUBLANES
    n_real = nreal_ref[0]
    n_tiles = xs_hbm.shape[0] // tile_rows

    def tile_of(ref, t):
        return ref.at[pl.ds(pl.multiple_of(t * tile_rows, tile_rows), tile_rows)]

    def in_copy(t):
        slot = t % EXPERT_IN_SLOTS
        return pltpu.make_async_copy(tile_of(xs_hbm, t), xbuf.at[slot], in_sems.at[slot])

    def out_copy(t):
        slot = t % EXPERT_OUT_SLOTS
        return pltpu.make_async_copy(ybuf.at[slot], tile_of(ys_hbm, t), out_sems.at[slot])

    @pl.when(e == 0)
    def _():
        for t in range(EXPERT_IN_SLOTS - 1):
            @pl.when(t < n_real)
            def _():
                in_copy(t).start()

    w1_scr[...] = w1_ref[...].astype(BF16)
    w3_scr[...] = w3_ref[...].astype(BF16)
    w2_scr[...] = w2_ref[...].astype(BF16)

    def tile(t, carry):
        ahead = t + EXPERT_IN_SLOTS - 1

        @pl.when(ahead < n_real)
        def _():
            in_copy(ahead).start()

        in_copy(t).wait()

        @pl.when(t >= EXPERT_OUT_SLOTS)
        def _():
            out_copy(t - EXPERT_OUT_SLOTS).wait()

        x = _load_token_major(xbuf.at[t % EXPERT_IN_SLOTS], EXPERT_ROWS).astype(BF16)
        a = _dot(x, w1_scr[...])
        b = _dot(x, w3_scr[...])
        hid = (a * _sigmoid(a)) * b
        _store_token_major(ybuf.at[t % EXPERT_OUT_SLOTS], _dot(hid.astype(BF16), w2_scr[...]))
        out_copy(t).start()
        return carry

    lax.fori_loop(first_ref[e], end_ref[e], tile, 0)

    @pl.when(e == pl.num_programs(0) - 1)
    def _():
        for back in range(EXPERT_OUT_SLOTS, 0, -1):
            @pl.when(n_real - back >= 0)
            def _():
                out_copy(n_real - back).wait()
        ybuf[0] = jnp.zeros(ybuf.shape[1:], F32)

        def tail_start(t, carry):
            pltpu.make_async_copy(ybuf.at[0], tile_of(ys_hbm, t), out_sems.at[0]).start()
            return carry

        def tail_wait(t, carry):
            pltpu.make_async_copy(ybuf.at[0], tile_of(ys_hbm, t), out_sems.at[0]).wait()
            return carry

        lax.fori_loop(n_real, n_tiles, tail_start, 0)
        lax.fori_loop(n_real, n_tiles, tail_wait, 0)


def _expert_call(first_tile, end_tile, n_real, xs, w1, w3, w2):
    n_experts, d, d_expert = w1.shape
    tile_rows = EXPERT_ROWS * TOKEN_SUBLANES
    per_expert = lambda e, f, n, nr: (e, 0, 0)
    return pl.pallas_call(
        _expert_kernel,
        grid_spec=pltpu.PrefetchScalarGridSpec(
            num_scalar_prefetch=3,
            grid=(n_experts,),
            in_specs=[pl.BlockSpec(memory_space=pl.ANY),
                      pl.BlockSpec((None, d, d_expert), per_expert),
                      pl.BlockSpec((None, d, d_expert), per_expert),
                      pl.BlockSpec((None, d_expert, d), per_expert)],
            out_specs=pl.BlockSpec(memory_space=pl.ANY),
            scratch_shapes=[pltpu.VMEM((d, d_expert), BF16), pltpu.VMEM((d, d_expert), BF16),
                            pltpu.VMEM((d_expert, d), BF16),
                            pltpu.VMEM((EXPERT_IN_SLOTS, tile_rows, LANES), F32),
                            pltpu.VMEM((EXPERT_OUT_SLOTS, tile_rows, LANES), F32),
                            pltpu.SemaphoreType.DMA((EXPERT_IN_SLOTS,)),
                            pltpu.SemaphoreType.DMA((EXPERT_OUT_SLOTS,))],
        ),
        out_shape=jax.ShapeDtypeStruct(xs.shape, F32),
        compiler_params=pltpu.CompilerParams(
            dimension_semantics=("arbitrary",), vmem_limit_bytes=VMEM_LIMIT_BYTES),
        name="moe_experts",
    )(first_tile, end_tile, n_real, xs, w1, w3, w2)


def _combine_kernel(p1_first, p2_first, p1_next, p2_next, ys_hbm, h_ref, route_ref, ng_ref,
                    out_ref, buf, sems):
    i = pl.program_id(0)
    n = pl.num_programs(0)
    rows = h_ref.shape[0]

    def issue(p1_ref, p2_ref, slot):
        def body(g, carry):
            for u in range(DMA_UNROLL):
                r = g * DMA_UNROLL + u
                for which, p_ref in ((0, p1_ref), (1, p2_ref)):
                    pltpu.make_async_copy(_token_rows(ys_hbm, p_ref[0, 0, r]),
                                          _token_rows(buf.at[2 * slot + which], r),
                                          sems.at[slot]).start(which)
            return carry
        lax.fori_loop(0, rows // DMA_UNROLL, body, 0)

    @pl.when(i == 0)
    def _():
        issue(p1_first, p2_first, 0)

    @pl.when(i + 1 < n)
    def _():
        issue(p1_next, p2_next, (i + 1) % 2)

    slot = i % 2
    for which in range(2):
        pltpu.make_async_copy(ys_hbm.at[pl.ds(0, rows * TOKEN_SUBLANES)],
                              buf.at[2 * slot + which], sems.at[slot]).wait()

    route = route_ref[...]
    w1 = route[:, ROUTE_W1:ROUTE_W1 + 1]
    w2 = route[:, ROUTE_W2:ROUTE_W2 + 1]
    y = (h_ref[...] + w1 * _load_token_major(buf.at[2 * slot], rows)
         + w2 * _load_token_major(buf.at[2 * slot + 1], rows))
    out_ref[...] = _rmsnorm(y, ng_ref[...])


def _combine_call(pos1, pos2, ys, h, route, norm_g):
    t, d = h.shape
    n_steps, _, rows = pos1.shape
    row_spec = pl.BlockSpec((rows, d), lambda i: (i, 0))
    first = pl.BlockSpec((1, 1, rows), lambda i: (0, 0, 0), memory_space=pltpu.SMEM)
    nxt = pl.BlockSpec((1, 1, rows), lambda i: (jnp.minimum(i + 1, n_steps - 1), 0, 0),
                       memory_space=pltpu.SMEM)
    return pl.pallas_call(
        _combine_kernel,
        grid=(n_steps,),
        in_specs=[first, first, nxt, nxt,
                  pl.BlockSpec(memory_space=pl.ANY),
                  row_spec,
                  pl.BlockSpec((rows, ROUTER_LANES), lambda i: (i, 0)),
                  pl.BlockSpec((1, d), lambda i: (0, 0))],
        out_specs=row_spec,
        out_shape=jax.ShapeDtypeStruct((t, d), F32),
        scratch_shapes=[pltpu.VMEM((4, rows * TOKEN_SUBLANES, LANES), F32),
                        pltpu.SemaphoreType.DMA((2,))],
        compiler_params=pltpu.CompilerParams(
            dimension_semantics=("arbitrary",), vmem_limit_bytes=VMEM_LIMIT_BYTES),
        name="moe_combine",
    )(pos1, pos2, pos1, pos2, ys, h, route, norm_g)


def _sparse_moe(xn, route, route_t, counts, h, w1, w3, w2, norm_g):
    t = h.shape[0]
    tile = EXPERT_ROWS
    n_tiles = (2 * t) // tile + N_EXPERTS
    expert = jnp.arange(N_EXPERTS, dtype=jnp.int32)
    counts = counts[:N_EXPERTS, 0].astype(jnp.int32)
    group_tiles = (counts + tile - 1) // tile
    end_tile = jnp.sum(jnp.where(expert[None, :] <= expert[:, None], group_tiles[None, :], 0), axis=1)
    first_tile = end_tile - group_tiles
    n_real = end_tile[-1:]
    last_tile = jnp.where(group_tiles > 0, end_tile - 1, -1)

    def positions(e_row, r_row):
        e = route_t[:, e_row, :].astype(jnp.int32)
        start = jnp.zeros_like(e)
        for k in range(N_EXPERTS):
            start = jnp.where(e == k, first_tile[k] * tile, start)
        return start + route_t[:, r_row, :].astype(jnp.int32)

    pos1 = positions(ROUTE_E1, ROUTE_R1)
    pos2 = positions(ROUTE_E2, ROUTE_R2)
    per_step = lambda pos, rows: pos.reshape(t // rows, 1, rows)

    e1 = route_t[:, ROUTE_E1, :].astype(jnp.int32)
    e2 = route_t[:, ROUTE_E2, :].astype(jnp.int32)
    hit = (e1[:, None, :] == expert[None, :, None]) | (e2[:, None, :] == expert[None, :, None])
    run_cnt = jnp.sum(hit.astype(jnp.int32), axis=2)
    tile_prefix = jnp.cumsum(run_cnt, axis=0) - run_cnt
    run_off = jnp.cumsum(run_cnt, axis=1) - run_cnt
    run_start = first_tile[None, :] * tile + tile_prefix
    shift = run_off - tile_prefix - first_tile[None, :] * tile

    def local(pos, e):
        delta = jnp.zeros_like(e)
        for k in range(N_EXPERTS):
            delta = jnp.where(e == k, shift[:, k:k + 1], delta)
        return (pos + delta)[:, None, :]

    xs = _dispatch_call(last_tile, n_real, run_start.reshape(-1), run_off.reshape(-1),
                        run_cnt.reshape(-1), local(pos1, e1), local(pos2, e2), xn, n_tiles * tile)
    ys = _expert_call(first_tile, end_tile, n_real, xs, w1, w3, w2)
    return _combine_call(per_step(pos1, COMBINE_ROWS), per_step(pos2, COMBINE_ROWS), ys, h,
                         route, norm_g)


def _layer(h, norm_mix_g, w_in, b_gates, gmlp_ln_g, gmlp_ln_b, w_spatial, b_spatial, bias_tiles,
           w_out, norm_ffn_g, w_group_router, b_group_router, w_expert_router, b_expert_router,
           w1, w3, w2, norm_out_g):
    b, s, d = h.shape
    t = b * s
    nb = s // MOBA_BLOCK
    x2 = h.reshape(t, d)
    row = lambda v: v.reshape(1, -1)

    ya, qT, k, vT, gb, kmean = _proj_call(
        x2, b, row(norm_mix_g), w_in, row(b_gates), row(gmlp_ln_g), row(gmlp_ln_b),
        w_spatial, b_spatial[:, :, None])

    km = jnp.transpose(kmean.reshape(b, nb, ATT_HEADS, HEAD_DIM), (0, 2, 1, 3))
    km_hi = km.astype(BF16)
    km_lo = (km - km_hi.astype(F32)).astype(BF16)
    yb = _attn_call(qT, k, vT, km_hi, km_lo, bias_tiles)

    w_router = jnp.concatenate(
        [jnp.transpose(w_expert_router, (0, 2, 1)).reshape(N_EXPERTS, d), w_group_router.T,
         jnp.zeros((ROUTER_UNITS - N_EXPERTS - N_GROUPS, d), F32)], axis=0)
    b_router = jnp.concatenate(
        [b_expert_router.reshape(-1), b_group_router,
         jnp.zeros((ROUTER_UNITS - N_EXPERTS - N_GROUPS,), F32)]).reshape(ROUTER_UNITS, 1)
    wr_hi = w_router.astype(BF16)
    wr_lo = (w_router - wr_hi.astype(F32)).astype(BF16)
    h2, xn, route, route_t, counts = _merge_call(
        x2, ya, gb, yb, w_out, row(norm_ffn_g), wr_hi, wr_lo, b_router)

    out = _sparse_moe(xn, route, route_t, counts, h2, w1, w3, w2, row(norm_out_g))
    return out.reshape(b, s, d)


def kernel(x, norm_mix_g, w_in, b_gates, gmlp_ln_g, gmlp_ln_b, w_spatial, b_spatial, rel_bias, w_out, norm_ffn_g, w_group_router, b_group_router, w_expert_router, b_expert_router, w1, w3, w2, norm_final_g):
    depth = w_in.shape[0]
    assert depth == 1, "the final rmsnorm is fused into the last layer's combine kernel"
    bias_tiles = _bias_call(rel_bias)
    return _layer(x, norm_mix_g[0], w_in[0], b_gates[0], gmlp_ln_g[0], gmlp_ln_b[0], w_spatial[0],
                  b_spatial[0], bias_tiles, w_out[0], norm_ffn_g[0], w_group_router[0],
                  b_group_router[0], w_expert_router[0], b_expert_router[0], w1[0], w3[0], w2[0],
                  norm_final_g)
```

```python
import functools
import math

import numpy as np
import jax
import jax.numpy as jnp
from jax import lax
from jax.experimental import pallas as pl
from jax.experimental.pallas import tpu as pltpu

F32 = jnp.float32
BF16 = jnp.bfloat16

D_MODEL = 1024
NORM_EPS = 1e-6
GMLP_GROUPS = 8
GMLP_CHUNK = 128
ATT_HEADS = 8
HEAD_DIM = 128
MOBA_BLOCK = 256
MOBA_TOPK = 3
REL_BUCKETS = 32
REL_MAX_DIST = 128
N_GROUPS = 4
EXPERTS_PER_GROUP = 8
N_EXPERTS = N_GROUPS * EXPERTS_PER_GROUP
N_SEGMENTS = 7

LANES = 128
TOKEN_SUBLANES = 8
ROUTE_ROWS = 8
VMEM_LIMIT_BYTES = 56 * 1024 * 1024

SQRT_HALF = math.sqrt(0.5)
LOG2E = math.log2(math.e)
SCORE_SCALE2 = (HEAD_DIM ** -0.5) * LOG2E
MASK_NEG = -(2.0 ** 100)
BF16_SUBLANES = 16
V_ROWS = HEAD_DIM + BF16_SUBLANES
ROUTER_LANES = LANES
ROUTER_UNITS = -(-(N_EXPERTS + N_GROUPS) // BF16_SUBLANES) * BF16_SUBLANES
GROUP_UNIT0 = N_EXPERTS

PROJ_ROWS = 512
W_STAGE_ROWS = 64
MERGE_ROWS = 1024
EXPERT_ROWS = 256
COMBINE_ROWS = 256
DMA_UNROLL = 8


def _rmsnorm(x, g):
    return x * lax.rsqrt(jnp.mean(x * x, axis=-1, keepdims=True) + NORM_EPS) * g


def _gelu(a):
    return 0.5 * a * (1.0 + lax.erf(a * SQRT_HALF))


def _sigmoid(a):
    return 1.0 / (1.0 + jnp.exp(-a))


def _dot(a, b):
    return jnp.dot(a, b, preferred_element_type=F32)


def _stage_weight_bf16(w_hbm, w_scr, stage, sems):
    chunk = stage.shape[1]
    n_chunks = w_scr.shape[0] // chunk

    def copy(c, slot):
        return pltpu.make_async_copy(
            w_hbm.at[pl.ds(c * chunk, chunk), :], stage.at[slot], sems.at[slot])

    copy(0, 0).start()

    def body(c, carry):
        slot = c % 2

        @pl.when(c + 1 < n_chunks)
        def _():
            copy(c + 1, 1 - slot).start()

        copy(c, slot).wait()
        w_scr[pl.ds(pl.multiple_of(c * chunk, chunk), chunk), :] = stage[slot].astype(BF16)
        return carry

    lax.fori_loop(0, n_chunks, body, 0)


def _proj_kernel(x_ref, ng_ref, w_hbm, bg_ref, lng_ref, lnb_ref, ws_ref, bs_ref,
                 ya_ref, qT_ref, k_ref, vT_ref, gb_ref, kmean_ref,
                 w_ref, w_stage, w_sems, xn_scr, vln_scr, mix_scr):
    rows = x_ref.shape[0]
    d = D_MODEL

    @pl.when(pl.program_id(0) == 0)
    def _():
        _stage_weight_bf16(w_hbm, w_ref, w_stage, w_sems)

    xn_scr[...] = _rmsnorm(x_ref[...], ng_ref[...]).astype(BF16)

    def seg(i):
        return _dot(xn_scr[...], w_ref[:, i * d:(i + 1) * d])

    hd = HEAD_DIM

    v = _gelu(seg(1))
    mu = jnp.mean(v, axis=-1, keepdims=True)
    vc = v - mu
    var = jnp.mean(vc * vc, axis=-1, keepdims=True)
    vln_scr[...] = (vc * lax.rsqrt(var + NORM_EPS) * lng_ref[...] + lnb_ref[...]).astype(BF16)

    mix_scr[...] = _gelu(seg(0)) * _sigmoid(seg(5) + bg_ref[:, :d])

    q = seg(2) * SCORE_SCALE2
    for h in range(ATT_HEADS):
        qT_ref[h] = q[:, h * hd:(h + 1) * hd].T.astype(BF16)

    t_idx = lax.broadcasted_iota(jnp.int32, (GMLP_CHUNK, GMLP_CHUNK), 0)
    s_idx = lax.broadcasted_iota(jnp.int32, (GMLP_CHUNK, GMLP_CHUNK), 1)
    causal = t_idx >= s_idx
    gd = d // GMLP_GROUPS
    n_chunks = rows // GMLP_CHUNK
    for g in range(GMLP_GROUPS):
        ws = jnp.where(causal, ws_ref[g], 0.0).astype(BF16)
        bias = bs_ref[g]
        vg = jnp.concatenate(
            [vln_scr[c * GMLP_CHUNK:(c + 1) * GMLP_CHUNK, g * gd:(g + 1) * gd]
             for c in range(n_chunks)], axis=1)
        mixed = _dot(ws, vg)
        for c in range(n_chunks):
            blk_rows = slice(c * GMLP_CHUNK, (c + 1) * GMLP_CHUNK)
            blk_cols = slice(g * gd, (g + 1) * gd)
            ya_ref[blk_rows, blk_cols] = (
                mix_scr[blk_rows, blk_cols] * (mixed[:, c * gd:(c + 1) * gd] + bias)).astype(BF16)

    v = seg(4)
    for blk in range(rows // MOBA_BLOCK):
        r0 = blk * MOBA_BLOCK
        for h in range(ATT_HEADS):
            vT_ref[h, blk, :hd, :] = v[r0:r0 + MOBA_BLOCK, h * hd:(h + 1) * hd].T.astype(BF16)
            vT_ref[h, blk, hd:, :] = jnp.ones((V_ROWS - hd, MOBA_BLOCK), BF16)

    gb_ref[...] = _sigmoid(seg(6) + bg_ref[:, d:]).astype(BF16)

    k = seg(3)
    k_ref[...] = k.astype(BF16)
    for blk in range(rows // MOBA_BLOCK):
        r0 = blk * MOBA_BLOCK
        kmean_ref[0, blk:blk + 1, :] = jnp.mean(k[r0:r0 + MOBA_BLOCK, :], axis=0, keepdims=True)


def _proj_call(x2, batch, norm_g, w_in, b_gates, ln_g, ln_b, w_spatial, b_spatial):
    t, d = x2.shape
    rows = PROJ_ROWS
    seq = t // batch
    assert seq % rows == 0 and rows % MOBA_BLOCK == 0 and rows % GMLP_CHUNK == 0
    n_tiles = t // rows
    tiles_per_seq = seq // rows
    blocks_per_tile = rows // MOBA_BLOCK
    nb = seq // MOBA_BLOCK
    row_spec = pl.BlockSpec((rows, d), lambda i: (i, 0))
    const2 = lambda i: (0, 0)
    const3 = lambda i: (0, 0, 0)
    act = jax.ShapeDtypeStruct((t, d), BF16)
    qT_spec = pl.BlockSpec((None, ATT_HEADS, HEAD_DIM, rows),
                           lambda i: (i // tiles_per_seq, 0, 0, i % tiles_per_seq))
    vT_spec = pl.BlockSpec((None, ATT_HEADS, blocks_per_tile, V_ROWS, MOBA_BLOCK),
                           lambda i: (i // tiles_per_seq, 0, i % tiles_per_seq, 0, 0))
    return pl.pallas_call(
        _proj_kernel,
        grid=(n_tiles,),
        in_specs=[
            row_spec,
            pl.BlockSpec((1, d), const2),
            pl.BlockSpec(memory_space=pl.ANY),
            pl.BlockSpec((1, 2 * d), const2),
            pl.BlockSpec((1, d), const2),
            pl.BlockSpec((1, d), const2),
            pl.BlockSpec((GMLP_GROUPS, GMLP_CHUNK, GMLP_CHUNK), const3),
            pl.BlockSpec((GMLP_GROUPS, GMLP_CHUNK, 1), const3),
        ],
        out_specs=[row_spec, qT_spec, row_spec, vT_spec, row_spec,
                   pl.BlockSpec((1, blocks_per_tile, d), lambda i: (i, 0, 0))],
        out_shape=[act,
                   jax.ShapeDtypeStruct((batch, ATT_HEADS, HEAD_DIM, seq), BF16),
                   act,
                   jax.ShapeDtypeStruct((batch, ATT_HEADS, nb, V_ROWS, MOBA_BLOCK), BF16),
                   act,
                   jax.ShapeDtypeStruct((n_tiles, blocks_per_tile, d), F32)],
        scratch_shapes=[pltpu.VMEM((d, N_SEGMENTS * d), BF16),
                        pltpu.VMEM((2, W_STAGE_ROWS, N_SEGMENTS * d), F32),
                        pltpu.SemaphoreType.DMA((2,)),
                        pltpu.VMEM((rows, d), BF16), pltpu.VMEM((rows, d), BF16),
                        pltpu.VMEM((rows, d), F32)],
        compiler_params=pltpu.CompilerParams(
            dimension_semantics=("arbitrary",), vmem_limit_bytes=VMEM_LIMIT_BYTES),
        name="proj_gmlp",
    )(x2, norm_g, w_in, b_gates, ln_g, ln_b, w_spatial, b_spatial)


def _t5_bucket_np(n):
    n = np.maximum(n, 0)
    max_exact = REL_BUCKETS // 2
    nf = np.maximum(n, max_exact).astype(np.float32)
    large = max_exact + (np.log(nf / max_exact) / math.log(REL_MAX_DIST / max_exact)
                         * (REL_BUCKETS - max_exact)).astype(np.int32)
    large = np.minimum(large, REL_BUCKETS - 1)
    return np.where(n < max_exact, n, large).astype(np.int32)


def _distance_buckets():
    dist = np.arange(N_BIAS_TILES * MOBA_BLOCK, dtype=np.int32)
    return np.tile(_t5_bucket_np(dist)[None, :], (TOKEN_SUBLANES, 1))


def _bias_kernel(relb_ref, bucket_ref, out_ref):
    k_idx = lax.broadcasted_iota(jnp.int32, (MOBA_BLOCK, MOBA_BLOCK), 0)
    q_idx = lax.broadcasted_iota(jnp.int32, (MOBA_BLOCK, MOBA_BLOCK), 1)
    bucket = bucket_ref[...]
    for h in range(ATT_HEADS):
        far = relb_ref[REL_BUCKETS - 1, h]
        by_dist = jnp.zeros(bucket.shape, F32)
        for b in range(REL_BUCKETS):
            by_dist = jnp.where(bucket == b, relb_ref[b, h], by_dist)
        by_dist = (by_dist - far) * LOG2E
        rows = jnp.broadcast_to(by_dist[:1], (MOBA_BLOCK, N_BIAS_TILES * MOBA_BLOCK))
        toeplitz = pltpu.roll(rows, 0, 1, stride=1, stride_axis=0)
        out_ref[h, BIAS_OWN] = jnp.where(q_idx >= k_idx, toeplitz[:, :MOBA_BLOCK], MASK_NEG)
        out_ref[h, BIAS_PREV] = toeplitz[:, MOBA_BLOCK:]


def _bias_call(rel_bias):
    return pl.pallas_call(
        _bias_kernel,
        in_specs=[pl.BlockSpec(memory_space=pltpu.SMEM),
                  pl.BlockSpec(memory_space=pltpu.VMEM)],
        out_specs=pl.BlockSpec(memory_space=pltpu.VMEM),
        out_shape=jax.ShapeDtypeStruct((ATT_HEADS, N_BIAS_TILES, MOBA_BLOCK, MOBA_BLOCK), F32),
        name="t5_bias_tiles",
    )(rel_bias, jnp.asarray(_distance_buckets()))


BIAS_OWN, BIAS_PREV = range(2)
N_BIAS_TILES = 2


def _attn_kernel(qT_ref, k_ref, vT_ref, kmh_ref, kml_ref, oh_ref, bias_ref, o_ref,
                 qp_scr, m_scr, acc_scr):
    i = pl.program_id(1)
    nh, nb = kmh_ref.shape[0], kmh_ref.shape[1]
    blk = MOBA_BLOCK
    hd = HEAD_DIM

    n_idx = lax.broadcasted_iota(jnp.int32, (nb, blk), 0)
    past = n_idx < i
    for h in range(nh):
        qT = qT_ref[h]
        gate = _dot(kmh_ref[h], qT) + _dot(kml_ref[h], qT)
        gate = jnp.where(past, gate, -jnp.inf)
        rank = jnp.zeros((nb, blk), F32)
        for m in range(nb):
            row = gate[m:m + 1, :]
            beats = (row > gate) | ((row == gate) & (m < n_idx))
            rank = rank + jnp.where(beats, 1.0, 0.0)
        keep = (past & (rank < MOBA_TOPK)) | (n_idx == i)
        sel = jnp.where(keep, 0.0, MASK_NEG)
        sel = jnp.concatenate([sel, jnp.zeros((hd - nb, blk), F32)], axis=0)
        qp_scr[h] = jnp.concatenate([qT, sel.astype(BF16)], axis=0)

    def scores(h, j):
        r0 = pl.multiple_of(j * blk, blk)
        kp = jnp.concatenate([k_ref[pl.ds(r0, blk), h * hd:(h + 1) * hd], oh_ref[j]], axis=1)
        return _dot(kp, qp_scr[h])

    def fold(blocks, bias_tiles, first):
        ss = []
        for h in range(nh):
            parts = []
            for j, tile in zip(blocks, bias_tiles):
                s = scores(h, j)
                parts.append((s if tile is None else s + bias_ref[h, tile]).astype(BF16))
            ss.append(parts)
        ps, alphas = [], []
        for h in range(nh):
            m_new = functools.reduce(
                jnp.maximum, [jnp.max(s, axis=0, keepdims=True) for s in ss[h]]).astype(F32)
            if not first:
                m_old = m_scr[h]
                m_new = jnp.maximum(m_old, m_new)
                alphas.append(jnp.exp2(m_old - m_new))
            m_scr[h] = m_new
            ps.append([jnp.exp2(s - m_new.astype(BF16)) for s in ss[h]])
        for h in range(nh):
            pv = functools.reduce(
                lambda a, b: a + b, [_dot(vT_ref[h, j], x) for j, x in zip(blocks, ps[h])])
            acc_scr[h] = pv if first else alphas[h] * acc_scr[h] + pv

    @pl.when(i == 0)
    def _():
        fold([i], [BIAS_OWN], first=True)

    @pl.when(i >= 1)
    def _():
        fold([i, i - 1], [BIAS_OWN, BIAS_PREV], first=True)

    n_far = i - 1

    def far_pair(p, carry):
        fold([2 * p, 2 * p + 1], [None, None], first=False)
        return carry

    lax.fori_loop(0, n_far // 2, far_pair, 0)

    @pl.when((n_far >= 1) & (n_far % 2 == 1))
    def _():
        fold([n_far - 1], [None], first=False)

    for h in range(nh):
        y = acc_scr[h, :hd, :] * (1.0 / acc_scr[h, hd:hd + 1, :])
        o_ref[:, h * hd:(h + 1) * hd] = y.T.astype(o_ref.dtype)


def _attn_call(qT, k2, vT, km_hi, km_lo, bias_tiles):
    b, h, nb, v_rows, blk = vT.shape
    hd = HEAD_DIM
    s = nb * blk
    onehot = np.zeros((nb, blk, LANES), np.float32)
    for j in range(nb):
        onehot[j, :, j] = 1.0
    once = pl.Buffered(1)
    return pl.pallas_call(
        _attn_kernel,
        grid=(b, nb),
        in_specs=[
            pl.BlockSpec((None, h, hd, blk), lambda bi, i: (bi, 0, 0, i)),
            pl.BlockSpec((s, h * hd), lambda bi, i: (bi, 0)),
            pl.BlockSpec((None, h, nb, v_rows, blk), lambda bi, i: (bi, 0, 0, 0, 0)),
            pl.BlockSpec((None, h, nb, hd), lambda bi, i: (bi, 0, 0, 0)),
            pl.BlockSpec((None, h, nb, hd), lambda bi, i: (bi, 0, 0, 0)),
            pl.BlockSpec((nb, blk, LANES), lambda bi, i: (0, 0, 0), pipeline_mode=once),
            pl.BlockSpec((h, N_BIAS_TILES, blk, blk), lambda bi, i: (0, 0, 0, 0),
                         pipeline_mode=once),
        ],
        out_specs=pl.BlockSpec((blk, h * hd), lambda bi, i: (bi * nb + i, 0)),
        out_shape=jax.ShapeDtypeStruct((b * s, h * hd), BF16),
        scratch_shapes=[pltpu.VMEM((h, 2 * hd, blk), BF16), pltpu.VMEM((h, 1, blk), F32),
                        pltpu.VMEM((h, v_rows, blk), F32)],
        compiler_params=pltpu.CompilerParams(
            dimension_semantics=("arbitrary", "arbitrary"),
            vmem_limit_bytes=VMEM_LIMIT_BYTES),
        name="moba_attention",
    )(qT, k2, vT, km_hi, km_lo, jnp.asarray(onehot, BF16), bias_tiles)


ROUTE_E1, ROUTE_E2, ROUTE_W1, ROUTE_W2, ROUTE_R1, ROUTE_R2 = range(6)


def _dot_nt(a, b):
    return lax.dot_general(a, b, (((1,), (1,)), ((), ())), preferred_element_type=F32)


def _store_token_major(ref, x):
    rows = x.shape[0]
    for s in range(TOKEN_SUBLANES):
        ref[pl.ds(s, rows, stride=TOKEN_SUBLANES), :] = x[:, s * LANES:(s + 1) * LANES]


def _load_token_major(ref, rows):
    return jnp.concatenate(
        [ref[pl.ds(s, rows, stride=TOKEN_SUBLANES), :] for s in range(TOKEN_SUBLANES)], axis=1)


def _merge_kernel(x_ref, ya_ref, gb_ref, yb_ref, wo_ref, ng_ref, wrh_ref, wrl_ref, br_ref,
                  h_ref, xn_ref, route_ref, route_t_ref, counts_ref, run_scr, wo_scr):
    @pl.when(pl.program_id(0) == 0)
    def _():
        run_scr[...] = jnp.zeros_like(run_scr)
        wo_scr[...] = wo_ref[...].astype(BF16)

    f = lambda r: r[...].astype(F32)
    mix = (f(ya_ref) + f(gb_ref) * f(yb_ref)).astype(BF16)
    h = x_ref[...] + _dot(mix, wo_scr[...])
    h_ref[...] = h
    xn = _rmsnorm(h, ng_ref[...])
    _store_token_major(xn_ref, xn)
    rows = xn.shape[0]

    x_hi = xn.astype(BF16)
    x_lo = (xn - x_hi.astype(F32)).astype(BF16)
    logits = (_dot_nt(wrh_ref[...], x_hi) + _dot_nt(wrh_ref[...], x_lo)
              + _dot_nt(wrl_ref[...], x_hi) + br_ref[...])
    unit = lax.broadcasted_iota(jnp.int32, logits.shape, 0).astype(F32)
    big = float(ROUTER_UNITS)
    neg_inf = -jnp.inf

    gl = jnp.where((unit >= GROUP_UNIT0) & (unit < GROUP_UNIT0 + N_GROUPS), logits, neg_inf)
    gmax = jnp.max(gl, axis=0, keepdims=True)
    g_w = 1.0 / jnp.sum(jnp.exp(gl - gmax), axis=0, keepdims=True)
    g_idx = jnp.min(jnp.where(gl == gmax, unit, big), axis=0, keepdims=True) - GROUP_UNIT0

    e0 = g_idx * EXPERTS_PER_GROUP
    el = jnp.where((unit >= e0) & (unit < e0 + EXPERTS_PER_GROUP), logits, neg_inf)
    m1 = jnp.max(el, axis=0, keepdims=True)
    i1 = jnp.min(jnp.where(el == m1, unit, big), axis=0, keepdims=True)
    el2 = jnp.where(unit == i1, neg_inf, el)
    m2 = jnp.max(el2, axis=0, keepdims=True)
    i2 = jnp.min(jnp.where(el2 == m2, unit, big), axis=0, keepdims=True)
    e2 = jnp.exp(m2 - m1)
    den = 1.0 + e2
    w1 = (1.0 / den) * g_w
    w2 = (e2 / den) * g_w

    hit1 = unit == i1
    hit2 = unit == i2
    onehot = jnp.where(hit1, 1.0, jnp.where(hit2, 1.0, 0.0))
    c_idx = lax.broadcasted_iota(jnp.int32, (rows, rows), 0)
    r_idx = lax.broadcasted_iota(jnp.int32, (rows, rows), 1)
    earlier = jnp.where(c_idx < r_idx, 1.0, 0.0).astype(BF16)
    prefix = run_scr[...] + _dot(onehot.astype(BF16), earlier)
    rank1 = jnp.sum(jnp.where(hit1, prefix, 0.0), axis=0, keepdims=True)
    rank2 = jnp.sum(jnp.where(hit2, prefix, 0.0), axis=0, keepdims=True)
    run_scr[...] = run_scr[...] + jnp.sum(onehot, axis=1, keepdims=True)
    counts_ref[...] = run_scr[...]

    route_t = jnp.concatenate(
        [i1, i2, w1, w2, rank1, rank2, jnp.zeros((ROUTER_LANES - 6, rows), F32)], axis=0)
    route_t_ref[0] = route_t[:ROUTE_ROWS]
    route_ref[...] = route_t.T


def _merge_call(x2, ya, gb, yb, w_out, norm_g, wr_hi, wr_lo, b_router):
    t, d = x2.shape
    rows = MERGE_ROWS
    assert t % rows == 0 and d == TOKEN_SUBLANES * LANES
    n_tiles = t // rows
    row_spec = pl.BlockSpec((rows, d), lambda i: (i, 0))
    const2 = lambda i: (0, 0)
    return pl.pallas_call(
        _merge_kernel,
        grid=(n_tiles,),
        in_specs=[row_spec, row_spec, row_spec, row_spec,
                  pl.BlockSpec((d, d), const2, pipeline_mode=pl.Buffered(1)),
                  pl.BlockSpec((1, d), const2),
                  pl.BlockSpec((ROUTER_UNITS, d), const2),
                  pl.BlockSpec((ROUTER_UNITS, d), const2),
                  pl.BlockSpec((ROUTER_UNITS, 1), const2)],
        out_specs=[row_spec,
                   pl.BlockSpec((rows * TOKEN_SUBLANES, LANES), lambda i: (i, 0)),
                   pl.BlockSpec((rows, ROUTER_LANES), lambda i: (i, 0)),
                   pl.BlockSpec((1, ROUTE_ROWS, rows), lambda i: (i, 0, 0)),
                   pl.BlockSpec((ROUTER_UNITS, 1), const2)],
        out_shape=[jax.ShapeDtypeStruct((t, d), F32),
                   jax.ShapeDtypeStruct((t * TOKEN_SUBLANES, LANES), F32),
                   jax.ShapeDtypeStruct((t, ROUTER_LANES), F32),
                   jax.ShapeDtypeStruct((n_tiles, ROUTE_ROWS, rows), F32),
                   jax.ShapeDtypeStruct((ROUTER_UNITS, 1), F32)],
        scratch_shapes=[pltpu.VMEM((ROUTER_UNITS, 1), F32), pltpu.VMEM((d, d), BF16)],
        compiler_params=pltpu.CompilerParams(
            dimension_semantics=("arbitrary",), vmem_limit_bytes=VMEM_LIMIT_BYTES),
        name="merge_outproj_router",
    )(x2, ya, gb, yb, w_out, norm_g, wr_hi, wr_lo, b_router)


def _token_rows(ref, token):
    return ref.at[pl.ds(pl.multiple_of(token * TOKEN_SUBLANES, TOKEN_SUBLANES), TOKEN_SUBLANES)]


def _dispatch_kernel(last_ref, nreal_ref, start_ref, off_ref, cnt_ref, lp1_ref, lp2_ref, xn_ref,
                     xs_hbm, zero_scr, stage, sems):
    rows = lp1_ref.shape[2]
    tile = EXPERT_ROWS
    n_tiles = xs_hbm.shape[0] // (tile * TOKEN_SUBLANES)
    j = pl.program_id(0)
    slot = j % 2
    sem = sems.at[0]

    def wait_step(s):
        for _ in range(2):
            pltpu.make_async_copy(
                xn_ref, xs_hbm.at[pl.ds(0, rows * TOKEN_SUBLANES)], sems.at[s]).wait()

    def zero_tile(j):
        start = pl.multiple_of(j * (tile * TOKEN_SUBLANES), tile * TOKEN_SUBLANES)
        return pltpu.make_async_copy(
            zero_scr, xs_hbm.at[pl.ds(start, tile * TOKEN_SUBLANES)], sem)

    @pl.when(pl.program_id(0) == 0)
    def _():
        zero_scr[...] = jnp.zeros_like(zero_scr)
        for e in range(N_EXPERTS):
            @pl.when(last_ref[e] >= 0)
            def _():
                zero_tile(last_ref[e]).start()

        def tail_start(j, carry):
            zero_tile(j).start()
            return carry

        lax.fori_loop(nreal_ref[0], n_tiles, tail_start, 0)

        for e in range(N_EXPERTS):
            @pl.when(last_ref[e] >= 0)
            def _():
                zero_tile(0).wait()

        def tail_wait(j, carry):
            zero_tile(0).wait()
            return carry

        lax.fori_loop(nreal_ref[0], n_tiles, tail_wait, 0)

    @pl.when(j >= 2)
    def _():
        wait_step(slot)

    buf = stage.at[slot]

    def place(g, carry):
        for u in range(DMA_UNROLL):
            r = g * DMA_UNROLL + u
            tok = _token_rows(xn_ref, r)[...]
            _token_rows(buf, lp1_ref[0, 0, r])[...] = tok
            _token_rows(buf, lp2_ref[0, 0, r])[...] = tok
        return carry

    lax.fori_loop(0, rows // DMA_UNROLL, place, 0)

    def run(e, carry):
        idx = j * N_EXPERTS + e
        n = cnt_ref[idx]
        src0 = off_ref[idx]
        dst0 = start_ref[idx]
        for b in reversed(range(rows.bit_length())):
            size = 1 << b
            done = n & ~(2 * size - 1)

            @pl.when((n & size) != 0)
            def _():
                src = buf.at[pl.ds(pl.multiple_of((src0 + done) * TOKEN_SUBLANES, TOKEN_SUBLANES),
                                   size * TOKEN_SUBLANES)]
                dst = xs_hbm.at[pl.ds(pl.multiple_of((dst0 + done) * TOKEN_SUBLANES,
                                                     TOKEN_SUBLANES), size * TOKEN_SUBLANES)]
                pltpu.make_async_copy(src, dst, sems.at[slot]).start()
        return carry

    lax.fori_loop(0, N_EXPERTS, run, 0)

    @pl.when(j == pl.num_programs(0) - 1)
    def _():
        wait_step(slot)

        @pl.when(j >= 1)
        def _():
            wait_step(1 - slot)


def _dispatch_call(last_tile, n_real, run_start, run_off, run_cnt, lpos1, lpos2, xn,
                   n_sorted_rows):
    n_steps, _, rows = lpos1.shape
    smem_row = pl.BlockSpec((1, 1, rows), lambda i, *_: (i, 0, 0), memory_space=pltpu.SMEM)
    return pl.pallas_call(
        _dispatch_kernel,
        grid_spec=pltpu.PrefetchScalarGridSpec(
            num_scalar_prefetch=5,
            grid=(n_steps,),
            in_specs=[smem_row, smem_row,
                      pl.BlockSpec((rows * TOKEN_SUBLANES, LANES), lambda i, *_: (i, 0))],
            out_specs=pl.BlockSpec(memory_space=pl.ANY),
            scratch_shapes=[pltpu.VMEM((EXPERT_ROWS * TOKEN_SUBLANES, LANES), F32),
                            pltpu.VMEM((2, 2 * rows * TOKEN_SUBLANES, LANES), F32),
                            pltpu.SemaphoreType.DMA((2,))],
        ),
        out_shape=jax.ShapeDtypeStruct((n_sorted_rows * TOKEN_SUBLANES, LANES), F32),
        compiler_params=pltpu.CompilerParams(
            dimension_semantics=("arbitrary",), vmem_limit_bytes=VMEM_LIMIT_BYTES),
        name="moe_dispatch",
    )(last_tile, n_real, run_start, run_off, run_cnt, lpos1, lpos2, xn)


EXPERT_IN_SLOTS = 6
EXPERT_OUT_SLOTS = 4


def _expert_kernel(first_ref, end_ref, nreal_ref, xs_hbm, w1_ref, w3_ref, w2_ref, ys_hbm,
                   w1_scr, w3_scr, w2_scr, xbuf, ybuf, in_sems, out_sems):
    e = pl.program_id(0)
    tile_rows = EXPERT_ROWS * TOKEN_SUBLANES
    n_real = nreal_ref[0]
    n_tiles = xs_hbm.shape[0] // tile_rows

    def tile_of(ref, t):
        return ref.at[pl.ds(pl.multiple_of(t * tile_rows, tile_rows), tile_rows)]

    def in_copy(t):
        slot = t % EXPERT_IN_SLOTS
        return pltpu.make_async_copy(tile_of(xs_hbm, t), xbuf.at[slot], in_sems.at[slot])

    def out_copy(t):
        slot = t % EXPERT_OUT_SLOTS
        return pltpu.make_async_copy(ybuf.at[slot], tile_of(ys_hbm, t), out_sems.at[slot])

    @pl.when(e == 0)
    def _():
        for t in range(EXPERT_IN_SLOTS - 1):
            @pl.when(t < n_real)
            def _():
                in_copy(t).start()

    w1_scr[...] = w1_ref[...].astype(BF16)
    w3_scr[...] = w3_ref[...].astype(BF16)
    w2_scr[...] = w2_ref[...].astype(BF16)

    def tile(t, carry):
        ahead = t + EXPERT_IN_SLOTS - 1

        @pl.when(ahead < n_real)
        def _():
            in_copy(ahead).start()

        in_copy(t).wait()

        @pl.when(t >= EXPERT_OUT_SLOTS)
        def _():
            out_copy(t - EXPERT_OUT_SLOTS).wait()

        x = _load_token_major(xbuf.at[t % EXPERT_IN_SLOTS], EXPERT_ROWS).astype(BF16)
        a = _dot(x, w1_scr[...])
        b = _dot(x, w3_scr[...])
        hid = (a * _sigmoid(a)) * b
        _store_token_major(ybuf.at[t % EXPERT_OUT_SLOTS], _dot(hid.astype(BF16), w2_scr[...]))
        out_copy(t).start()
        return carry

    lax.fori_loop(first_ref[e], end_ref[e], tile, 0)

    @pl.when(e == pl.num_programs(0) - 1)
    def _():
        for back in range(EXPERT_OUT_SLOTS, 0, -1):
            @pl.when(n_real - back >= 0)
            def _():
                out_copy(n_real - back).wait()
        ybuf[0] = jnp.zeros(ybuf.shape[1:], F32)

        def tail_start(t, carry):
            pltpu.make_async_copy(ybuf.at[0], tile_of(ys_hbm, t), out_sems.at[0]).start()
            return carry

        def tail_wait(t, carry):
            pltpu.make_async_copy(ybuf.at[0], tile_of(ys_hbm, t), out_sems.at[0]).wait()
            return carry

        lax.fori_loop(n_real, n_tiles, tail_start, 0)
        lax.fori_loop(n_real, n_tiles, tail_wait, 0)


def _expert_call(first_tile, end_tile, n_real, xs, w1, w3, w2):
    n_experts, d, d_expert = w1.shape
    tile_rows = EXPERT_ROWS * TOKEN_SUBLANES
    per_expert = lambda e, f, n, nr: (e, 0, 0)
    return pl.pallas_call(
        _expert_kernel,
        grid_spec=pltpu.PrefetchScalarGridSpec(
            num_scalar_prefetch=3,
            grid=(n_experts,),
            in_specs=[pl.BlockSpec(memory_space=pl.ANY),
                      pl.BlockSpec((None, d, d_expert), per_expert),
                      pl.BlockSpec((None, d, d_expert), per_expert),
                      pl.BlockSpec((None, d_expert, d), per_expert)],
            out_specs=pl.BlockSpec(memory_space=pl.ANY),
            scratch_shapes=[pltpu.VMEM((d, d_expert), BF16), pltpu.VMEM((d, d_expert), BF16),
                            pltpu.VMEM((d_expert, d), BF16),
                            pltpu.VMEM((EXPERT_IN_SLOTS, tile_rows, LANES), F32),
                            pltpu.VMEM((EXPERT_OUT_SLOTS, tile_rows, LANES), F32),
                            pltpu.SemaphoreType.DMA((EXPERT_IN_SLOTS,)),
                            pltpu.SemaphoreType.DMA((EXPERT_OUT_SLOTS,))],
        ),
        out_shape=jax.ShapeDtypeStruct(xs.shape, F32),
        compiler_params=pltpu.CompilerParams(
            dimension_semantics=("arbitrary",), vmem_limit_bytes=VMEM_LIMIT_BYTES),
        name="moe_experts",
    )(first_tile, end_tile, n_real, xs, w1, w3, w2)


def _combine_kernel(start_ref, off_ref, cnt_ref, lp1_ref, lp2_ref, ys_hbm, h_ref, route_ref,
                    ng_ref, out_ref, stage, buf, sems):
    i = pl.program_id(0)
    n_steps = pl.num_programs(0)
    rows = h_ref.shape[0]

    def fetch(step, slot):
        def run(e, carry):
            idx = step * N_EXPERTS + e
            n = cnt_ref[idx]
            src0 = start_ref[idx]
            dst0 = off_ref[idx]
            for b in reversed(range(rows.bit_length())):
                size = 1 << b
                done = n & ~(2 * size - 1)

                @pl.when((n & size) != 0)
                def _():
                    src = ys_hbm.at[pl.ds(pl.multiple_of((src0 + done) * TOKEN_SUBLANES,
                                                         TOKEN_SUBLANES), size * TOKEN_SUBLANES)]
                    dst = stage.at[slot, pl.ds(pl.multiple_of((dst0 + done) * TOKEN_SUBLANES,
                                                              TOKEN_SUBLANES),
                                               size * TOKEN_SUBLANES)]
                    pltpu.make_async_copy(src, dst, sems.at[slot]).start()
            return carry

        lax.fori_loop(0, N_EXPERTS, run, 0)

    @pl.when(i == 0)
    def _():
        fetch(0, 0)

    @pl.when(i + 1 < n_steps)
    def _():
        fetch(i + 1, (i + 1) % 2)

    slot = i % 2
    for which in range(2):
        pltpu.make_async_copy(ys_hbm.at[pl.ds(0, rows * TOKEN_SUBLANES)],
                              buf.at[which], sems.at[slot]).wait()

    src = stage.at[slot]

    def place(g, carry):
        for u in range(DMA_UNROLL):
            r = g * DMA_UNROLL + u
            _token_rows(buf.at[0], r)[...] = _token_rows(src, lp1_ref[0, 0, r])[...]
            _token_rows(buf.at[1], r)[...] = _token_rows(src, lp2_ref[0, 0, r])[...]
        return carry

    lax.fori_loop(0, rows // DMA_UNROLL, place, 0)

    chunk = COMBINE_ROWS
    for c in range(rows // chunk):
        r0 = c * chunk
        route = route_ref[r0:r0 + chunk, :]
        w1 = route[:, ROUTE_W1:ROUTE_W1 + 1]
        w2 = route[:, ROUTE_W2:ROUTE_W2 + 1]
        tiles = pl.ds(r0 * TOKEN_SUBLANES, chunk * TOKEN_SUBLANES)
        y = (h_ref[r0:r0 + chunk, :] + w1 * _load_token_major(buf.at[0, tiles], chunk)
             + w2 * _load_token_major(buf.at[1, tiles], chunk))
        out_ref[r0:r0 + chunk, :] = _rmsnorm(y, ng_ref[...])


def _combine_call(run_start, run_off, run_cnt, lpos1, lpos2, ys, h, route, norm_g):
    t, d = h.shape
    n_steps, _, rows = lpos1.shape
    row_spec = pl.BlockSpec((rows, d), lambda i, *_: (i, 0))
    smem_row = pl.BlockSpec((1, 1, rows), lambda i, *_: (i, 0, 0), memory_space=pltpu.SMEM)
    return pl.pallas_call(
        _combine_kernel,
        grid_spec=pltpu.PrefetchScalarGridSpec(
            num_scalar_prefetch=3,
            grid=(n_steps,),
            in_specs=[smem_row, smem_row,
                      pl.BlockSpec(memory_space=pl.ANY),
                      row_spec,
                      pl.BlockSpec((rows, ROUTER_LANES), lambda i, *_: (i, 0)),
                      pl.BlockSpec((1, d), lambda i, *_: (0, 0))],
            out_specs=row_spec,
            scratch_shapes=[pltpu.VMEM((2, 2 * rows * TOKEN_SUBLANES, LANES), F32),
                            pltpu.VMEM((2, rows * TOKEN_SUBLANES, LANES), F32),
                            pltpu.SemaphoreType.DMA((2,))],
        ),
        out_shape=jax.ShapeDtypeStruct((t, d), F32),
        compiler_params=pltpu.CompilerParams(
            dimension_semantics=("arbitrary",), vmem_limit_bytes=VMEM_LIMIT_BYTES),
        name="moe_combine",
    )(run_start, run_off, run_cnt, lpos1, lpos2, ys, h, route, norm_g)


def _sparse_moe(xn, route, route_t, counts, h, w1, w3, w2, norm_g):
    t = h.shape[0]
    tile = EXPERT_ROWS
    n_tiles = (2 * t) // tile + N_EXPERTS
    expert = jnp.arange(N_EXPERTS, dtype=jnp.int32)
    counts = counts[:N_EXPERTS, 0].astype(jnp.int32)
    group_tiles = (counts + tile - 1) // tile
    end_tile = jnp.sum(jnp.where(expert[None, :] <= expert[:, None], group_tiles[None, :], 0), axis=1)
    first_tile = end_tile - group_tiles
    n_real = end_tile[-1:]
    last_tile = jnp.where(group_tiles > 0, end_tile - 1, -1)

    def positions(e_row, r_row):
        e = route_t[:, e_row, :].astype(jnp.int32)
        start = jnp.zeros_like(e)
        for k in range(N_EXPERTS):
            start = jnp.where(e == k, first_tile[k] * tile, start)
        return start + route_t[:, r_row, :].astype(jnp.int32)

    pos1 = positions(ROUTE_E1, ROUTE_R1)
    pos2 = positions(ROUTE_E2, ROUTE_R2)

    e1 = route_t[:, ROUTE_E1, :].astype(jnp.int32)
    e2 = route_t[:, ROUTE_E2, :].astype(jnp.int32)
    hit = (e1[:, None, :] == expert[None, :, None]) | (e2[:, None, :] == expert[None, :, None])
    run_cnt = jnp.sum(hit.astype(jnp.int32), axis=2)
    tile_prefix = jnp.cumsum(run_cnt, axis=0) - run_cnt
    run_off = jnp.cumsum(run_cnt, axis=1) - run_cnt
    run_start = first_tile[None, :] * tile + tile_prefix
    shift = run_off - tile_prefix - first_tile[None, :] * tile

    def local(pos, e):
        delta = jnp.zeros_like(e)
        for k in range(N_EXPERTS):
            delta = jnp.where(e == k, shift[:, k:k + 1], delta)
        return (pos + delta)[:, None, :]

    runs = (run_start.reshape(-1), run_off.reshape(-1), run_cnt.reshape(-1),
            local(pos1, e1), local(pos2, e2))
    xs = _dispatch_call(last_tile, n_real, *runs, xn, n_tiles * tile)
    ys = _expert_call(first_tile, end_tile, n_real, xs, w1, w3, w2)
    return _combine_call(*runs, ys, h, route, norm_g)


def _layer(h, norm_mix_g, w_in, b_gates, gmlp_ln_g, gmlp_ln_b, w_spatial, b_spatial, bias_tiles,
           w_out, norm_ffn_g, w_group_router, b_group_router, w_expert_router, b_expert_router,
           w1, w3, w2, norm_out_g):
    b, s, d = h.shape
    t = b * s
    nb = s // MOBA_BLOCK
    x2 = h.reshape(t, d)
    row = lambda v: v.reshape(1, -1)

    ya, qT, k, vT, gb, kmean = _proj_call(
        x2, b, row(norm_mix_g), w_in, row(b_gates), row(gmlp_ln_g), row(gmlp_ln_b),
        w_spatial, b_spatial[:, :, None])

    km = jnp.transpose(kmean.reshape(b, nb, ATT_HEADS, HEAD_DIM), (0, 2, 1, 3))
    km_hi = km.astype(BF16)
    km_lo = (km - km_hi.astype(F32)).astype(BF16)
    yb = _attn_call(qT, k, vT, km_hi, km_lo, bias_tiles)

    w_router = jnp.concatenate(
        [jnp.transpose(w_expert_router, (0, 2, 1)).reshape(N_EXPERTS, d), w_group_router.T,
         jnp.zeros((ROUTER_UNITS - N_EXPERTS - N_GROUPS, d), F32)], axis=0)
    b_router = jnp.concatenate(
        [b_expert_router.reshape(-1), b_group_router,
         jnp.zeros((ROUTER_UNITS - N_EXPERTS - N_GROUPS,), F32)]).reshape(ROUTER_UNITS, 1)
    wr_hi = w_router.astype(BF16)
    wr_lo = (w_router - wr_hi.astype(F32)).astype(BF16)
    h2, xn, route, route_t, counts = _merge_call(
        x2, ya, gb, yb, w_out, row(norm_ffn_g), wr_hi, wr_lo, b_router)

    out = _sparse_moe(xn, route, route_t, counts, h2, w1, w3, w2, row(norm_out_g))
    return out.reshape(b, s, d)


def kernel(x, norm_mix_g, w_in, b_gates, gmlp_ln_g, gmlp_ln_b, w_spatial, b_spatial, rel_bias, w_out, norm_ffn_g, w_group_router, b_group_router, w_expert_router, b_expert_router, w1, w3, w2, norm_final_g):
    depth = w_in.shape[0]
    assert depth == 1, "the final rmsnorm is fused into the last layer's combine kernel"
    bias_tiles = _bias_call(rel_bias)
    return _layer(x, norm_mix_g[0], w_in[0], b_gates[0], gmlp_ln_g[0], gmlp_ln_b[0], w_spatial[0],
                  b_spatial[0], bias_tiles, w_out[0], norm_ffn_g[0], w_group_router[0],
                  b_group_router[0], w_expert_router[0], b_expert_router[0], w1[0], w3[0], w2[0],
                  norm_final_g)
```

```python
import functools
import math

import numpy as np
import jax
import jax.numpy as jnp
from jax import lax
from jax.experimental import pallas as pl
from jax.experimental.pallas import tpu as pltpu

F32 = jnp.float32
BF16 = jnp.bfloat16

D_MODEL = 1024
NORM_EPS = 1e-6
GMLP_GROUPS = 8
GMLP_CHUNK = 128
ATT_HEADS = 8
HEAD_DIM = 128
MOBA_BLOCK = 256
MOBA_TOPK = 3
REL_BUCKETS = 32
REL_MAX_DIST = 128
N_GROUPS = 4
EXPERTS_PER_GROUP = 8
N_EXPERTS = N_GROUPS * EXPERTS_PER_GROUP
N_SEGMENTS = 7

LANES = 128
TOKEN_SUBLANES = 8
ROUTE_ROWS = 8
VMEM_LIMIT_BYTES = 56 * 1024 * 1024

SQRT_HALF = math.sqrt(0.5)
LOG2E = math.log2(math.e)
SCORE_SCALE2 = (HEAD_DIM ** -0.5) * LOG2E
MASK_NEG = -(2.0 ** 100)
BF16_SUBLANES = 16
V_ROWS = HEAD_DIM + BF16_SUBLANES
ROUTER_LANES = LANES
ROUTER_UNITS = -(-(N_EXPERTS + N_GROUPS) // BF16_SUBLANES) * BF16_SUBLANES
GROUP_UNIT0 = N_EXPERTS

PROJ_ROWS = 512
W_STAGE_ROWS = 64
MERGE_ROWS = 1024
EXPERT_ROWS = 256
COMBINE_ROWS = 256
DMA_UNROLL = 8


def _rmsnorm(x, g):
    return x * lax.rsqrt(jnp.mean(x * x, axis=-1, keepdims=True) + NORM_EPS) * g


def _gelu(a):
    return 0.5 * a * (1.0 + lax.erf(a * SQRT_HALF))


def _sigmoid(a):
    return 1.0 / (1.0 + jnp.exp(-a))


def _dot(a, b):
    return jnp.dot(a, b, preferred_element_type=F32)


def _stage_weight_bf16(w_hbm, w_scr, stage, sems):
    chunk = stage.shape[1]
    n_chunks = w_scr.shape[0] // chunk

    def copy(c, slot):
        return pltpu.make_async_copy(
            w_hbm.at[pl.ds(c * chunk, chunk), :], stage.at[slot], sems.at[slot])

    copy(0, 0).start()

    def body(c, carry):
        slot = c % 2

        @pl.when(c + 1 < n_chunks)
        def _():
            copy(c + 1, 1 - slot).start()

        copy(c, slot).wait()
        w_scr[pl.ds(pl.multiple_of(c * chunk, chunk), chunk), :] = stage[slot].astype(BF16)
        return carry

    lax.fori_loop(0, n_chunks, body, 0)


def _proj_kernel(x_ref, ng_ref, w_hbm, bg_ref, lng_ref, lnb_ref, ws_ref, bs_ref,
                 ya_ref, qT_ref, k_ref, vT_ref, gb_ref, kmean_ref,
                 w_ref, w_stage, w_sems, xn_scr, vln_scr, mix_scr):
    rows = x_ref.shape[0]
    d = D_MODEL

    @pl.when(pl.program_id(0) == 0)
    def _():
        _stage_weight_bf16(w_hbm, w_ref, w_stage, w_sems)

    xn_scr[...] = _rmsnorm(x_ref[...], ng_ref[...]).astype(BF16)

    def seg(i):
        return _dot(xn_scr[...], w_ref[:, i * d:(i + 1) * d])

    hd = HEAD_DIM

    v = _gelu(seg(1))
    mu = jnp.mean(v, axis=-1, keepdims=True)
    vc = v - mu
    var = jnp.mean(vc * vc, axis=-1, keepdims=True)
    vln_scr[...] = (vc * lax.rsqrt(var + NORM_EPS) * lng_ref[...] + lnb_ref[...]).astype(BF16)

    mix_scr[...] = _gelu(seg(0)) * _sigmoid(seg(5) + bg_ref[:, :d])

    q = seg(2) * SCORE_SCALE2
    for h in range(ATT_HEADS):
        qT_ref[h] = q[:, h * hd:(h + 1) * hd].T.astype(BF16)

    t_idx = lax.broadcasted_iota(jnp.int32, (GMLP_CHUNK, GMLP_CHUNK), 0)
    s_idx = lax.broadcasted_iota(jnp.int32, (GMLP_CHUNK, GMLP_CHUNK), 1)
    causal = t_idx >= s_idx
    gd = d // GMLP_GROUPS
    n_chunks = rows // GMLP_CHUNK
    for g in range(GMLP_GROUPS):
        ws = jnp.where(causal, ws_ref[g], 0.0).astype(BF16)
        bias = bs_ref[g]
        vg = jnp.concatenate(
            [vln_scr[c * GMLP_CHUNK:(c + 1) * GMLP_CHUNK, g * gd:(g + 1) * gd]
             for c in range(n_chunks)], axis=1)
        mixed = _dot(ws, vg)
        for c in range(n_chunks):
            blk_rows = slice(c * GMLP_CHUNK, (c + 1) * GMLP_CHUNK)
            blk_cols = slice(g * gd, (g + 1) * gd)
            ya_ref[blk_rows, blk_cols] = (
                mix_scr[blk_rows, blk_cols] * (mixed[:, c * gd:(c + 1) * gd] + bias)).astype(BF16)

    v = seg(4)
    for blk in range(rows // MOBA_BLOCK):
        r0 = blk * MOBA_BLOCK
        for h in range(ATT_HEADS):
            vT_ref[h, blk, :hd, :] = v[r0:r0 + MOBA_BLOCK, h * hd:(h + 1) * hd].T.astype(BF16)
            vT_ref[h, blk, hd:, :] = jnp.ones((V_ROWS - hd, MOBA_BLOCK), BF16)

    gb_ref[...] = _sigmoid(seg(6) + bg_ref[:, d:]).astype(BF16)

    k = seg(3)
    k_ref[...] = k.astype(BF16)
    for blk in range(rows // MOBA_BLOCK):
        r0 = blk * MOBA_BLOCK
        kmean_ref[0, blk:blk + 1, :] = jnp.mean(k[r0:r0 + MOBA_BLOCK, :], axis=0, keepdims=True)


def _proj_call(x2, batch, norm_g, w_in, b_gates, ln_g, ln_b, w_spatial, b_spatial):
    t, d = x2.shape
    rows = PROJ_ROWS
    seq = t // batch
    assert seq % rows == 0 and rows % MOBA_BLOCK == 0 and rows % GMLP_CHUNK == 0
    n_tiles = t // rows
    tiles_per_seq = seq // rows
    blocks_per_tile = rows // MOBA_BLOCK
    nb = seq // MOBA_BLOCK
    row_spec = pl.BlockSpec((rows, d), lambda i: (i, 0))
    const2 = lambda i: (0, 0)
    const3 = lambda i: (0, 0, 0)
    act = jax.ShapeDtypeStruct((t, d), BF16)
    qT_spec = pl.BlockSpec((None, ATT_HEADS, HEAD_DIM, rows),
                           lambda i: (i // tiles_per_seq, 0, 0, i % tiles_per_seq))
    vT_spec = pl.BlockSpec((None, ATT_HEADS, blocks_per_tile, V_ROWS, MOBA_BLOCK),
                           lambda i: (i // tiles_per_seq, 0, i % tiles_per_seq, 0, 0))
    return pl.pallas_call(
        _proj_kernel,
        grid=(n_tiles,),
        in_specs=[
            row_spec,
            pl.BlockSpec((1, d), const2),
            pl.BlockSpec(memory_space=pl.ANY),
            pl.BlockSpec((1, 2 * d), const2),
            pl.BlockSpec((1, d), const2),
            pl.BlockSpec((1, d), const2),
            pl.BlockSpec((GMLP_GROUPS, GMLP_CHUNK, GMLP_CHUNK), const3),
            pl.BlockSpec((GMLP_GROUPS, GMLP_CHUNK, 1), const3),
        ],
        out_specs=[row_spec, qT_spec, row_spec, vT_spec, row_spec,
                   pl.BlockSpec((1, blocks_per_tile, d), lambda i: (i, 0, 0))],
        out_shape=[act,
                   jax.ShapeDtypeStruct((batch, ATT_HEADS, HEAD_DIM, seq), BF16),
                   act,
                   jax.ShapeDtypeStruct((batch, ATT_HEADS, nb, V_ROWS, MOBA_BLOCK), BF16),
                   act,
                   jax.ShapeDtypeStruct((n_tiles, blocks_per_tile, d), F32)],
        scratch_shapes=[pltpu.VMEM((d, N_SEGMENTS * d), BF16),
                        pltpu.VMEM((2, W_STAGE_ROWS, N_SEGMENTS * d), F32),
                        pltpu.SemaphoreType.DMA((2,)),
                        pltpu.VMEM((rows, d), BF16), pltpu.VMEM((rows, d), BF16),
                        pltpu.VMEM((rows, d), F32)],
        compiler_params=pltpu.CompilerParams(
            dimension_semantics=("arbitrary",), vmem_limit_bytes=VMEM_LIMIT_BYTES),
        name="proj_gmlp",
    )(x2, norm_g, w_in, b_gates, ln_g, ln_b, w_spatial, b_spatial)


def _t5_bucket_np(n):
    n = np.maximum(n, 0)
    max_exact = REL_BUCKETS // 2
    nf = np.maximum(n, max_exact).astype(np.float32)
    large = max_exact + (np.log(nf / max_exact) / math.log(REL_MAX_DIST / max_exact)
                         * (REL_BUCKETS - max_exact)).astype(np.int32)
    large = np.minimum(large, REL_BUCKETS - 1)
    return np.where(n < max_exact, n, large).astype(np.int32)


def _distance_buckets():
    dist = np.arange(N_BIAS_TILES * MOBA_BLOCK, dtype=np.int32)
    return np.tile(_t5_bucket_np(dist)[None, :], (TOKEN_SUBLANES, 1))


def _bias_kernel(relb_ref, bucket_ref, out_ref):
    k_idx = lax.broadcasted_iota(jnp.int32, (MOBA_BLOCK, MOBA_BLOCK), 0)
    q_idx = lax.broadcasted_iota(jnp.int32, (MOBA_BLOCK, MOBA_BLOCK), 1)
    bucket = bucket_ref[...]
    for h in range(ATT_HEADS):
        far = relb_ref[REL_BUCKETS - 1, h]
        by_dist = jnp.zeros(bucket.shape, F32)
        for b in range(REL_BUCKETS):
            by_dist = jnp.where(bucket == b, relb_ref[b, h], by_dist)
        by_dist = (by_dist - far) * LOG2E
        rows = jnp.broadcast_to(by_dist[:1], (MOBA_BLOCK, N_BIAS_TILES * MOBA_BLOCK))
        toeplitz = pltpu.roll(rows, 0, 1, stride=1, stride_axis=0)
        out_ref[h, BIAS_OWN] = jnp.where(q_idx >= k_idx, toeplitz[:, :MOBA_BLOCK], MASK_NEG)
        out_ref[h, BIAS_PREV] = toeplitz[:, MOBA_BLOCK:]


def _bias_call(rel_bias):
    return pl.pallas_call(
        _bias_kernel,
        in_specs=[pl.BlockSpec(memory_space=pltpu.SMEM),
                  pl.BlockSpec(memory_space=pltpu.VMEM)],
        out_specs=pl.BlockSpec(memory_space=pltpu.VMEM),
        out_shape=jax.ShapeDtypeStruct((ATT_HEADS, N_BIAS_TILES, MOBA_BLOCK, MOBA_BLOCK), F32),
        name="t5_bias_tiles",
    )(rel_bias, jnp.asarray(_distance_buckets()))


BIAS_OWN, BIAS_PREV = range(2)
N_BIAS_TILES = 2


def _attn_kernel(qT_ref, k_ref, vT_ref, kmh_ref, kml_ref, oh_ref, bias_ref, o_ref,
                 qp_scr, m_scr, acc_scr):
    i = pl.program_id(1)
    nh, nb = kmh_ref.shape[0], kmh_ref.shape[1]
    blk = MOBA_BLOCK
    hd = HEAD_DIM

    n_idx = lax.broadcasted_iota(jnp.int32, (nb, blk), 0)
    past = n_idx < i
    for h in range(nh):
        qT = qT_ref[h]
        gate = _dot(kmh_ref[h], qT) + _dot(kml_ref[h], qT)
        gate = jnp.where(past, gate, -jnp.inf)
        rank = jnp.zeros((nb, blk), F32)
        for m in range(nb):
            row = gate[m:m + 1, :]
            beats = (row > gate) | ((row == gate) & (m < n_idx))
            rank = rank + jnp.where(beats, 1.0, 0.0)
        keep = (past & (rank < MOBA_TOPK)) | (n_idx == i)
        sel = jnp.where(keep, 0.0, MASK_NEG)
        sel = jnp.concatenate([sel, jnp.zeros((hd - nb, blk), F32)], axis=0)
        qp_scr[h] = jnp.concatenate([qT, sel.astype(BF16)], axis=0)

    def scores(h, j):
        r0 = pl.multiple_of(j * blk, blk)
        kp = jnp.concatenate([k_ref[pl.ds(r0, blk), h * hd:(h + 1) * hd], oh_ref[j]], axis=1)
        return _dot(kp, qp_scr[h])

    def fold(blocks, bias_tiles, first):
        ss = []
        for h in range(nh):
            parts = []
            for j, tile in zip(blocks, bias_tiles):
                s = scores(h, j)
                parts.append((s if tile is None else s + bias_ref[h, tile]).astype(BF16))
            ss.append(parts)
        ps, alphas = [], []
        for h in range(nh):
            m_new = functools.reduce(
                jnp.maximum, [jnp.max(s, axis=0, keepdims=True) for s in ss[h]]).astype(F32)
            if not first:
                m_old = m_scr[h]
                m_new = jnp.maximum(m_old, m_new)
                alphas.append(jnp.exp2(m_old - m_new))
            m_scr[h] = m_new
            ps.append([jnp.exp2(s - m_new.astype(BF16)) for s in ss[h]])
        for h in range(nh):
            pv = functools.reduce(
                lambda a, b: a + b, [_dot(vT_ref[h, j], x) for j, x in zip(blocks, ps[h])])
            acc_scr[h] = pv if first else alphas[h] * acc_scr[h] + pv

    @pl.when(i == 0)
    def _():
        fold([i], [BIAS_OWN], first=True)

    @pl.when(i >= 1)
    def _():
        fold([i, i - 1], [BIAS_OWN, BIAS_PREV], first=True)

    n_far = i - 1

    def far_pair(p, carry):
        fold([2 * p, 2 * p + 1], [None, None], first=False)
        return carry

    lax.fori_loop(0, n_far // 2, far_pair, 0)

    @pl.when((n_far >= 1) & (n_far % 2 == 1))
    def _():
        fold([n_far - 1], [None], first=False)

    for h in range(nh):
        y = acc_scr[h, :hd, :] * (1.0 / acc_scr[h, hd:hd + 1, :])
        o_ref[:, h * hd:(h + 1) * hd] = y.T.astype(o_ref.dtype)


def _attn_call(qT, k2, vT, km_hi, km_lo, bias_tiles):
    b, h, nb, v_rows, blk = vT.shape
    hd = HEAD_DIM
    s = nb * blk
    onehot = np.zeros((nb, blk, LANES), np.float32)
    for j in range(nb):
        onehot[j, :, j] = 1.0
    once = pl.Buffered(1)
    return pl.pallas_call(
        _attn_kernel,
        grid=(b, nb),
        in_specs=[
            pl.BlockSpec((None, h, hd, blk), lambda bi, i: (bi, 0, 0, i)),
            pl.BlockSpec((s, h * hd), lambda bi, i: (bi, 0)),
            pl.BlockSpec((None, h, nb, v_rows, blk), lambda bi, i: (bi, 0, 0, 0, 0)),
            pl.BlockSpec((None, h, nb, hd), lambda bi, i: (bi, 0, 0, 0)),
            pl.BlockSpec((None, h, nb, hd), lambda bi, i: (bi, 0, 0, 0)),
            pl.BlockSpec((nb, blk, LANES), lambda bi, i: (0, 0, 0), pipeline_mode=once),
            pl.BlockSpec((h, N_BIAS_TILES, blk, blk), lambda bi, i: (0, 0, 0, 0),
                         pipeline_mode=once),
        ],
        out_specs=pl.BlockSpec((blk, h * hd), lambda bi, i: (bi * nb + i, 0)),
        out_shape=jax.ShapeDtypeStruct((b * s, h * hd), BF16),
        scratch_shapes=[pltpu.VMEM((h, 2 * hd, blk), BF16), pltpu.VMEM((h, 1, blk), F32),
                        pltpu.VMEM((h, v_rows, blk), F32)],
        compiler_params=pltpu.CompilerParams(
            dimension_semantics=("arbitrary", "arbitrary"),
            vmem_limit_bytes=VMEM_LIMIT_BYTES),
        name="moba_attention",
    )(qT, k2, vT, km_hi, km_lo, jnp.asarray(onehot, BF16), bias_tiles)


ROUTE_E1, ROUTE_E2, ROUTE_W1, ROUTE_W2, ROUTE_R1, ROUTE_R2 = range(6)


def _dot_nt(a, b):
    return lax.dot_general(a, b, (((1,), (1,)), ((), ())), preferred_element_type=F32)


def _store_token_major(ref, x):
    rows = x.shape[0]
    for s in range(TOKEN_SUBLANES):
        ref[pl.ds(s, rows, stride=TOKEN_SUBLANES), :] = x[:, s * LANES:(s + 1) * LANES]


def _load_token_major(ref, rows):
    return jnp.concatenate(
        [ref[pl.ds(s, rows, stride=TOKEN_SUBLANES), :] for s in range(TOKEN_SUBLANES)], axis=1)


def _merge_kernel(x_ref, ya_ref, gb_ref, yb_ref, wo_ref, ng_ref, wrh_ref, wrl_ref, br_ref,
                  h_ref, xn_ref, route_ref, route_t_ref, counts_ref, run_scr, wo_scr):
    @pl.when(pl.program_id(0) == 0)
    def _():
        run_scr[...] = jnp.zeros_like(run_scr)
        wo_scr[...] = wo_ref[...].astype(BF16)

    f = lambda r: r[...].astype(F32)
    mix = (f(ya_ref) + f(gb_ref) * f(yb_ref)).astype(BF16)
    h = x_ref[...] + _dot(mix, wo_scr[...])
    h_ref[...] = h
    xn = _rmsnorm(h, ng_ref[...])
    _store_token_major(xn_ref, xn)
    rows = xn.shape[0]

    x_hi = xn.astype(BF16)
    x_lo = (xn - x_hi.astype(F32)).astype(BF16)
    logits = (_dot_nt(wrh_ref[...], x_hi) + _dot_nt(wrh_ref[...], x_lo)
              + _dot_nt(wrl_ref[...], x_hi) + br_ref[...])
    unit = lax.broadcasted_iota(jnp.int32, logits.shape, 0).astype(F32)
    big = float(ROUTER_UNITS)
    neg_inf = -jnp.inf

    gl = jnp.where((unit >= GROUP_UNIT0) & (unit < GROUP_UNIT0 + N_GROUPS), logits, neg_inf)
    gmax = jnp.max(gl, axis=0, keepdims=True)
    g_w = 1.0 / jnp.sum(jnp.exp(gl - gmax), axis=0, keepdims=True)
    g_idx = jnp.min(jnp.where(gl == gmax, unit, big), axis=0, keepdims=True) - GROUP_UNIT0

    e0 = g_idx * EXPERTS_PER_GROUP
    el = jnp.where((unit >= e0) & (unit < e0 + EXPERTS_PER_GROUP), logits, neg_inf)
    m1 = jnp.max(el, axis=0, keepdims=True)
    i1 = jnp.min(jnp.where(el == m1, unit, big), axis=0, keepdims=True)
    el2 = jnp.where(unit == i1, neg_inf, el)
    m2 = jnp.max(el2, axis=0, keepdims=True)
    i2 = jnp.min(jnp.where(el2 == m2, unit, big), axis=0, keepdims=True)
    e2 = jnp.exp(m2 - m1)
    den = 1.0 + e2
    w1 = (1.0 / den) * g_w
    w2 = (e2 / den) * g_w

    hit1 = unit == i1
    hit2 = unit == i2
    onehot = jnp.where(hit1, 1.0, jnp.where(hit2, 1.0, 0.0))
    c_idx = lax.broadcasted_iota(jnp.int32, (rows, rows), 0)
    r_idx = lax.broadcasted_iota(jnp.int32, (rows, rows), 1)
    earlier = jnp.where(c_idx < r_idx, 1.0, 0.0).astype(BF16)
    prefix = run_scr[...] + _dot(onehot.astype(BF16), earlier)
    rank1 = jnp.sum(jnp.where(hit1, prefix, 0.0), axis=0, keepdims=True)
    rank2 = jnp.sum(jnp.where(hit2, prefix, 0.0), axis=0, keepdims=True)
    run_scr[...] = run_scr[...] + jnp.sum(onehot, axis=1, keepdims=True)
    counts_ref[...] = run_scr[...]

    route_t = jnp.concatenate(
        [i1, i2, w1, w2, rank1, rank2, jnp.zeros((ROUTER_LANES - 6, rows), F32)], axis=0)
    route_t_ref[0] = route_t[:ROUTE_ROWS]
    route_ref[...] = route_t.T


def _merge_call(x2, ya, gb, yb, w_out, norm_g, wr_hi, wr_lo, b_router):
    t, d = x2.shape
    rows = MERGE_ROWS
    assert t % rows == 0 and d == TOKEN_SUBLANES * LANES
    n_tiles = t // rows
    row_spec = pl.BlockSpec((rows, d), lambda i: (i, 0))
    const2 = lambda i: (0, 0)
    return pl.pallas_call(
        _merge_kernel,
        grid=(n_tiles,),
        in_specs=[row_spec, row_spec, row_spec, row_spec,
                  pl.BlockSpec((d, d), const2, pipeline_mode=pl.Buffered(1)),
                  pl.BlockSpec((1, d), const2),
                  pl.BlockSpec((ROUTER_UNITS, d), const2),
                  pl.BlockSpec((ROUTER_UNITS, d), const2),
                  pl.BlockSpec((ROUTER_UNITS, 1), const2)],
        out_specs=[row_spec,
                   pl.BlockSpec((rows * TOKEN_SUBLANES, LANES), lambda i: (i, 0)),
                   pl.BlockSpec((rows, ROUTER_LANES), lambda i: (i, 0)),
                   pl.BlockSpec((1, ROUTE_ROWS, rows), lambda i: (i, 0, 0)),
                   pl.BlockSpec((ROUTER_UNITS, 1), const2)],
        out_shape=[jax.ShapeDtypeStruct((t, d), F32),
                   jax.ShapeDtypeStruct((t * TOKEN_SUBLANES, LANES), F32),
                   jax.ShapeDtypeStruct((t, ROUTER_LANES), F32),
                   jax.ShapeDtypeStruct((n_tiles, ROUTE_ROWS, rows), F32),
                   jax.ShapeDtypeStruct((ROUTER_UNITS, 1), F32)],
        scratch_shapes=[pltpu.VMEM((ROUTER_UNITS, 1), F32), pltpu.VMEM((d, d), BF16)],
        compiler_params=pltpu.CompilerParams(
            dimension_semantics=("arbitrary",), vmem_limit_bytes=VMEM_LIMIT_BYTES),
        name="merge_outproj_router",
    )(x2, ya, gb, yb, w_out, norm_g, wr_hi, wr_lo, b_router)


def _token_rows(ref, token):
    return ref.at[pl.ds(pl.multiple_of(token * TOKEN_SUBLANES, TOKEN_SUBLANES), TOKEN_SUBLANES)]


def _dispatch_kernel(last_ref, nreal_ref, start_ref, off_ref, cnt_ref, lp1_ref, lp2_ref, xn_ref,
                     xs_hbm, zero_scr, stage, sems):
    rows = lp1_ref.shape[2]
    tile = EXPERT_ROWS
    n_tiles = xs_hbm.shape[0] // (tile * TOKEN_SUBLANES)
    j = pl.program_id(0)
    slot = j % 2
    sem = sems.at[0]

    def wait_step(s):
        for _ in range(2):
            pltpu.make_async_copy(
                xn_ref, xs_hbm.at[pl.ds(0, rows * TOKEN_SUBLANES)], sems.at[s]).wait()

    def zero_tile(j):
        start = pl.multiple_of(j * (tile * TOKEN_SUBLANES), tile * TOKEN_SUBLANES)
        return pltpu.make_async_copy(
            zero_scr, xs_hbm.at[pl.ds(start, tile * TOKEN_SUBLANES)], sem)

    @pl.when(pl.program_id(0) == 0)
    def _():
        zero_scr[...] = jnp.zeros_like(zero_scr)
        for e in range(N_EXPERTS):
            @pl.when(last_ref[e] >= 0)
            def _():
                zero_tile(last_ref[e]).start()

        def tail_start(j, carry):
            zero_tile(j).start()
            return carry

        lax.fori_loop(nreal_ref[0], n_tiles, tail_start, 0)

        for e in range(N_EXPERTS):
            @pl.when(last_ref[e] >= 0)
            def _():
                zero_tile(0).wait()

        def tail_wait(j, carry):
            zero_tile(0).wait()
            return carry

        lax.fori_loop(nreal_ref[0], n_tiles, tail_wait, 0)

    @pl.when(j >= 2)
    def _():
        wait_step(slot)

    buf = stage.at[slot]

    def place(g, carry):
        for u in range(DMA_UNROLL):
            r = g * DMA_UNROLL + u
            tok = _token_rows(xn_ref, r)[...]
            _token_rows(buf, lp1_ref[0, 0, r])[...] = tok
            _token_rows(buf, lp2_ref[0, 0, r])[...] = tok
        return carry

    lax.fori_loop(0, rows // DMA_UNROLL, place, 0)

    def run(e, carry):
        idx = j * N_EXPERTS + e
        n = cnt_ref[idx]
        src0 = off_ref[idx]
        dst0 = start_ref[idx]
        for b in reversed(range(rows.bit_length())):
            size = 1 << b
            done = n & ~(2 * size - 1)

            @pl.when((n & size) != 0)
            def _():
                src = buf.at[pl.ds(pl.multiple_of((src0 + done) * TOKEN_SUBLANES, TOKEN_SUBLANES),
                                   size * TOKEN_SUBLANES)]
                dst = xs_hbm.at[pl.ds(pl.multiple_of((dst0 + done) * TOKEN_SUBLANES,
                                                     TOKEN_SUBLANES), size * TOKEN_SUBLANES)]
                pltpu.make_async_copy(src, dst, sems.at[slot]).start(b % 2)
        return carry

    lax.fori_loop(0, N_EXPERTS, run, 0)

    @pl.when(j == pl.num_programs(0) - 1)
    def _():
        wait_step(slot)

        @pl.when(j >= 1)
        def _():
            wait_step(1 - slot)


def _dispatch_call(last_tile, n_real, run_start, run_off, run_cnt, lpos1, lpos2, xn,
                   n_sorted_rows):
    n_steps, _, rows = lpos1.shape
    smem_row = pl.BlockSpec((1, 1, rows), lambda i, *_: (i, 0, 0), memory_space=pltpu.SMEM)
    return pl.pallas_call(
        _dispatch_kernel,
        grid_spec=pltpu.PrefetchScalarGridSpec(
            num_scalar_prefetch=5,
            grid=(n_steps,),
            in_specs=[smem_row, smem_row,
                      pl.BlockSpec((rows * TOKEN_SUBLANES, LANES), lambda i, *_: (i, 0))],
            out_specs=pl.BlockSpec(memory_space=pl.ANY),
            scratch_shapes=[pltpu.VMEM((EXPERT_ROWS * TOKEN_SUBLANES, LANES), F32),
                            pltpu.VMEM((2, 2 * rows * TOKEN_SUBLANES, LANES), F32),
                            pltpu.SemaphoreType.DMA((2,))],
        ),
        out_shape=jax.ShapeDtypeStruct((n_sorted_rows * TOKEN_SUBLANES, LANES), F32),
        compiler_params=pltpu.CompilerParams(
            dimension_semantics=("arbitrary",), vmem_limit_bytes=VMEM_LIMIT_BYTES),
        name="moe_dispatch",
    )(last_tile, n_real, run_start, run_off, run_cnt, lpos1, lpos2, xn)


EXPERT_IN_SLOTS = 6
EXPERT_OUT_SLOTS = 4


def _expert_kernel(first_ref, end_ref, nreal_ref, xs_hbm, w1_ref, w3_ref, w2_ref, ys_hbm,
                   w1_scr, w3_scr, w2_scr, xbuf, ybuf, in_sems, out_sems):
    e = pl.program_id(0)
    tile_rows = EXPERT_ROWS * TOKEN_SUBLANES
    n_real = nreal_ref[0]
    n_tiles = xs_hbm.shape[0] // tile_rows

    def tile_of(ref, t):
        return ref.at[pl.ds(pl.multiple_of(t * tile_rows, tile_rows), tile_rows)]

    def in_copy(t):
        slot = t % EXPERT_IN_SLOTS
        return pltpu.make_async_copy(tile_of(xs_hbm, t), xbuf.at[slot], in_sems.at[slot])

    def out_copy(t):
        slot = t % EXPERT_OUT_SLOTS
        return pltpu.make_async_copy(ybuf.at[slot], tile_of(ys_hbm, t), out_sems.at[slot])

    @pl.when(e == 0)
    def _():
        for t in range(EXPERT_IN_SLOTS - 1):
            @pl.when(t < n_real)
            def _():
                in_copy(t).start()

    w1_scr[...] = w1_ref[...].astype(BF16)
    w3_scr[...] = w3_ref[...].astype(BF16)
    w2_scr[...] = w2_ref[...].astype(BF16)

    def tile(t, carry):
        ahead = t + EXPERT_IN_SLOTS - 1

        @pl.when(ahead < n_real)
        def _():
            in_copy(ahead).start()

        in_copy(t).wait()

        @pl.when(t >= EXPERT_OUT_SLOTS)
        def _():
            out_copy(t - EXPERT_OUT_SLOTS).wait()

        x = _load_token_major(xbuf.at[t % EXPERT_IN_SLOTS], EXPERT_ROWS).astype(BF16)
        a = _dot(x, w1_scr[...])
        b = _dot(x, w3_scr[...])
        hid = (a * _sigmoid(a)) * b
        _store_token_major(ybuf.at[t % EXPERT_OUT_SLOTS], _dot(hid.astype(BF16), w2_scr[...]))
        out_copy(t).start()
        return carry

    lax.fori_loop(first_ref[e], end_ref[e], tile, 0)

    @pl.when(e == pl.num_programs(0) - 1)
    def _():
        for back in range(EXPERT_OUT_SLOTS, 0, -1):
            @pl.when(n_real - back >= 0)
            def _():
                out_copy(n_real - back).wait()
        ybuf[0] = jnp.zeros(ybuf.shape[1:], F32)

        def tail_start(t, carry):
            pltpu.make_async_copy(ybuf.at[0], tile_of(ys_hbm, t), out_sems.at[0]).start()
            return carry

        def tail_wait(t, carry):
            pltpu.make_async_copy(ybuf.at[0], tile_of(ys_hbm, t), out_sems.at[0]).wait()
            return carry

        lax.fori_loop(n_real, n_tiles, tail_start, 0)
        lax.fori_loop(n_real, n_tiles, tail_wait, 0)


def _expert_call(first_tile, end_tile, n_real, xs, w1, w3, w2):
    n_experts, d, d_expert = w1.shape
    tile_rows = EXPERT_ROWS * TOKEN_SUBLANES
    per_expert = lambda e, f, n, nr: (e, 0, 0)
    return pl.pallas_call(
        _expert_kernel,
        grid_spec=pltpu.PrefetchScalarGridSpec(
            num_scalar_prefetch=3,
            grid=(n_experts,),
            in_specs=[pl.BlockSpec(memory_space=pl.ANY),
                      pl.BlockSpec((None, d, d_expert), per_expert),
                      pl.BlockSpec((None, d, d_expert), per_expert),
                      pl.BlockSpec((None, d_expert, d), per_expert)],
            out_specs=pl.BlockSpec(memory_space=pl.ANY),
            scratch_shapes=[pltpu.VMEM((d, d_expert), BF16), pltpu.VMEM((d, d_expert), BF16),
                            pltpu.VMEM((d_expert, d), BF16),
                            pltpu.VMEM((EXPERT_IN_SLOTS, tile_rows, LANES), F32),
                            pltpu.VMEM((EXPERT_OUT_SLOTS, tile_rows, LANES), F32),
                            pltpu.SemaphoreType.DMA((EXPERT_IN_SLOTS,)),
                            pltpu.SemaphoreType.DMA((EXPERT_OUT_SLOTS,))],
        ),
        out_shape=jax.ShapeDtypeStruct(xs.shape, F32),
        compiler_params=pltpu.CompilerParams(
            dimension_semantics=("arbitrary",), vmem_limit_bytes=VMEM_LIMIT_BYTES),
        name="moe_experts",
    )(first_tile, end_tile, n_real, xs, w1, w3, w2)


def _combine_kernel(start_ref, off_ref, cnt_ref, lp1_ref, lp2_ref, ys_hbm, h_ref, route_ref,
                    ng_ref, out_ref, stage, buf, sems):
    i = pl.program_id(0)
    n_steps = pl.num_programs(0)
    rows = h_ref.shape[0]

    def fetch(step, slot):
        def run(e, carry):
            idx = step * N_EXPERTS + e
            n = cnt_ref[idx]
            src0 = start_ref[idx]
            dst0 = off_ref[idx]
            for b in reversed(range(rows.bit_length())):
                size = 1 << b
                done = n & ~(2 * size - 1)

                @pl.when((n & size) != 0)
                def _():
                    src = ys_hbm.at[pl.ds(pl.multiple_of((src0 + done) * TOKEN_SUBLANES,
                                                         TOKEN_SUBLANES), size * TOKEN_SUBLANES)]
                    dst = stage.at[slot, pl.ds(pl.multiple_of((dst0 + done) * TOKEN_SUBLANES,
                                                              TOKEN_SUBLANES),
                                               size * TOKEN_SUBLANES)]
                    pltpu.make_async_copy(src, dst, sems.at[slot]).start(b % 2)
            return carry

        lax.fori_loop(0, N_EXPERTS, run, 0)

    @pl.when(i == 0)
    def _():
        fetch(0, 0)

    @pl.when(i + 1 < n_steps)
    def _():
        fetch(i + 1, (i + 1) % 2)

    slot = i % 2
    for which in range(2):
        pltpu.make_async_copy(ys_hbm.at[pl.ds(0, rows * TOKEN_SUBLANES)],
                              buf.at[which], sems.at[slot]).wait()

    src = stage.at[slot]

    def place(g, carry):
        for u in range(DMA_UNROLL):
            r = g * DMA_UNROLL + u
            _token_rows(buf.at[0], r)[...] = _token_rows(src, lp1_ref[0, 0, r])[...]
            _token_rows(buf.at[1], r)[...] = _token_rows(src, lp2_ref[0, 0, r])[...]
        return carry

    lax.fori_loop(0, rows // DMA_UNROLL, place, 0)

    chunk = COMBINE_ROWS
    for c in range(rows // chunk):
        r0 = c * chunk
        route = route_ref[r0:r0 + chunk, :]
        w1 = route[:, ROUTE_W1:ROUTE_W1 + 1]
        w2 = route[:, ROUTE_W2:ROUTE_W2 + 1]
        tiles = pl.ds(r0 * TOKEN_SUBLANES, chunk * TOKEN_SUBLANES)
        y = (h_ref[r0:r0 + chunk, :] + w1 * _load_token_major(buf.at[0, tiles], chunk)
             + w2 * _load_token_major(buf.at[1, tiles], chunk))
        out_ref[r0:r0 + chunk, :] = _rmsnorm(y, ng_ref[...])


def _combine_call(run_start, run_off, run_cnt, lpos1, lpos2, ys, h, route, norm_g):
    t, d = h.shape
    n_steps, _, rows = lpos1.shape
    row_spec = pl.BlockSpec((rows, d), lambda i, *_: (i, 0))
    smem_row = pl.BlockSpec((1, 1, rows), lambda i, *_: (i, 0, 0), memory_space=pltpu.SMEM)
    return pl.pallas_call(
        _combine_kernel,
        grid_spec=pltpu.PrefetchScalarGridSpec(
            num_scalar_prefetch=3,
            grid=(n_steps,),
            in_specs=[smem_row, smem_row,
                      pl.BlockSpec(memory_space=pl.ANY),
                      row_spec,
                      pl.BlockSpec((rows, ROUTER_LANES), lambda i, *_: (i, 0)),
                      pl.BlockSpec((1, d), lambda i, *_: (0, 0))],
            out_specs=row_spec,
            scratch_shapes=[pltpu.VMEM((2, 2 * rows * TOKEN_SUBLANES, LANES), F32),
                            pltpu.VMEM((2, rows * TOKEN_SUBLANES, LANES), F32),
                            pltpu.SemaphoreType.DMA((2,))],
        ),
        out_shape=jax.ShapeDtypeStruct((t, d), F32),
        compiler_params=pltpu.CompilerParams(
            dimension_semantics=("arbitrary",), vmem_limit_bytes=VMEM_LIMIT_BYTES),
        name="moe_combine",
    )(run_start, run_off, run_cnt, lpos1, lpos2, ys, h, route, norm_g)


def _sparse_moe(xn, route, route_t, counts, h, w1, w3, w2, norm_g):
    t = h.shape[0]
    tile = EXPERT_ROWS
    n_tiles = (2 * t) // tile + N_EXPERTS
    expert = jnp.arange(N_EXPERTS, dtype=jnp.int32)
    counts = counts[:N_EXPERTS, 0].astype(jnp.int32)
    group_tiles = (counts + tile - 1) // tile
    end_tile = jnp.sum(jnp.where(expert[None, :] <= expert[:, None], group_tiles[None, :], 0), axis=1)
    first_tile = end_tile - group_tiles
    n_real = end_tile[-1:]
    last_tile = jnp.where(group_tiles > 0, end_tile - 1, -1)

    def positions(e_row, r_row):
        e = route_t[:, e_row, :].astype(jnp.int32)
        start = jnp.zeros_like(e)
        for k in range(N_EXPERTS):
            start = jnp.where(e == k, first_tile[k] * tile, start)
        return start + route_t[:, r_row, :].astype(jnp.int32)

    pos1 = positions(ROUTE_E1, ROUTE_R1)
    pos2 = positions(ROUTE_E2, ROUTE_R2)

    e1 = route_t[:, ROUTE_E1, :].astype(jnp.int32)
    e2 = route_t[:, ROUTE_E2, :].astype(jnp.int32)
    hit = (e1[:, None, :] == expert[None, :, None]) | (e2[:, None, :] == expert[None, :, None])
    run_cnt = jnp.sum(hit.astype(jnp.int32), axis=2)
    tile_prefix = jnp.cumsum(run_cnt, axis=0) - run_cnt
    run_off = jnp.cumsum(run_cnt, axis=1) - run_cnt
    run_start = first_tile[None, :] * tile + tile_prefix
    shift = run_off - tile_prefix - first_tile[None, :] * tile

    def local(pos, e):
        delta = jnp.zeros_like(e)
        for k in range(N_EXPERTS):
            delta = jnp.where(e == k, shift[:, k:k + 1], delta)
        return (pos + delta)[:, None, :]

    runs = (run_start.reshape(-1), run_off.reshape(-1), run_cnt.reshape(-1),
            local(pos1, e1), local(pos2, e2))
    xs = _dispatch_call(last_tile, n_real, *runs, xn, n_tiles * tile)
    ys = _expert_call(first_tile, end_tile, n_real, xs, w1, w3, w2)
    return _combine_call(*runs, ys, h, route, norm_g)


def _layer(h, norm_mix_g, w_in, b_gates, gmlp_ln_g, gmlp_ln_b, w_spatial, b_spatial, bias_tiles,
           w_out, norm_ffn_g, w_group_router, b_group_router, w_expert_router, b_expert_router,
           w1, w3, w2, norm_out_g):
    b, s, d = h.shape
    t = b * s
    nb = s // MOBA_BLOCK
    x2 = h.reshape(t, d)
    row = lambda v: v.reshape(1, -1)

    ya, qT, k, vT, gb, kmean = _proj_call(
        x2, b, row(norm_mix_g), w_in, row(b_gates), row(gmlp_ln_g), row(gmlp_ln_b),
        w_spatial, b_spatial[:, :, None])

    km = jnp.transpose(kmean.reshape(b, nb, ATT_HEADS, HEAD_DIM), (0, 2, 1, 3))
    km_hi = km.astype(BF16)
    km_lo = (km - km_hi.astype(F32)).astype(BF16)
    yb = _attn_call(qT, k, vT, km_hi, km_lo, bias_tiles)

    w_router = jnp.concatenate(
        [jnp.transpose(w_expert_router, (0, 2, 1)).reshape(N_EXPERTS, d), w_group_router.T,
         jnp.zeros((ROUTER_UNITS - N_EXPERTS - N_GROUPS, d), F32)], axis=0)
    b_router = jnp.concatenate(
        [b_expert_router.reshape(-1), b_group_router,
         jnp.zeros((ROUTER_UNITS - N_EXPERTS - N_GROUPS,), F32)]).reshape(ROUTER_UNITS, 1)
    wr_hi = w_router.astype(BF16)
    wr_lo = (w_router - wr_hi.astype(F32)).astype(BF16)
    h2, xn, route, route_t, counts = _merge_call(
        x2, ya, gb, yb, w_out, row(norm_ffn_g), wr_hi, wr_lo, b_router)

    out = _sparse_moe(xn, route, route_t, counts, h2, w1, w3, w2, row(norm_out_g))
    return out.reshape(b, s, d)


def kernel(x, norm_mix_g, w_in, b_gates, gmlp_ln_g, gmlp_ln_b, w_spatial, b_spatial, rel_bias, w_out, norm_ffn_g, w_group_router, b_group_router, w_expert_router, b_expert_router, w1, w3, w2, norm_final_g):
    depth = w_in.shape[0]
    assert depth == 1, "the final rmsnorm is fused into the last layer's combine kernel"
    bias_tiles = _bias_call(rel_bias)
    return _layer(x, norm_mix_g[0], w_in[0], b_gates[0], gmlp_ln_g[0], gmlp_ln_b[0], w_spatial[0],
                  b_spatial[0], bias_tiles, w_out[0], norm_ffn_g[0], w_group_router[0],
                  b_group_router[0], w_expert_router[0], b_expert_router[0], w1[0], w3[0], w2[0],
                  norm_final_g)
```
